```python
import jax
import jax.numpy as jnp
from jax import lax
import numpy as np

D_MODEL = 4096
BATCH = 1
SEQ = 16384
DEPTH = 2

GRID_W = 64
CTX_LEN = 256
HEAD_DIM = 128
NA_HEADS = 16
GQA_Q_HEADS = 16
GQA_KV_HEADS = 4
NA_WIDTH = NA_HEADS * HEAD_DIM
GQA_Q_WIDTH = GQA_Q_HEADS * HEAD_DIM
GQA_KV_WIDTH = GQA_KV_HEADS * HEAD_DIM
Q_COLS = NA_WIDTH + GQA_Q_WIDTH
IN_COLS = Q_COLS + 2 * NA_WIDTH + 2 * GQA_KV_WIDTH
MIX_WIDTH = NA_WIDTH + GQA_Q_WIDTH
NA_WIN_H = 8
NA_WIN_W = 16
NA_QCB = 16
NA_KCB = 32
Q_BLOCK = 128
ROPE_THETA = 10000.0
AXIS_DIM = HEAD_DIM // 2
POOL_WINDOWS = (2, 4, 8, 16)
POOL_GROUPS = 4
N_EXPERTS = 64
N_EXPERT_GROUPS = 8
TOPK_GROUPS = 4
TOP_K = 8
EXPERT_FF = 256
SHARED_FF = 1024
ROUTED_SCALE = 2.5
EXPERT_CHUNK = 8
N_MOD = 6
RMS_EPS = 1e-6
N_ATTN_LAYERS = (DEPTH + 1) // 2
N_POOL_LAYERS = DEPTH // 2

kernel_name = 'hybrid_na_gqa_pool_moe_dit'


def _rms_norm(x, gain):
    xf = x.astype(jnp.float32)
    y = xf * lax.rsqrt(jnp.mean(xf * xf, axis=-1, keepdims=True) + RMS_EPS)
    return (y * gain.astype(jnp.float32)).astype(x.dtype)


def _modulate(h, shift, scale):
    return h * (1 + scale) + shift


def _adaln(cond, w_mod, b_mod):
    return jnp.split(jax.nn.silu(cond) @ w_mod + b_mod, N_MOD, axis=-1)


def _axial_rope_tables(n):
    t = jnp.arange(n)
    row = (t // GRID_W).astype(jnp.float32)
    col = (t % GRID_W).astype(jnp.float32)
    inv = ROPE_THETA ** (-jnp.arange(0, AXIS_DIM, 2, dtype=jnp.float32) / AXIS_DIM)
    ang = jnp.stack([row[:, None] * inv, col[:, None] * inv], axis=0)
    return jnp.cos(ang), jnp.sin(ang)


def _apply_axial_rope(x, cos, sin):
    B, n, H, dh = x.shape
    xa = x.astype(jnp.float32).reshape(B, n, H, 2, 2, AXIS_DIM // 2)
    c = cos.transpose(1, 0, 2)[None, :, None]
    s = sin.transpose(1, 0, 2)[None, :, None]
    x1, x2 = xa[..., 0, :], xa[..., 1, :]
    out = jnp.stack([x1 * c - x2 * s, x1 * s + x2 * c], axis=-2)
    return out.reshape(B, n, H, dh).astype(x.dtype)


def _block_attention(q, k, v):
    B, L, Hq, dh = q.shape
    Hkv = k.shape[2]
    G = Hq // Hkv
    scale = dh ** -0.5
    qb = q.reshape(B, L // Q_BLOCK, Q_BLOCK, Hkv, G, dh).transpose(1, 0, 2, 3, 4, 5)

    def one_block(q_blk):
        s = jnp.einsum('bqhgd,bkhd->bhgqk', q_blk, k).astype(jnp.float32) * scale
        p = jax.nn.softmax(s, axis=-1).astype(v.dtype)
        return jnp.einsum('bhgqk,bkhd->bqhgd', p, v)

    o = lax.map(one_block, qb)
    return o.transpose(1, 0, 2, 3, 4, 5).reshape(B, L, Hq * dh)


def _na_column_tables():
    q_cols = np.arange(GRID_W).reshape(GRID_W // NA_QCB, NA_QCB)
    win_start = np.clip(q_cols - NA_WIN_W // 2, 0, GRID_W - NA_WIN_W)
    k_start = np.clip(win_start[:, 0], 0, GRID_W - NA_KCB)
    k_cols = k_start[:, None] + np.arange(NA_KCB)[None, :]
    rel = k_cols[:, None, :] - q_cols[:, :, None]
    in_win = (k_cols[:, None, :] >= win_start[:, :, None]) & (k_cols[:, None, :] < win_start[:, :, None] + NA_WIN_W)
    rel_idx = np.clip(rel + NA_WIN_W - 1, 0, 2 * NA_WIN_W - 2)
    return k_cols, rel_idx, in_win


def _neighbourhood_attention(q, k, v, k_ctx, v_ctx, rpb):
    B, L, H, dh = q.shape
    rows = L // GRID_W
    kh = min(NA_WIN_H, rows)
    ncb = GRID_W // NA_QCB
    nk = kh * NA_KCB
    k_cols, rel_idx, in_win = _na_column_tables()
    mask = jnp.asarray(np.broadcast_to(in_win[:, :, None, :], (ncb, NA_QCB, kh, NA_KCB)).reshape(ncb, NA_QCB, nk))
    qg = q.reshape(B, rows, GRID_W, H, dh)
    kg = k.reshape(B, rows, GRID_W, H, dh)
    vg = v.reshape(B, rows, GRID_W, H, dh)
    scale = dh ** -0.5

    def row_block(r):
        rs = jnp.clip(r - kh // 2, 0, rows - kh)
        q_blk = lax.dynamic_index_in_dim(qg, r, axis=1, keepdims=False).reshape(B, ncb, NA_QCB, H, dh)

        def gather(t):
            t = lax.dynamic_slice_in_dim(t, rs, kh, axis=1)[:, :, k_cols]
            return t.transpose(0, 2, 1, 3, 4, 5).reshape(B, ncb, nk, H, dh)

        k_blk, v_blk = gather(kg), gather(vg)
        rel_row = rs + jnp.arange(kh) - r + (NA_WIN_H - 1)
        bias = rpb[:, rel_row][:, :, rel_idx]
        bias = bias.transpose(0, 2, 3, 1, 4).reshape(H, ncb, NA_QCB, nk).astype(jnp.float32)
        s_loc = jnp.einsum('bnqhd,bnkhd->bhnqk', q_blk, k_blk).astype(jnp.float32) * scale + bias
        s_loc = jnp.where(mask, s_loc, -jnp.inf)
        s_ctx = jnp.einsum('bnqhd,bkhd->bhnqk', q_blk, k_ctx).astype(jnp.float32) * scale
        p = jax.nn.softmax(jnp.concatenate([s_loc, s_ctx], axis=-1), axis=-1).astype(v.dtype)
        o = (jnp.einsum('bhnqk,bnkhd->bnqhd', p[..., :nk], v_blk)
             + jnp.einsum('bhnqk,bkhd->bnqhd', p[..., nk:], v_ctx))
        return o.reshape(B, GRID_W, H * dh)

    o = lax.map(row_block, jnp.arange(rows))
    return o.transpose(1, 0, 2, 3).reshape(B, L, H * dh)


def _hybrid_attention(h_x, h_c, w_in, w_out, rpb, q_gain, k_gain, with_ctx_out):
    B, L, _ = h_x.shape
    Lc = h_c.shape[1]
    kv_splits = [NA_WIDTH, 2 * NA_WIDTH, 2 * NA_WIDTH + GQA_KV_WIDTH]

    def heads(t, n, h):
        return t.reshape(B, n, h, HEAD_DIM)

    def split_kv(p, n):
        a_k, a_v, b_k, b_v = jnp.split(p, kv_splits, axis=-1)
        return (heads(a_k, n, NA_HEADS), heads(a_v, n, NA_HEADS),
                _rms_norm(heads(b_k, n, GQA_KV_HEADS), k_gain), heads(b_v, n, GQA_KV_HEADS))

    def split_q(p, n):
        a_q, b_q = jnp.split(p, [NA_WIDTH], axis=-1)
        return heads(a_q, n, NA_HEADS), _rms_norm(heads(b_q, n, GQA_Q_HEADS), q_gain)

    ca_k, ca_v, cb_k, cb_v = split_kv(h_c @ w_in[:, Q_COLS:], Lc)

    p_x = h_x @ w_in
    xa_q, xb_q = split_q(p_x[..., :Q_COLS], L)
    xa_k, xa_v, xb_k, xb_v = split_kv(p_x[..., Q_COLS:], L)
    cos, sin = _axial_rope_tables(L)
    xb_q = _apply_axial_rope(xb_q, cos, sin)
    xb_k = _apply_axial_rope(xb_k, cos, sin)

    o_a = _neighbourhood_attention(xa_q, xa_k, xa_v, ca_k, ca_v, rpb)
    o_b = _block_attention(xb_q, jnp.concatenate([xb_k, cb_k], axis=1), jnp.concatenate([xb_v, cb_v], axis=1))
    y_x = jnp.concatenate([o_a, o_b], axis=-1) @ w_out
    if not with_ctx_out:
        return y_x, None
    ca_q, cb_q = split_q(h_c @ w_in[:, :Q_COLS], Lc)
    o_c = jnp.concatenate([_block_attention(ca_q, ca_k, ca_v), _block_attention(cb_q, cb_k, cb_v)], axis=-1)
    return y_x, o_c @ w_out


def _multiscale_pool(h, w_pool, pool_scale):
    B, N, D = h.shape
    dg = D // POOL_GROUPS
    hf = h.astype(jnp.float32).reshape(B, N, POOL_GROUPS, dg)
    csum = jnp.pad(jnp.cumsum(hf, axis=1), ((0, 0), (1, 0), (0, 0), (0, 0)))
    t = jnp.arange(N)
    diffs = []
    for g, w in enumerate(POOL_WINDOWS):
        lo = jnp.clip(t - w // 2, 0, N)
        hi = jnp.clip(t + (w - w // 2), 0, N)
        mean = (csum[:, hi, g] - csum[:, lo, g]) / (hi - lo).astype(jnp.float32)[None, :, None]
        diffs.append(mean - hf[:, :, g])
    d = jnp.stack(diffs, axis=2).astype(h.dtype)
    y = jnp.einsum('bngc,gce->bnge', d, w_pool).reshape(B, N, D)
    return y * pool_scale


def _moe(h, w_router, router_bias, w_gate, w_up, w_down, ws_gate, ws_up, ws_down):
    shape = h.shape
    t = h.reshape(-1, shape[-1])
    T = t.shape[0]
    scores = jax.nn.sigmoid((t @ w_router).astype(jnp.float32))
    biased = scores + router_bias.astype(jnp.float32)
    per_group = N_EXPERTS // N_EXPERT_GROUPS
    group_score = lax.top_k(biased.reshape(T, N_EXPERT_GROUPS, per_group), 2)[0].sum(axis=-1)
    _, top_groups = lax.top_k(group_score, TOPK_GROUPS)
    rows = jnp.arange(T)[:, None]
    group_keep = jnp.zeros((T, N_EXPERT_GROUPS), dtype=bool).at[rows, top_groups].set(True)
    expert_keep = jnp.repeat(group_keep, per_group, axis=1)
    _, top_experts = lax.top_k(jnp.where(expert_keep, biased, -jnp.inf), TOP_K)
    w = jnp.take_along_axis(scores, top_experts, axis=-1)
    w = w / jnp.sum(w, axis=-1, keepdims=True) * ROUTED_SCALE
    gates = jnp.zeros_like(scores).at[rows, top_experts].set(w).astype(t.dtype)
    out = (jax.nn.silu(t @ ws_gate) * (t @ ws_up)) @ ws_down
    for e0 in range(0, N_EXPERTS, EXPERT_CHUNK):
        sl = slice(e0, e0 + EXPERT_CHUNK)
        a = jax.nn.silu(jnp.einsum('td,edf->tef', t, w_gate[sl])) * jnp.einsum('td,edf->tef', t, w_up[sl])
        out = out + jnp.einsum('tef,efd->td', a * gates[:, sl, None], w_down[sl])
    return out.reshape(shape)


def setup_inputs(seed: int = 0) -> dict:
    key = jax.random.key(seed)
    ks = jax.random.split(key, 24)
    f32 = jnp.float32
    D = D_MODEL
    dg = D // POOL_GROUPS

    def nrm(k, shape, fan_in, gain=1.0):
        return jax.random.normal(k, shape, f32) * (gain * fan_in ** -0.5)

    def ones_noise(k, shape, s=0.02):
        return 1.0 + s * jax.random.normal(k, shape, f32)

    return {
        'x': jax.random.normal(ks[0], (BATCH, SEQ, D), f32),
        'c': jax.random.normal(ks[1], (BATCH, D), f32),
        'ctx': jax.random.normal(ks[2], (BATCH, CTX_LEN, D), f32),
        'c_ctx': jax.random.normal(ks[3], (D,), f32),
        'w_mod': nrm(ks[4], (DEPTH, D, N_MOD * D), D, 0.5),
        'b_mod': 0.02 * jax.random.normal(ks[5], (DEPTH, N_MOD * D), f32),
        'g_mix': ones_noise(ks[6], (DEPTH, D)),
        'g_ffn': ones_noise(ks[7], (DEPTH, D)),
        'attn_w_in': nrm(ks[8], (N_ATTN_LAYERS, D, IN_COLS), D),
        'attn_w_out': nrm(ks[9], (N_ATTN_LAYERS, MIX_WIDTH, D), MIX_WIDTH),
        'na_rpb': 0.1 * jax.random.normal(ks[10], (N_ATTN_LAYERS, NA_HEADS, 2 * NA_WIN_H - 1, 2 * NA_WIN_W - 1), f32),
        'q_gain': ones_noise(ks[11], (N_ATTN_LAYERS, HEAD_DIM)),
        'k_gain': ones_noise(ks[12], (N_ATTN_LAYERS, HEAD_DIM)),
        'pool_w': nrm(ks[13], (N_POOL_LAYERS, POOL_GROUPS, dg, dg), dg),
        'pool_scale': ones_noise(ks[14], (N_POOL_LAYERS, D), 0.1),
        'moe_w_router': nrm(ks[15], (DEPTH, D, N_EXPERTS), D),
        'moe_router_bias': 0.01 * jax.random.normal(ks[16], (DEPTH, N_EXPERTS), f32),
        'moe_w_gate': nrm(ks[17], (DEPTH, N_EXPERTS, D, EXPERT_FF), D),
        'moe_w_up': nrm(ks[18], (DEPTH, N_EXPERTS, D, EXPERT_FF), D),
        'moe_w_down': nrm(ks[19], (DEPTH, N_EXPERTS, EXPERT_FF, D), EXPERT_FF),
        'moe_ws_gate': nrm(ks[20], (DEPTH, D, SHARED_FF), D),
        'moe_ws_up': nrm(ks[21], (DEPTH, D, SHARED_FF), D),
        'moe_ws_down': nrm(ks[22], (DEPTH, SHARED_FF, D), SHARED_FF),
        'g_final': ones_noise(ks[23], (D,)),
    }


def reference(x, c, ctx, c_ctx, w_mod, b_mod, g_mix, g_ffn, attn_w_in, attn_w_out, na_rpb, q_gain, k_gain,
              pool_w, pool_scale, moe_w_router, moe_router_bias, moe_w_gate, moe_w_up, moe_w_down,
              moe_ws_gate, moe_ws_up, moe_ws_down, g_final):
    h_c = ctx
    for i in range(DEPTH):
        j = i // 2
        is_attn = i % 2 == 0
        ctx_live = any(l % 2 == 0 for l in range(i + 1, DEPTH))

        def channel_mix(h):
            return _moe(h, moe_w_router[i], moe_router_bias[i], moe_w_gate[i], moe_w_up[i], moe_w_down[i],
                        moe_ws_gate[i], moe_ws_up[i], moe_ws_down[i])

        shift1, scale1, gate1, shift2, scale2, gate2 = [m[:, None, :] for m in _adaln(c, w_mod[i], b_mod[i])]
        a_x = _modulate(_rms_norm(x, g_mix[i]), shift1, scale1)
        if is_attn or ctx_live:
            cshift1, cscale1, cgate1, cshift2, cscale2, cgate2 = _adaln(c_ctx, w_mod[i], b_mod[i])
            a_c = _modulate(_rms_norm(h_c, g_mix[i]), cshift1, cscale1)
        if is_attn:
            y_x, y_c = _hybrid_attention(a_x, a_c, attn_w_in[j], attn_w_out[j], na_rpb[j], q_gain[j], k_gain[j],
                                         ctx_live)
        else:
            y_x = _multiscale_pool(a_x, pool_w[j], pool_scale[j])
            y_c = _multiscale_pool(a_c, pool_w[j], pool_scale[j]) if ctx_live else None
        x = x + gate1 * y_x
        x = x + gate2 * channel_mix(_modulate(_rms_norm(x, g_ffn[i]), shift2, scale2))
        if ctx_live:
            h_c = h_c + cgate1 * y_c
            h_c = h_c + cgate2 * channel_mix(_modulate(_rms_norm(h_c, g_ffn[i]), cshift2, cscale2))
    return _rms_norm(x, g_final)
```

```python
import functools

import numpy as np
import jax
import jax.numpy as jnp
from jax import lax
from jax.experimental import pallas as pl
from jax.experimental.pallas import tpu as pltpu

GRID_W = 64
HEAD_DIM = 128
NA_HEADS = 16
GQA_Q_HEADS = 16
GQA_KV_HEADS = 4
NA_WIN_H = 8
NA_WIN_W = 16
ROPE_THETA = 10000.0
POOL_WINDOWS = (2, 4, 8, 16)
N_EXPERT_GROUPS = 8
TOPK_GROUPS = 4
TOP_K = 8
ROUTED_SCALE = 2.5
N_MOD = 6
RMS_EPS = 1e-6

_F32 = jnp.float32
_BF16 = jnp.bfloat16
_NEG = -1e30
_VMEM_LIMIT_BYTES = 56 * 1024 * 1024


def _cparams(*sem):
    return pltpu.CompilerParams(dimension_semantics=sem, vmem_limit_bytes=_VMEM_LIMIT_BYTES)


def _tile(n, target, mult=8):
    t = min(n, target)
    while t > mult and (n % t or t % mult):
        t -= mult
    assert n % t == 0, (n, target, mult)
    return t


def _silu(x):
    return x / (1.0 + jnp.exp(-x))


def _adaln_kernel(c_ref, w_ref, b_ref, o_ref, *, kc):
    n_rows, d, _ = c_ref.shape
    bn = w_ref.shape[1]
    accs = [jnp.zeros((1, bn), _F32) for _ in range(n_rows)]
    for k0 in range(0, d, kc):
        w = w_ref[k0:k0 + kc, :]
        for r in range(n_rows):
            cc = _silu(c_ref[r, k0:k0 + kc, :])
            accs[r] = accs[r] + jnp.sum(cc * w, axis=0, keepdims=True)
    o_ref[...] = jnp.concatenate(accs, axis=0) + b_ref[...]


def _adaln(conds, w_mod, b_mod):
    n_rows, d = conds.shape
    n = w_mod.shape[1]
    bn = _tile(n, 512, 128)
    return pl.pallas_call(
        functools.partial(_adaln_kernel, kc=_tile(d, 256)),
        grid=(n // bn,),
        in_specs=[
            pl.BlockSpec((n_rows, d, 1), lambda j: (0, 0, 0)),
            pl.BlockSpec((d, bn), lambda j: (0, j)),
            pl.BlockSpec((1, bn), lambda j: (0, j)),
        ],
        out_specs=pl.BlockSpec((n_rows, bn), lambda j: (0, j)),
        out_shape=jax.ShapeDtypeStruct((n_rows, n), _F32),
        compiler_params=_cparams("arbitrary"),
        name="adaln",
    )(conds[:, :, None], w_mod, b_mod[None, :])


def _rms(x, gain):
    return x * lax.rsqrt(jnp.mean(x * x, axis=-1, keepdims=True) + RMS_EPS) * gain


def _norm_mod_kernel(x_ref, g_ref, sh_ref, sc_ref, o_ref):
    y = _rms(x_ref[...], g_ref[...])
    o_ref[...] = (y * (1.0 + sc_ref[...]) + sh_ref[...]).astype(o_ref.dtype)


def _norm_kernel(x_ref, g_ref, o_ref):
    o_ref[...] = _rms(x_ref[...], g_ref[...]).astype(o_ref.dtype)


def _norm_mod(x, gain, shift, scale, out_dtype):
    t, d = x.shape
    tm = _tile(t, 256)
    row = pl.BlockSpec((1, d), lambda i: (0, 0))
    blk = pl.BlockSpec((tm, d), lambda i: (i, 0))
    if shift is None:
        return pl.pallas_call(
            _norm_kernel, grid=(t // tm,), in_specs=[blk, row], out_specs=blk,
            out_shape=jax.ShapeDtypeStruct((t, d), out_dtype),
            compiler_params=_cparams("parallel"), name="rmsnorm",
        )(x, gain[None, :])
    return pl.pallas_call(
        _norm_mod_kernel, grid=(t // tm,), in_specs=[blk, row, row, row], out_specs=blk,
        out_shape=jax.ShapeDtypeStruct((t, d), out_dtype),
        compiler_params=_cparams("parallel"), name="rmsnorm_modulate",
    )(x, gain[None, :], shift[None, :], scale[None, :])


def _mm_kernel(a_ref, w_ref, o_ref):
    o_ref[...] = jnp.dot(a_ref[...], w_ref[...], preferred_element_type=_F32).astype(o_ref.dtype)


def _mm_res_kernel(a_ref, w_ref, res_ref, gate_ref, cs_ref, o_ref):
    y = jnp.dot(a_ref[...], w_ref[...], preferred_element_type=_F32) * cs_ref[...]
    o_ref[...] = res_ref[...] + gate_ref[...] * y


def _matmul(a, w, out_dtype=_F32, *, grouped=False, res=None, gate=None, colscale=None, tm_target=512,
            tn_target=1024):
    m = a.shape[0]
    if grouped:
        g, kk, tn = w.shape
        n = g * tn
        a_spec = lambda tm: pl.BlockSpec((tm, kk), lambda j, i: (i, j))
        w_spec = pl.BlockSpec((None, kk, tn), lambda j, i: (j, 0, 0))
    else:
        kk, n = w.shape
        tn = _tile(n, tn_target, 128)
        a_spec = lambda tm: pl.BlockSpec((tm, kk), lambda j, i: (i, 0))
        w_spec = pl.BlockSpec((kk, tn), lambda j, i: (0, j))
    tm = _tile(m, tm_target)
    o_spec = pl.BlockSpec((tm, tn), lambda j, i: (i, j))
    row = pl.BlockSpec((1, tn), lambda j, i: (0, j))
    grid = (n // tn, m // tm)
    if res is None:
        return pl.pallas_call(
            _mm_kernel, grid=grid, in_specs=[a_spec(tm), w_spec], out_specs=o_spec,
            out_shape=jax.ShapeDtypeStruct((m, n), out_dtype),
            compiler_params=_cparams("parallel", "parallel"), name="matmul",
        )(a, w)
    if colscale is None:
        colscale = jnp.ones((n,), _F32)
    return pl.pallas_call(
        _mm_res_kernel, grid=grid, in_specs=[a_spec(tm), w_spec, o_spec, row, row], out_specs=o_spec,
        out_shape=jax.ShapeDtypeStruct((m, n), _F32),
        compiler_params=_cparams("parallel", "parallel"), name="matmul_gated_residual",
    )(a, w, res, gate[None, :], colscale[None, :])


def _qk_norm_rope_kernel(x_ref, gain_ref, cos_ref, sin_ref, o_ref, *, rope):
    n_heads = x_ref.shape[1] // HEAD_DIM
    if rope:
        cos = cos_ref[...]
        sin = sin_ref[...]
        lane = lax.broadcasted_iota(jnp.int32, cos.shape, 1)
        first_half = (lane % (HEAD_DIM // 2)) < (HEAD_DIM // 4)
    for h in range(n_heads):
        sl = slice(h * HEAD_DIM, (h + 1) * HEAD_DIM)
        y = _rms(x_ref[:, sl], gain_ref[:, sl])
        if rope:
            partner = jnp.where(first_half, pltpu.roll(y, HEAD_DIM - HEAD_DIM // 4, 1),
                                pltpu.roll(y, HEAD_DIM // 4, 1))
            y = y * cos + partner * sin
        o_ref[:, sl] = y.astype(o_ref.dtype)


def _qk_norm_rope(x, gains, cos, sin_signed):
    t, w = x.shape
    tm = _tile(t, 256)
    rope = cos is not None
    if not rope:
        cos = jnp.zeros((t, HEAD_DIM), _F32)
        sin_signed = cos
    blk = pl.BlockSpec((tm, w), lambda i: (i, 0))
    tab = pl.BlockSpec((tm, HEAD_DIM), lambda i: (i, 0))
    return pl.pallas_call(
        functools.partial(_qk_norm_rope_kernel, rope=rope), grid=(t // tm,),
        in_specs=[blk, pl.BlockSpec((1, w), lambda i: (0, 0)), tab, tab], out_specs=blk,
        out_shape=jax.ShapeDtypeStruct((t, w), _BF16),
        compiler_params=_cparams("parallel"), name="qk_norm_rope",
    )(x, gains[None, :], cos, sin_signed)


def _rope_tables(n):
    axis_dim = HEAD_DIM // 2
    t = jnp.arange(n)
    row = (t // GRID_W).astype(_F32)
    col = (t % GRID_W).astype(_F32)
    inv = ROPE_THETA ** (-jnp.arange(0, axis_dim, 2, dtype=_F32) / axis_dim)
    ang_r = row[:, None] * inv
    ang_c = col[:, None] * inv
    cos = jnp.concatenate([jnp.cos(ang_r)] * 2 + [jnp.cos(ang_c)] * 2, axis=-1)
    sin = jnp.concatenate([-jnp.sin(ang_r), jnp.sin(ang_r), -jnp.sin(ang_c), jnp.sin(ang_c)], axis=-1)
    return cos, sin


def _dot_nt(a, b):
    return lax.dot_general(a, b, (((1,), (1,)), ((), ())), preferred_element_type=_F32)


def _gqa_kernel(q_ref, k_ref, v_ref, o_ref, m_ref, l_ref, acc_ref, *, tk, scale):
    tq = q_ref.shape[0]
    group = q_ref.shape[1] // HEAD_DIM
    q = jnp.concatenate([q_ref[:, g * HEAD_DIM:(g + 1) * HEAD_DIM] for g in range(group)], axis=0)
    m_ref[...] = jnp.full(m_ref.shape, -jnp.inf, _F32)
    l_ref[...] = jnp.zeros(l_ref.shape, _F32)
    acc_ref[...] = jnp.zeros(acc_ref.shape, _F32)

    def body(j, carry):
        start = pl.multiple_of(j * tk, tk)
        s = _dot_nt(q, k_ref[pl.ds(start, tk), :]) * scale
        m_prev = m_ref[...]
        m_new = jnp.maximum(m_prev, jnp.max(s, axis=-1, keepdims=True))
        alpha = jnp.exp(m_prev - m_new)
        p = jnp.exp(s - m_new)
        l_ref[...] = alpha * l_ref[...] + jnp.sum(p, axis=-1, keepdims=True)
        acc_ref[...] = alpha * acc_ref[...] + jnp.dot(p.astype(_BF16), v_ref[pl.ds(start, tk), :],
                                                      preferred_element_type=_F32)
        m_ref[...] = m_new
        return carry

    lax.fori_loop(0, k_ref.shape[0] // tk, body, 0)
    o = acc_ref[...] / l_ref[...]
    for g in range(group):
        o_ref[:, g * HEAD_DIM:(g + 1) * HEAD_DIM] = o[g * tq:(g + 1) * tq, :].astype(o_ref.dtype)


def _gqa_attention(q, k, v, n_kv_heads):
    l, qw = q.shape
    s = k.shape[0]
    gw = qw // n_kv_heads
    tq = _tile(l, 256)
    tk = _tile(s, 640, 128)
    rows = tq * gw // HEAD_DIM
    return pl.pallas_call(
        functools.partial(_gqa_kernel, tk=tk, scale=HEAD_DIM ** -0.5),
        grid=(n_kv_heads, l // tq),
        in_specs=[
            pl.BlockSpec((tq, gw), lambda h, i: (i, h)),
            pl.BlockSpec((s, HEAD_DIM), lambda h, i: (0, h)),
            pl.BlockSpec((s, HEAD_DIM), lambda h, i: (0, h)),
        ],
        out_specs=pl.BlockSpec((tq, gw), lambda h, i: (i, h)),
        out_shape=jax.ShapeDtypeStruct((l, qw), _BF16),
        scratch_shapes=[pltpu.VMEM((rows, 1), _F32), pltpu.VMEM((rows, 1), _F32),
                        pltpu.VMEM((rows, HEAD_DIM), _F32)],
        compiler_params=_cparams("parallel", "parallel"), name="gqa_attention",
    )(q, k, v)


_NA_QROWS = 8
_NA_KROWS = 16


def _na_patterns():
    first = dict(delta=0, a=[max(i - NA_WIN_H // 2, 0) for i in range(_NA_QROWS)])
    inner = dict(delta=-(NA_WIN_H // 2), a=list(range(_NA_QROWS)))
    last = dict(delta=-(_NA_KROWS - _NA_QROWS),
                a=[min(i + NA_WIN_H // 2, _NA_KROWS - NA_WIN_H) for i in range(_NA_QROWS)])
    return first, inner, last


def _na_kernel(q_ref, k_ref, v_ref, kc_ref, vc_ref, tb_ref, o_ref, bias_ref, *, n_blocks, scale):
    b = pl.program_id(1)
    w = GRID_W
    lane_lo = lax.broadcasted_iota(jnp.int32, (w, 2 * w), 1) < w

    def build_bias(pat):
        for i in range(_NA_QROWS):
            for jj in range(_NA_KROWS // 2):
                j0 = 2 * jj
                in0 = 0 <= j0 - pat["a"][i] < NA_WIN_H
                in1 = 0 <= j0 + 1 - pat["a"][i] < NA_WIN_H
                rel0 = pat["delta"] + j0 - i + NA_WIN_H - 1
                if not (in0 or in1):
                    tile = jnp.full((w, 2 * w), _NEG, _F32)
                else:
                    tile = tb_ref[rel0 + 1]
                    if not in1:
                        tile = jnp.where(lane_lo, tile, _NEG)
                    elif not in0:
                        tile = jnp.where(lane_lo, _NEG, tile)
                bias_ref[i * w:(i + 1) * w, j0 * w:(j0 + 2) * w] = tile

    first, inner, last = _na_patterns()
    pl.when(b == 0)(functools.partial(build_bias, first))
    pl.when(b == 1)(functools.partial(build_bias, inner))
    pl.when(b == n_blocks - 1)(functools.partial(build_bias, last))

    slab_row = jnp.where(b == 0, 0, jnp.where(b == n_blocks - 1, (n_blocks - 2) * _NA_QROWS,
                                              b * _NA_QROWS - NA_WIN_H // 2))
    start = pl.multiple_of(slab_row * w, w)
    q = q_ref[...]
    s_loc = _dot_nt(q, k_ref[pl.ds(start, _NA_KROWS * w), :]) * scale + bias_ref[...]
    s_ctx = _dot_nt(q, kc_ref[...]) * scale
    m = jnp.maximum(jnp.max(s_loc, axis=-1, keepdims=True), jnp.max(s_ctx, axis=-1, keepdims=True))
    p_loc = jnp.exp(s_loc - m)
    p_ctx = jnp.exp(s_ctx - m)
    denom = jnp.sum(p_loc, axis=-1, keepdims=True) + jnp.sum(p_ctx, axis=-1, keepdims=True)
    o = (jnp.dot(p_loc.astype(_BF16), v_ref[pl.ds(start, _NA_KROWS * w), :], preferred_element_type=_F32)
         + jnp.dot(p_ctx.astype(_BF16), vc_ref[...], preferred_element_type=_F32))
    o_ref[...] = (o / denom).astype(o_ref.dtype)


def _na_bias_table(rpb):
    qc = np.arange(GRID_W)
    win_start = np.clip(qc - NA_WIN_W // 2, 0, GRID_W - NA_WIN_W)
    rel = qc[None, :] - qc[:, None]
    in_win = (qc[None, :] >= win_start[:, None]) & (qc[None, :] < win_start[:, None] + NA_WIN_W)
    rel_idx = np.clip(rel + NA_WIN_W - 1, 0, 2 * NA_WIN_W - 2)
    toe = jnp.where(in_win[None, None], rpb[:, :, rel_idx], _NEG).astype(_F32)
    neg = jnp.full_like(toe[:, :1], _NEG)
    lo = jnp.concatenate([neg, toe], axis=1)
    hi = jnp.concatenate([toe, neg], axis=1)
    return jnp.concatenate([lo, hi], axis=-1)


def _na_attention(px, pc, rpb, n_heads, q_head0, k_head0, v_head0, ck_head0, cv_head0):
    l = px.shape[0]
    lc = pc.shape[0]
    rows = l // GRID_W
    n_blocks = rows // _NA_QROWS
    assert rows % _NA_QROWS == 0 and n_blocks >= 3
    tq = _NA_QROWS * GRID_W
    tb = _na_bias_table(rpb)
    head_blk = lambda n, h0: pl.BlockSpec((n, HEAD_DIM), lambda h, b: (0, h0 + h))
    return pl.pallas_call(
        functools.partial(_na_kernel, n_blocks=n_blocks, scale=HEAD_DIM ** -0.5),
        grid=(n_heads, n_blocks),
        in_specs=[
            pl.BlockSpec((tq, HEAD_DIM), lambda h, b: (b, q_head0 + h)),
            head_blk(l, k_head0), head_blk(l, v_head0), head_blk(lc, ck_head0), head_blk(lc, cv_head0),
            pl.BlockSpec((None, 2 * NA_WIN_H, GRID_W, 2 * GRID_W), lambda h, b: (h, 0, 0, 0)),
        ],
        out_specs=pl.BlockSpec((tq, HEAD_DIM), lambda h, b: (b, h)),
        out_shape=jax.ShapeDtypeStruct((l, n_heads * HEAD_DIM), _BF16),
        scratch_shapes=[pltpu.VMEM((tq, _NA_KROWS * GRID_W), _F32)],
        compiler_params=_cparams("arbitrary", "arbitrary"), name="na_attention",
    )(px, px, px, pc, pc, tb)


_POOL_HALO = 16


def _pool_kernel(prev_ref, cur_ref, next_ref, g_ref, sh_ref, sc_ref, o_ref, *, n_tokens):
    i = pl.program_id(0)
    tm, d = cur_ref.shape
    dg = d // len(POOL_WINDOWS)
    x = jnp.concatenate([prev_ref[...], cur_ref[...], next_ref[...]], axis=0)
    a = _rms(x, g_ref[...]) * (1.0 + sc_ref[...]) + sh_ref[...]
    pos = i * tm - _POOL_HALO + lax.broadcasted_iota(jnp.int32, (tm + 2 * _POOL_HALO, 1), 0)
    a = jnp.where((pos >= 0) & (pos < n_tokens), a, 0.0)
    t = i * tm + lax.broadcasted_iota(jnp.int32, (tm, 1), 0)
    for g, w in enumerate(POOL_WINDOWS):
        ag = a[:, g * dg:(g + 1) * dg]
        tot = jnp.zeros((tm, dg), _F32)
        for off in range(-(w // 2), w - w // 2):
            tot = tot + ag[_POOL_HALO + off:_POOL_HALO + off + tm, :]
        cnt = jnp.minimum(t + (w - w // 2), n_tokens) - jnp.maximum(t - w // 2, 0)
        diff = tot / cnt.astype(_F32) - ag[_POOL_HALO:_POOL_HALO + tm, :]
        o_ref[:, g * dg:(g + 1) * dg] = diff.astype(o_ref.dtype)


def _pool_diffs(x, gain, shift, scale):
    t, d = x.shape
    tm = _tile(t, 256, _POOL_HALO)
    hb = tm // _POOL_HALO
    n_halo_blocks = t // _POOL_HALO
    row = pl.BlockSpec((1, d), lambda i: (0, 0))
    return pl.pallas_call(
        functools.partial(_pool_kernel, n_tokens=t), grid=(t // tm,),
        in_specs=[
            pl.BlockSpec((_POOL_HALO, d), lambda i: (jnp.maximum(i * hb - 1, 0), 0)),
            pl.BlockSpec((tm, d), lambda i: (i, 0)),
            pl.BlockSpec((_POOL_HALO, d), lambda i: (jnp.minimum((i + 1) * hb, n_halo_blocks - 1), 0)),
            row, row, row,
        ],
        out_specs=pl.BlockSpec((tm, d), lambda i: (i, 0)),
        out_shape=jax.ShapeDtypeStruct((t, d), _BF16),
        compiler_params=_cparams("parallel"), name="pool_diffs",
    )(x, x, x, gain[None, :], shift[None, :], scale[None, :])


def _router_kernel(t_ref, w_ref, b_ref, gates_ref):
    n_exp = w_ref.shape[0]
    tm = t_ref.shape[0]
    per_group = n_exp // N_EXPERT_GROUPS
    logits = _dot_nt(w_ref[...], t_ref[...])
    scores = 1.0 / (1.0 + jnp.exp(-logits))
    biased = scores + b_ref[...]

    def first_max(vals):
        idx = lax.broadcasted_iota(jnp.int32, vals.shape, 0).astype(_F32)
        m = jnp.max(vals, axis=0, keepdims=True)
        first = jnp.min(jnp.where(vals == m, idx, float(vals.shape[0])), axis=0, keepdims=True)
        return m, idx == first

    def take_top(vals, k):
        sel = jnp.zeros(vals.shape, _F32)
        for _ in range(k):
            _, hit = first_max(vals)
            sel = jnp.where(hit, 1.0, sel)
            vals = jnp.where(hit, -jnp.inf, vals)
        return sel

    group_scores = []
    for g in range(N_EXPERT_GROUPS):
        v = biased[g * per_group:(g + 1) * per_group, :]
        m1, hit = first_max(v)
        m2 = jnp.max(jnp.where(hit, -jnp.inf, v), axis=0, keepdims=True)
        group_scores.append(m1 + m2)
    gsel = take_top(jnp.concatenate(group_scores, axis=0), TOPK_GROUPS)
    keep = jnp.concatenate([jnp.broadcast_to(gsel[g:g + 1, :], (per_group, tm)) for g in range(N_EXPERT_GROUPS)],
                           axis=0)
    esel = take_top(jnp.where(keep > 0.0, biased, -jnp.inf), TOP_K)
    wsel = esel * scores
    gates_ref[...] = wsel / jnp.sum(wsel, axis=0, keepdims=True) * ROUTED_SCALE


def _router(t, w_router, router_bias):
    n_tok, d = t.shape
    n_exp = w_router.shape[1]
    tm = _tile(n_tok, 512, 128)
    return pl.pallas_call(
        _router_kernel, grid=(n_tok // tm,),
        in_specs=[
            pl.BlockSpec((tm, d), lambda i: (i, 0)),
            pl.BlockSpec((n_exp, d), lambda i: (0, 0)),
            pl.BlockSpec((n_exp, 1), lambda i: (0, 0)),
        ],
        out_specs=pl.BlockSpec((n_exp, tm), lambda i: (0, i)),
        out_shape=jax.ShapeDtypeStruct((n_exp, n_tok), _F32),
        compiler_params=_cparams("parallel"), name="moe_router",
    )(t, w_router.T.astype(_BF16), router_bias[:, None])


def _ffn_kernel(*refs, gated, residual):
    t_ref, wg_ref, wu_ref, wd_ref = refs[:4]
    rest = list(refs[4:])
    gates_ref = rest.pop(0) if gated else None
    if residual:
        add_ref, res_ref, rgate_ref = rest[:3]
        rest = rest[3:]
    o_ref, = rest
    e = pl.program_id(1)
    t = t_ref[...]
    h = _silu(jnp.dot(t, wg_ref[...], preferred_element_type=_F32)) * jnp.dot(t, wu_ref[...],
                                                                               preferred_element_type=_F32)
    if gated:
        gt = gates_ref[...]
        lane = lax.broadcasted_iota(jnp.int32, gt.shape, 1)
        h = h * jnp.sum(jnp.where(lane == e, gt, 0.0), axis=1, keepdims=True)
    y = jnp.dot(h.astype(_BF16), wd_ref[...], preferred_element_type=_F32)

    @pl.when(e == 0)
    def _():
        o_ref[...] = y

    @pl.when(e > 0)
    def _():
        o_ref[...] += y

    if residual:
        @pl.when(e == pl.num_programs(1) - 1)
        def _():
            o_ref[...] = res_ref[...] + rgate_ref[...] * (o_ref[...] + add_ref[...])


def _ffn(t, w_gate, w_up, w_down, gates=None, residual=None):
    n_tok, d = t.shape
    n_exp, _, ff = w_gate.shape
    tm = _tile(n_tok, 512 if residual is None else 256)
    blk = pl.BlockSpec((tm, d), lambda i, e: (i, 0))
    in_specs = [blk,
                pl.BlockSpec((None, d, ff), lambda i, e: (e, 0, 0)),
                pl.BlockSpec((None, d, ff), lambda i, e: (e, 0, 0)),
                pl.BlockSpec((None, ff, d), lambda i, e: (e, 0, 0))]
    args = [t, w_gate, w_up, w_down]
    if gates is not None:
        in_specs.append(pl.BlockSpec((tm, gates.shape[1]), lambda i, e: (i, 0)))
        args.append(gates)
    if residual is not None:
        add, res, rgate = residual
        in_specs += [blk, blk, pl.BlockSpec((1, d), lambda i, e: (0, 0))]
        args += [add, res, rgate[None, :]]
    return pl.pallas_call(
        functools.partial(_ffn_kernel, gated=gates is not None, residual=residual is not None),
        grid=(n_tok // tm, n_exp), in_specs=in_specs, out_specs=blk,
        out_shape=jax.ShapeDtypeStruct((n_tok, d), _F32),
        compiler_params=_cparams("parallel", "arbitrary"), name="gated_ffn",
    )(*args)


def _moe_residual(x, t, gate2, w_router, router_bias, w_gate, w_up, w_down, ws_gate, ws_up, ws_down):
    d = t.shape[1]
    gates = _router(t, w_router, router_bias)
    routed = _ffn(t, w_gate.astype(_BF16), w_up.astype(_BF16), w_down.astype(_BF16), gates=gates.T)
    sff = ws_gate.shape[1]
    ff = w_gate.shape[2]
    n_sh = sff // ff
    to_chunks = lambda w: w.reshape(d, n_sh, ff).transpose(1, 0, 2).astype(_BF16)
    return _ffn(t, to_chunks(ws_gate), to_chunks(ws_up), ws_down.reshape(n_sh, ff, d).astype(_BF16),
                residual=(routed, x, gate2))


def _attention_layer(x, h_c, mods, mods_c, g_mix, w_in, w_out, rpb, q_gain, k_gain):
    shift1, scale1, gate1 = mods[:3]
    cshift1, cscale1 = mods_c[:2]
    n_tok = x.shape[0]
    na_w = NA_HEADS * HEAD_DIM
    gq_w = GQA_Q_HEADS * HEAD_DIM
    gkv_w = GQA_KV_HEADS * HEAD_DIM
    q_cols = na_w + gq_w
    o_nk, o_nv, o_gk, o_gv = q_cols, q_cols + na_w, q_cols + 2 * na_w, q_cols + 2 * na_w + gkv_w
    w_plain = jnp.concatenate([w_in[:, :na_w], w_in[:, o_nk:o_gk], w_in[:, o_gv:]], axis=1).astype(_BF16)
    w_normed = jnp.concatenate([w_in[:, na_w:q_cols], w_in[:, o_gk:o_gv]], axis=1).astype(_BF16)

    a_x = _norm_mod(x, g_mix, shift1, scale1, _BF16)
    a_c = _norm_mod(h_c, g_mix, cshift1, cscale1, _BF16)

    p_plain = _matmul(a_x, w_plain, _BF16)
    p_normed = _matmul(a_x, w_normed, _F32)
    c_plain = _matmul(a_c, w_plain[:, na_w:], _BF16)
    c_gk = _matmul(a_c, w_normed[:, gq_w:], _F32)

    cos, sin = _rope_tables(n_tok)
    gains = jnp.concatenate([jnp.tile(q_gain, GQA_Q_HEADS), jnp.tile(k_gain, GQA_KV_HEADS)])
    qk = _qk_norm_rope(p_normed, gains, cos, sin)
    ck = _qk_norm_rope(c_gk, gains[gq_w:], None, None)

    o_a = _na_attention(p_plain, c_plain, rpb, NA_HEADS, 0, NA_HEADS, 2 * NA_HEADS, 0, NA_HEADS)
    k_all = jnp.concatenate([qk[:, gq_w:], ck], axis=0)
    v_all = jnp.concatenate([p_plain[:, 3 * na_w:], c_plain[:, 2 * na_w:]], axis=0)
    o_b = _gqa_attention(qk[:, :gq_w], k_all, v_all, GQA_KV_HEADS)
    o = jnp.concatenate([o_a, o_b], axis=1)
    return _matmul(o, w_out.astype(_BF16), res=x, gate=gate1)


def _pool_layer(x, mods, g_mix, w_pool, pool_scale):
    shift1, scale1, gate1 = mods[:3]
    diffs = _pool_diffs(x, g_mix, shift1, scale1)
    return _matmul(diffs, w_pool.astype(_BF16), grouped=True, res=x, gate=gate1, colscale=pool_scale)


def kernel(x, c, ctx, c_ctx, w_mod, b_mod, g_mix, g_ffn, attn_w_in, attn_w_out, na_rpb, q_gain, k_gain, pool_w,
           pool_scale, moe_w_router, moe_router_bias, moe_w_gate, moe_w_up, moe_w_down, moe_ws_gate, moe_ws_up,
           moe_ws_down, g_final):
    batch, seq, d = x.shape
    depth = w_mod.shape[0]
    outs = []
    for bi in range(batch):
        xb = x[bi]
        h_c = ctx[bi]
        conds = jnp.stack([c[bi], c_ctx])
        for i in range(depth):
            j = i // 2
            mod_rows = _adaln(conds, w_mod[i], b_mod[i])
            mods = jnp.split(mod_rows[0], N_MOD)
            mods_c = jnp.split(mod_rows[1], N_MOD)
            if i % 2 == 0:
                xb = _attention_layer(xb, h_c, mods, mods_c, g_mix[i], attn_w_in[j], attn_w_out[j], na_rpb[j],
                                      q_gain[j], k_gain[j])
            else:
                xb = _pool_layer(xb, mods, g_mix[i], pool_w[j], pool_scale[j])
            assert not any(l % 2 == 0 for l in range(i + 1, depth)), "context-stream update not implemented"
            t = _norm_mod(xb, g_ffn[i], mods[3], mods[4], _BF16)
            xb = _moe_residual(xb, t, mods[5], moe_w_router[i], moe_router_bias[i], moe_w_gate[i], moe_w_up[i],
                               moe_w_down[i], moe_ws_gate[i], moe_ws_up[i], moe_ws_down[i])
        outs.append(_norm_mod(xb, g_final, None, None, _F32))
    return jnp.stack(outs)
```

```python
import functools

import numpy as np
import jax
import jax.numpy as jnp
from jax import lax
from jax.experimental import pallas as pl
from jax.experimental.pallas import tpu as pltpu

GRID_W = 64
HEAD_DIM = 128
NA_HEADS = 16
GQA_Q_HEADS = 16
GQA_KV_HEADS = 4
NA_WIN_H = 8
NA_WIN_W = 16
ROPE_THETA = 10000.0
POOL_WINDOWS = (2, 4, 8, 16)
N_EXPERT_GROUPS = 8
TOPK_GROUPS = 4
TOP_K = 8
ROUTED_SCALE = 2.5
N_MOD = 6
RMS_EPS = 1e-6

_F32 = jnp.float32
_BF16 = jnp.bfloat16
_NEG = -1e30
_VMEM_LIMIT_BYTES = 56 * 1024 * 1024


def _cparams(*sem):
    return pltpu.CompilerParams(dimension_semantics=sem, vmem_limit_bytes=_VMEM_LIMIT_BYTES)


def _tile(n, target, mult=8):
    t = min(n, target)
    while t > mult and (n % t or t % mult):
        t -= mult
    assert n % t == 0, (n, target, mult)
    return t


def _silu(x):
    return x / (1.0 + jnp.exp(-x))


def _adaln_kernel(c_ref, w_ref, b_ref, o_ref, *, kc):
    n_rows, d, _ = c_ref.shape
    bn = w_ref.shape[1]
    accs = [jnp.zeros((1, bn), _F32) for _ in range(n_rows)]
    for k0 in range(0, d, kc):
        w = w_ref[k0:k0 + kc, :]
        for r in range(n_rows):
            cc = _silu(c_ref[r, k0:k0 + kc, :])
            accs[r] = accs[r] + jnp.sum(cc * w, axis=0, keepdims=True)
    o_ref[...] = jnp.concatenate(accs, axis=0) + b_ref[...]


def _adaln(conds, w_mod, b_mod):
    n_rows, d = conds.shape
    n = w_mod.shape[1]
    bn = _tile(n, 512, 128)
    return pl.pallas_call(
        functools.partial(_adaln_kernel, kc=_tile(d, 256)),
        grid=(n // bn,),
        in_specs=[
            pl.BlockSpec((n_rows, d, 1), lambda j: (0, 0, 0)),
            pl.BlockSpec((d, bn), lambda j: (0, j)),
            pl.BlockSpec((1, bn), lambda j: (0, j)),
        ],
        out_specs=pl.BlockSpec((n_rows, bn), lambda j: (0, j)),
        out_shape=jax.ShapeDtypeStruct((n_rows, n), _F32),
        compiler_params=_cparams("arbitrary"),
        name="adaln",
    )(conds[:, :, None], w_mod, b_mod[None, :])


def _rms(x, gain):
    return x * lax.rsqrt(jnp.mean(x * x, axis=-1, keepdims=True) + RMS_EPS) * gain


def _norm_mod_kernel(x_ref, g_ref, sh_ref, sc_ref, o_ref):
    y = _rms(x_ref[...], g_ref[...])
    o_ref[...] = (y * (1.0 + sc_ref[...]) + sh_ref[...]).astype(o_ref.dtype)


def _norm_kernel(x_ref, g_ref, o_ref):
    o_ref[...] = _rms(x_ref[...], g_ref[...]).astype(o_ref.dtype)


def _norm_mod(x, gain, shift, scale, out_dtype):
    t, d = x.shape
    tm = _tile(t, 256)
    row = pl.BlockSpec((1, d), lambda i: (0, 0))
    blk = pl.BlockSpec((tm, d), lambda i: (i, 0))
    if shift is None:
        return pl.pallas_call(
            _norm_kernel, grid=(t // tm,), in_specs=[blk, row], out_specs=blk,
            out_shape=jax.ShapeDtypeStruct((t, d), out_dtype),
            compiler_params=_cparams("parallel"), name="rmsnorm",
        )(x, gain[None, :])
    return pl.pallas_call(
        _norm_mod_kernel, grid=(t // tm,), in_specs=[blk, row, row, row], out_specs=blk,
        out_shape=jax.ShapeDtypeStruct((t, d), out_dtype),
        compiler_params=_cparams("parallel"), name="rmsnorm_modulate",
    )(x, gain[None, :], shift[None, :], scale[None, :])


def _mm_kernel(a_ref, w_ref, o_ref):
    o_ref[...] = jnp.dot(a_ref[...], w_ref[...], preferred_element_type=_F32).astype(o_ref.dtype)


def _mm_res_kernel(a_ref, w_ref, res_ref, gate_ref, cs_ref, o_ref):
    y = jnp.dot(a_ref[...], w_ref[...], preferred_element_type=_F32) * cs_ref[...]
    o_ref[...] = res_ref[...] + gate_ref[...] * y


def _matmul(a, w, out_dtype=_F32, *, grouped=False, res=None, gate=None, colscale=None, tm_target=512,
            tn_target=1024):
    m = a.shape[0]
    if grouped:
        g, kk, tn = w.shape
        n = g * tn
        a_spec = lambda tm: pl.BlockSpec((tm, kk), lambda j, i: (i, j))
        w_spec = pl.BlockSpec((None, kk, tn), lambda j, i: (j, 0, 0))
    else:
        kk, n = w.shape
        tn = _tile(n, tn_target, 128)
        a_spec = lambda tm: pl.BlockSpec((tm, kk), lambda j, i: (i, 0))
        w_spec = pl.BlockSpec((kk, tn), lambda j, i: (0, j))
    tm = _tile(m, tm_target)
    o_spec = pl.BlockSpec((tm, tn), lambda j, i: (i, j))
    row = pl.BlockSpec((1, tn), lambda j, i: (0, j))
    grid = (n // tn, m // tm)
    if res is None:
        return pl.pallas_call(
            _mm_kernel, grid=grid, in_specs=[a_spec(tm), w_spec], out_specs=o_spec,
            out_shape=jax.ShapeDtypeStruct((m, n), out_dtype),
            compiler_params=_cparams("parallel", "parallel"), name="matmul",
        )(a, w)
    if colscale is None:
        colscale = jnp.ones((n,), _F32)
    return pl.pallas_call(
        _mm_res_kernel, grid=grid, in_specs=[a_spec(tm), w_spec, o_spec, row, row], out_specs=o_spec,
        out_shape=jax.ShapeDtypeStruct((m, n), _F32),
        compiler_params=_cparams("parallel", "parallel"), name="matmul_gated_residual",
    )(a, w, res, gate[None, :], colscale[None, :])


def _qk_norm_rope_kernel(x_ref, gain_ref, post_ref, cos_ref, sin_ref, o_ref, *, rope):
    n_heads = x_ref.shape[1] // HEAD_DIM
    if rope:
        cos = cos_ref[...]
        sin = sin_ref[...]
        lane = lax.broadcasted_iota(jnp.int32, cos.shape, 1)
        first_half = (lane % (HEAD_DIM // 2)) < (HEAD_DIM // 4)
    for h in range(n_heads):
        sl = slice(h * HEAD_DIM, (h + 1) * HEAD_DIM)
        y = _rms(x_ref[:, sl], gain_ref[:, sl])
        if rope:
            partner = jnp.where(first_half, pltpu.roll(y, HEAD_DIM - HEAD_DIM // 4, 1),
                                pltpu.roll(y, HEAD_DIM // 4, 1))
            y = y * cos + partner * sin
        o_ref[:, sl] = (y * post_ref[:, sl]).astype(o_ref.dtype)


def _qk_norm_rope(x, gains, post_scale, cos, sin_signed):
    t, w = x.shape
    tm = _tile(t, 256)
    rope = cos is not None
    if not rope:
        cos = jnp.zeros((t, HEAD_DIM), _F32)
        sin_signed = cos
    blk = pl.BlockSpec((tm, w), lambda i: (i, 0))
    tab = pl.BlockSpec((tm, HEAD_DIM), lambda i: (i, 0))
    vec = pl.BlockSpec((1, w), lambda i: (0, 0))
    return pl.pallas_call(
        functools.partial(_qk_norm_rope_kernel, rope=rope), grid=(t // tm,),
        in_specs=[blk, vec, vec, tab, tab], out_specs=blk,
        out_shape=jax.ShapeDtypeStruct((t, w), _BF16),
        compiler_params=_cparams("parallel"), name="qk_norm_rope",
    )(x, gains[None, :], post_scale[None, :], cos, sin_signed)


def _rope_tables(n):
    axis_dim = HEAD_DIM // 2
    t = jnp.arange(n)
    row = (t // GRID_W).astype(_F32)
    col = (t % GRID_W).astype(_F32)
    inv = ROPE_THETA ** (-jnp.arange(0, axis_dim, 2, dtype=_F32) / axis_dim)
    ang_r = row[:, None] * inv
    ang_c = col[:, None] * inv
    cos = jnp.concatenate([jnp.cos(ang_r)] * 2 + [jnp.cos(ang_c)] * 2, axis=-1)
    sin = jnp.concatenate([-jnp.sin(ang_r), jnp.sin(ang_r), -jnp.sin(ang_c), jnp.sin(ang_c)], axis=-1)
    return cos, sin


def _dot_nt(a, b):
    return lax.dot_general(a, b, (((1,), (1,)), ((), ())), preferred_element_type=_F32)


def _gqa_kernel(q_ref, k_ref, v_ref, o_ref, m_ref, l_ref, acc_ref, *, tk):
    tq = q_ref.shape[0]
    group = q_ref.shape[1] // HEAD_DIM
    q = jnp.concatenate([q_ref[:, g * HEAD_DIM:(g + 1) * HEAD_DIM] for g in range(group)], axis=0)
    m_ref[...] = jnp.full(m_ref.shape, -jnp.inf, _F32)
    l_ref[...] = jnp.zeros(l_ref.shape, _F32)
    acc_ref[...] = jnp.zeros(acc_ref.shape, _F32)

    def body(j, carry):
        start = pl.multiple_of(j * tk, tk)
        s = _dot_nt(q, k_ref[pl.ds(start, tk), :])
        m_prev = m_ref[...]
        m_new = jnp.maximum(m_prev, jnp.max(s, axis=-1, keepdims=True))
        alpha = jnp.exp2(m_prev - m_new)
        p = jnp.exp2(s - m_new)
        l_ref[...] = alpha * l_ref[...] + jnp.sum(p, axis=-1, keepdims=True)
        acc_ref[...] = alpha * acc_ref[...] + jnp.dot(p.astype(_BF16), v_ref[pl.ds(start, tk), :],
                                                      preferred_element_type=_F32)
        m_ref[...] = m_new
        return carry

    lax.fori_loop(0, k_ref.shape[0] // tk, body, 0)
    o = acc_ref[...] / l_ref[...]
    for g in range(group):
        o_ref[:, g * HEAD_DIM:(g + 1) * HEAD_DIM] = o[g * tq:(g + 1) * tq, :].astype(o_ref.dtype)


def _gqa_attention(q, k, v, n_kv_heads):
    l, qw = q.shape
    s = k.shape[0]
    gw = qw // n_kv_heads
    tq = _tile(l, 256)
    tk = _tile(s, 640, 128)
    rows = tq * gw // HEAD_DIM
    col = pltpu.VMEM((rows, 1), _F32)
    return pl.pallas_call(
        functools.partial(_gqa_kernel, tk=tk),
        grid=(n_kv_heads, l // tq),
        in_specs=[
            pl.BlockSpec((tq, gw), lambda h, i: (i, h)),
            pl.BlockSpec((s, HEAD_DIM), lambda h, i: (0, h)),
            pl.BlockSpec((s, HEAD_DIM), lambda h, i: (0, h)),
        ],
        out_specs=pl.BlockSpec((tq, gw), lambda h, i: (i, h)),
        out_shape=jax.ShapeDtypeStruct((l, qw), _BF16),
        scratch_shapes=[col, col, pltpu.VMEM((rows, HEAD_DIM), _F32)],
        compiler_params=_cparams("parallel", "parallel"), name="gqa_attention",
    )(q, k, v)


_NA_QROWS = 8
_NA_KROWS = 16


def _na_patterns():
    first = dict(delta=0, a=[max(i - NA_WIN_H // 2, 0) for i in range(_NA_QROWS)])
    inner = dict(delta=-(NA_WIN_H // 2), a=list(range(_NA_QROWS)))
    last = dict(delta=-(_NA_KROWS - _NA_QROWS),
                a=[min(i + NA_WIN_H // 2, _NA_KROWS - NA_WIN_H) for i in range(_NA_QROWS)])
    return first, inner, last


def _na_kernel(q_ref, k_ref, v_ref, kc_ref, vc_ref, tb_ref, o_ref, bias_ref, *, n_blocks, scale):
    b = pl.program_id(1)
    w = GRID_W
    lane_lo = lax.broadcasted_iota(jnp.int32, (w, 2 * w), 1) < w

    def build_bias(pat):
        for i in range(_NA_QROWS):
            for jj in range(_NA_KROWS // 2):
                j0 = 2 * jj
                in0 = 0 <= j0 - pat["a"][i] < NA_WIN_H
                in1 = 0 <= j0 + 1 - pat["a"][i] < NA_WIN_H
                rel0 = pat["delta"] + j0 - i + NA_WIN_H - 1
                if not (in0 or in1):
                    tile = jnp.full((w, 2 * w), _NEG, _F32)
                else:
                    tile = tb_ref[rel0 + 1]
                    if not in1:
                        tile = jnp.where(lane_lo, tile, _NEG)
                    elif not in0:
                        tile = jnp.where(lane_lo, _NEG, tile)
                bias_ref[i * w:(i + 1) * w, j0 * w:(j0 + 2) * w] = tile

    first, inner, last = _na_patterns()
    pl.when(b == 0)(functools.partial(build_bias, first))
    pl.when(b == 1)(functools.partial(build_bias, inner))
    pl.when(b == n_blocks - 1)(functools.partial(build_bias, last))

    slab_row = jnp.where(b == 0, 0, jnp.where(b == n_blocks - 1, (n_blocks - 2) * _NA_QROWS,
                                              b * _NA_QROWS - NA_WIN_H // 2))
    start = pl.multiple_of(slab_row * w, w)
    q = q_ref[...]
    s_loc = _dot_nt(q, k_ref[pl.ds(start, _NA_KROWS * w), :]) * scale + bias_ref[...]
    s_ctx = _dot_nt(q, kc_ref[...]) * scale
    m = jnp.maximum(jnp.max(s_loc, axis=-1, keepdims=True), jnp.max(s_ctx, axis=-1, keepdims=True))
    p_loc = jnp.exp(s_loc - m)
    p_ctx = jnp.exp(s_ctx - m)
    denom = jnp.sum(p_loc, axis=-1, keepdims=True) + jnp.sum(p_ctx, axis=-1, keepdims=True)
    o = (jnp.dot(p_loc.astype(_BF16), v_ref[pl.ds(start, _NA_KROWS * w), :], preferred_element_type=_F32)
         + jnp.dot(p_ctx.astype(_BF16), vc_ref[...], preferred_element_type=_F32))
    o_ref[...] = (o / denom).astype(o_ref.dtype)


def _na_bias_table(rpb):
    qc = np.arange(GRID_W)
    win_start = np.clip(qc - NA_WIN_W // 2, 0, GRID_W - NA_WIN_W)
    rel = qc[None, :] - qc[:, None]
    in_win = (qc[None, :] >= win_start[:, None]) & (qc[None, :] < win_start[:, None] + NA_WIN_W)
    rel_idx = np.clip(rel + NA_WIN_W - 1, 0, 2 * NA_WIN_W - 2)
    toe = jnp.where(in_win[None, None], rpb[:, :, rel_idx], _NEG).astype(_F32)
    neg = jnp.full_like(toe[:, :1], _NEG)
    lo = jnp.concatenate([neg, toe], axis=1)
    hi = jnp.concatenate([toe, neg], axis=1)
    return jnp.concatenate([lo, hi], axis=-1)


def _na_attention(px, pc, rpb, n_heads, q_head0, k_head0, v_head0, ck_head0, cv_head0):
    l = px.shape[0]
    lc = pc.shape[0]
    rows = l // GRID_W
    n_blocks = rows // _NA_QROWS
    assert rows % _NA_QROWS == 0 and n_blocks >= 3
    tq = _NA_QROWS * GRID_W
    tb = _na_bias_table(rpb)
    head_blk = lambda n, h0: pl.BlockSpec((n, HEAD_DIM), lambda h, b: (0, h0 + h))
    return pl.pallas_call(
        functools.partial(_na_kernel, n_blocks=n_blocks, scale=HEAD_DIM ** -0.5),
        grid=(n_heads, n_blocks),
        in_specs=[
            pl.BlockSpec((tq, HEAD_DIM), lambda h, b: (b, q_head0 + h)),
            head_blk(l, k_head0), head_blk(l, v_head0), head_blk(lc, ck_head0), head_blk(lc, cv_head0),
            pl.BlockSpec((None, 2 * NA_WIN_H, GRID_W, 2 * GRID_W), lambda h, b: (h, 0, 0, 0)),
        ],
        out_specs=pl.BlockSpec((tq, HEAD_DIM), lambda h, b: (b, h)),
        out_shape=jax.ShapeDtypeStruct((l, n_heads * HEAD_DIM), _BF16),
        scratch_shapes=[pltpu.VMEM((tq, _NA_KROWS * GRID_W), _F32)],
        compiler_params=_cparams("arbitrary", "arbitrary"), name="na_attention",
    )(px, px, px, pc, pc, tb)


_POOL_HALO = 16


def _pool_kernel(prev_ref, cur_ref, next_ref, g_ref, sh_ref, sc_ref, o_ref, *, n_tokens):
    i = pl.program_id(0)
    tm, d = cur_ref.shape
    dg = d // len(POOL_WINDOWS)
    x = jnp.concatenate([prev_ref[...], cur_ref[...], next_ref[...]], axis=0)
    a = _rms(x, g_ref[...]) * (1.0 + sc_ref[...]) + sh_ref[...]
    pos = i * tm - _POOL_HALO + lax.broadcasted_iota(jnp.int32, (tm + 2 * _POOL_HALO, 1), 0)
    a = jnp.where((pos >= 0) & (pos < n_tokens), a, 0.0)
    t = i * tm + lax.broadcasted_iota(jnp.int32, (tm, 1), 0)
    for g, w in enumerate(POOL_WINDOWS):
        ag = a[:, g * dg:(g + 1) * dg]
        tot = jnp.zeros((tm, dg), _F32)
        for off in range(-(w // 2), w - w // 2):
            tot = tot + ag[_POOL_HALO + off:_POOL_HALO + off + tm, :]
        cnt = jnp.minimum(t + (w - w // 2), n_tokens) - jnp.maximum(t - w // 2, 0)
        diff = tot / cnt.astype(_F32) - ag[_POOL_HALO:_POOL_HALO + tm, :]
        o_ref[:, g * dg:(g + 1) * dg] = diff.astype(o_ref.dtype)


def _pool_diffs(x, gain, shift, scale):
    t, d = x.shape
    tm = _tile(t, 256, _POOL_HALO)
    hb = tm // _POOL_HALO
    n_halo_blocks = t // _POOL_HALO
    row = pl.BlockSpec((1, d), lambda i: (0, 0))
    return pl.pallas_call(
        functools.partial(_pool_kernel, n_tokens=t), grid=(t // tm,),
        in_specs=[
            pl.BlockSpec((_POOL_HALO, d), lambda i: (jnp.maximum(i * hb - 1, 0), 0)),
            pl.BlockSpec((tm, d), lambda i: (i, 0)),
            pl.BlockSpec((_POOL_HALO, d), lambda i: (jnp.minimum((i + 1) * hb, n_halo_blocks - 1), 0)),
            row, row, row,
        ],
        out_specs=pl.BlockSpec((tm, d), lambda i: (i, 0)),
        out_shape=jax.ShapeDtypeStruct((t, d), _BF16),
        compiler_params=_cparams("parallel"), name="pool_diffs",
    )(x, x, x, gain[None, :], shift[None, :], scale[None, :])


def _router_kernel(t_ref, w_ref, b_ref, idx_ref, gate_ref, rank_ref, cnt_ref):
    n_exp = w_ref.shape[0]
    tm = t_ref.shape[0]
    per_group = n_exp // N_EXPERT_GROUPS

    @pl.when(pl.program_id(0) == 0)
    def _():
        cnt_ref[...] = jnp.zeros(cnt_ref.shape, _F32)

    logits = _dot_nt(w_ref[...], t_ref[...])
    scores = 1.0 / (1.0 + jnp.exp(-logits))
    biased = scores + b_ref[...]

    def first_max(vals):
        idx = lax.broadcasted_iota(jnp.int32, vals.shape, 0).astype(_F32)
        m = jnp.max(vals, axis=0, keepdims=True)
        first = jnp.min(jnp.where(vals == m, idx, float(vals.shape[0])), axis=0, keepdims=True)
        return m, idx == first

    def take_top(vals, k):
        hits = []
        for _ in range(k):
            _, hit = first_max(vals)
            hits.append(hit)
            vals = jnp.where(hit, -jnp.inf, vals)
        return hits

    def union(hits):
        sel = jnp.zeros(hits[0].shape, _F32)
        for hit in hits:
            sel = jnp.where(hit, 1.0, sel)
        return sel

    group_scores = []
    for g in range(N_EXPERT_GROUPS):
        v = biased[g * per_group:(g + 1) * per_group, :]
        m1, hit = first_max(v)
        m2 = jnp.max(jnp.where(hit, -jnp.inf, v), axis=0, keepdims=True)
        group_scores.append(m1 + m2)
    gsel = union(take_top(jnp.concatenate(group_scores, axis=0), TOPK_GROUPS))
    keep = jnp.concatenate([jnp.broadcast_to(gsel[g:g + 1, :], (per_group, tm)) for g in range(N_EXPERT_GROUPS)],
                           axis=0)
    hits = take_top(jnp.where(keep > 0.0, biased, -jnp.inf), TOP_K)
    esel = union(hits)
    wsel = esel * scores
    gates = wsel / jnp.sum(wsel, axis=0, keepdims=True) * ROUTED_SCALE

    earlier = (lax.broadcasted_iota(jnp.int32, (tm, tm), 0) < lax.broadcasted_iota(jnp.int32, (tm, tm), 1))
    rank = cnt_ref[...] + jnp.dot(esel.astype(_BF16), jnp.where(earlier, 1.0, 0.0).astype(_BF16),
                                  preferred_element_type=_F32)
    cnt_ref[...] += jnp.sum(esel, axis=1, keepdims=True)

    eidx = lax.broadcasted_iota(jnp.int32, (n_exp, tm), 0).astype(_F32)
    pick = lambda hit, v: jnp.sum(jnp.where(hit, v, 0.0), axis=0, keepdims=True)
    idx_ref[...] = jnp.concatenate([pick(h, eidx) for h in hits], axis=0).astype(jnp.int32)
    gate_ref[...] = jnp.concatenate([pick(h, gates) for h in hits], axis=0)
    rank_ref[...] = jnp.concatenate([pick(h, rank) for h in hits], axis=0).astype(jnp.int32)


def _router(t, w_router, router_bias):
    n_tok, d = t.shape
    n_exp = w_router.shape[1]
    tm = _tile(n_tok, 512, 128)
    per_tok = pl.BlockSpec((TOP_K, tm), lambda i: (0, i))
    return pl.pallas_call(
        _router_kernel, grid=(n_tok // tm,),
        in_specs=[
            pl.BlockSpec((tm, d), lambda i: (i, 0)),
            pl.BlockSpec((n_exp, d), lambda i: (0, 0)),
            pl.BlockSpec((n_exp, 1), lambda i: (0, 0)),
        ],
        out_specs=[per_tok, per_tok, per_tok, pl.BlockSpec((n_exp, 1), lambda i: (0, 0))],
        out_shape=[jax.ShapeDtypeStruct((TOP_K, n_tok), jnp.int32), jax.ShapeDtypeStruct((TOP_K, n_tok), _F32),
                   jax.ShapeDtypeStruct((TOP_K, n_tok), jnp.int32), jax.ShapeDtypeStruct((n_exp, 1), _F32)],
        compiler_params=_cparams("arbitrary"), name="moe_router",
    )(t, w_router.T.astype(_BF16), router_bias[:, None])


_MOE_TILE_ROWS = 256


def _packed_cols(d):
    return d // 2


def _pack_pairs(y):
    half = y.shape[1] // 2
    bits = lambda v: lax.bitcast_convert_type(v.astype(_BF16).astype(_F32), jnp.uint32)
    return (bits(y[:, :half]) & jnp.uint32(0xFFFF0000)) | (bits(y[:, half:]) >> jnp.uint32(16))


def _unpack_pairs(w):
    hi = lax.bitcast_convert_type(w & jnp.uint32(0xFFFF0000), _F32)
    lo = lax.bitcast_convert_type(w << jnp.uint32(16), _F32)
    return hi, lo


def _norm_mod_pack_kernel(x_ref, g_ref, sh_ref, sc_ref, o_ref, p_ref):
    y = _rms(x_ref[...], g_ref[...]) * (1.0 + sc_ref[...]) + sh_ref[...]
    o_ref[...] = y.astype(o_ref.dtype)
    p_ref[...] = _pack_pairs(y)


def _norm_mod_pack(x, gain, shift, scale):
    t, d = x.shape
    tm = _tile(t, 256)
    row = pl.BlockSpec((1, d), lambda i: (0, 0))
    blk = pl.BlockSpec((tm, d), lambda i: (i, 0))
    return pl.pallas_call(
        _norm_mod_pack_kernel, grid=(t // tm,), in_specs=[blk, row, row, row],
        out_specs=[blk, pl.BlockSpec((tm, _packed_cols(d)), lambda i: (i, 0))],
        out_shape=[jax.ShapeDtypeStruct((t, d), _BF16), jax.ShapeDtypeStruct((t, _packed_cols(d)), jnp.uint32)],
        compiler_params=_cparams("parallel"), name="rmsnorm_modulate_pack",
    )(x, gain[None, :], shift[None, :], scale[None, :])


def _moe_plan(idx8, rank8, counts, tile_rows):
    n_exp = counts.shape[0]
    n_tok = idx8.shape[1]
    cnt = counts[:, 0].astype(jnp.int32)
    padded = (cnt + tile_rows - 1) // tile_rows * tile_rows
    ends = jnp.cumsum(padded)
    starts = ends - padded
    onehot = idx8[:, :, None] == jnp.arange(n_exp, dtype=jnp.int32)
    pos8 = rank8 + jnp.sum(jnp.where(onehot, starts, 0), axis=-1)
    n_tiles = TOP_K * n_tok // tile_rows + n_exp
    tile_row0 = jnp.arange(n_tiles, dtype=jnp.int32) * tile_rows
    tile_expert = jnp.minimum(jnp.searchsorted(ends, tile_row0, side="right"), n_exp - 1).astype(jnp.int32)
    n_active = (ends[-1:] // tile_rows).astype(jnp.int32)
    last_tile_row = jnp.where(padded > 0, ends - tile_rows, -1).astype(jnp.int32)
    return pos8.T.reshape(-1), tile_expert, n_active, last_tile_row, n_tiles


def _dispatch_kernel(pos_ref, last_ref, na_ref, tp_ref, xs_ref, zero_ref, sem, zsem, *, tile_rows):
    i = pl.program_id(0)
    tb = tp_ref.shape[0]
    n_exp = last_ref.shape[0]
    n_tiles = xs_ref.shape[0] // tile_rows

    @pl.when(i == 0)
    def _():
        zero_ref[...] = jnp.zeros(zero_ref.shape, zero_ref.dtype)

        def fill(row0):
            return pltpu.make_async_copy(zero_ref, xs_ref.at[pl.ds(pl.multiple_of(row0, tile_rows), tile_rows)],
                                         zsem)

        def each(e, start):
            tail = na_ref[0] + e

            def go(row0):
                cp = fill(row0)
                cp.start() if start else cp.wait()

            pl.when(last_ref[e] >= 0)(lambda: go(jnp.maximum(last_ref[e], 0)))
            pl.when(tail < n_tiles)(lambda: go(jnp.minimum(tail, n_tiles - 1) * tile_rows))

        lax.fori_loop(0, n_exp, lambda e, c: (each(e, True), c)[1], 0)
        lax.fori_loop(0, n_exp, lambda e, c: (each(e, False), c)[1], 0)

    def row(r, c):
        base = (i * tb + r) * TOP_K
        for k in range(TOP_K):
            pltpu.make_async_copy(tp_ref.at[pl.ds(r, 1)], xs_ref.at[pl.ds(pos_ref[base + k], 1)], sem).start()
        return c

    lax.fori_loop(0, tb, row, 0)
    for _ in range(TOP_K):
        pltpu.make_async_copy(tp_ref, xs_ref.at[pl.ds(0, tb)], sem).wait()


def _dispatch(tp, pos_flat, last_tile_row, n_active, n_rows, tile_rows):
    n_tok, c = tp.shape
    tb = _tile(n_tok, 256)
    return pl.pallas_call(
        functools.partial(_dispatch_kernel, tile_rows=tile_rows),
        grid_spec=pltpu.PrefetchScalarGridSpec(
            num_scalar_prefetch=3, grid=(n_tok // tb,),
            in_specs=[pl.BlockSpec((tb, c), lambda i, pos, last, na: (i, 0))],
            out_specs=pl.BlockSpec(memory_space=pl.ANY),
            scratch_shapes=[pltpu.VMEM((tile_rows, c), jnp.uint32), pltpu.SemaphoreType.DMA,
                            pltpu.SemaphoreType.DMA],
        ),
        out_shape=jax.ShapeDtypeStruct((n_rows, c), jnp.uint32),
        compiler_params=_cparams("arbitrary"), name="moe_dispatch",
    )(pos_flat, last_tile_row, n_active, tp)


def _grouped_ffn_kernel(te_ref, na_ref, xs_ref, wg_ref, wu_ref, wd_ref, ys_ref, wg_bf, wu_bf, wd_bf):
    i = pl.program_id(0)
    active = i < na_ref[0]
    new_expert = jnp.logical_or(i == 0, te_ref[i] != te_ref[jnp.maximum(i - 1, 0)])

    @pl.when(jnp.logical_and(active, new_expert))
    def _():
        wg_bf[...] = wg_ref[...].astype(_BF16)
        wu_bf[...] = wu_ref[...].astype(_BF16)
        wd_bf[...] = wd_ref[...].astype(_BF16)

    @pl.when(active)
    def _():
        half = wg_bf.shape[0] // 2
        hi, lo = _unpack_pairs(xs_ref[...])
        hi = hi.astype(_BF16)
        lo = lo.astype(_BF16)
        dot = lambda a, b: jnp.dot(a, b, preferred_element_type=_F32)
        hg = dot(hi, wg_bf[:half, :]) + dot(lo, wg_bf[half:, :])
        hu = dot(hi, wu_bf[:half, :]) + dot(lo, wu_bf[half:, :])
        h = (_silu(hg) * hu).astype(_BF16)
        ys_ref[...] = _pack_pairs(dot(h, wd_bf[...]))

    @pl.when(jnp.logical_not(active))
    def _():
        ys_ref[...] = jnp.zeros(ys_ref.shape, ys_ref.dtype)


def _grouped_ffn(xs, tile_expert, n_active, w_gate, w_up, w_down, tile_rows):
    n_rows, c = xs.shape
    n_exp, d, ff = w_gate.shape
    n_tiles = n_rows // tile_rows
    row_blk = pl.BlockSpec((tile_rows, c), lambda i, te, na: (jnp.minimum(i, jnp.maximum(na[0] - 1, 0)), 0))
    return pl.pallas_call(
        _grouped_ffn_kernel,
        grid_spec=pltpu.PrefetchScalarGridSpec(
            num_scalar_prefetch=2, grid=(n_tiles,),
            in_specs=[row_blk,
                      pl.BlockSpec((None, d, ff), lambda i, te, na: (te[i], 0, 0)),
                      pl.BlockSpec((None, d, ff), lambda i, te, na: (te[i], 0, 0)),
                      pl.BlockSpec((None, ff, d), lambda i, te, na: (te[i], 0, 0))],
            out_specs=pl.BlockSpec((tile_rows, c), lambda i, te, na: (i, 0)),
            scratch_shapes=[pltpu.VMEM((d, ff), _BF16), pltpu.VMEM((d, ff), _BF16), pltpu.VMEM((ff, d), _BF16)],
        ),
        out_shape=jax.ShapeDtypeStruct((n_rows, c), jnp.uint32),
        compiler_params=_cparams("arbitrary"), name="moe_grouped_ffn",
    )(tile_expert, n_active, xs, w_gate, w_up, w_down)


def _combine_kernel(pos_ref, ys_ref, g8_ref, x_ref, sh_ref, g2_ref, o_ref, buf, sem):
    i = pl.program_id(0)
    tb, d = x_ref.shape
    half = d // 2

    def row(r, c):
        base = (i * tb + r) * TOP_K
        for k in range(TOP_K):
            pltpu.make_async_copy(ys_ref.at[pl.ds(pos_ref[base + k], 1)], buf.at[k, pl.ds(r, 1)], sem).start()
        return c

    lax.fori_loop(0, tb, row, 0)
    for k in range(TOP_K):
        pltpu.make_async_copy(ys_ref.at[pl.ds(0, tb)], buf.at[k], sem).wait()

    g8 = g8_ref[...]
    acc_hi = jnp.zeros((tb, half), _F32)
    acc_lo = jnp.zeros((tb, half), _F32)
    for k in range(TOP_K):
        hi, lo = _unpack_pairs(buf[k])
        acc_hi = acc_hi + g8[:, k:k + 1] * hi
        acc_lo = acc_lo + g8[:, k:k + 1] * lo
    o_ref[:, :half] = x_ref[:, :half] + g2_ref[:, :half] * (sh_ref[:, :half] + acc_hi)
    o_ref[:, half:] = x_ref[:, half:] + g2_ref[:, half:] * (sh_ref[:, half:] + acc_lo)


def _combine(ys, pos_flat, gate8, x, shared, gate2):
    n_tok, d = x.shape
    c = ys.shape[1]
    tb = _tile(n_tok, 128)
    blk = pl.BlockSpec((tb, d), lambda i, pos: (i, 0))
    return pl.pallas_call(
        _combine_kernel,
        grid_spec=pltpu.PrefetchScalarGridSpec(
            num_scalar_prefetch=1, grid=(n_tok // tb,),
            in_specs=[pl.BlockSpec(memory_space=pl.ANY),
                      pl.BlockSpec((tb, TOP_K), lambda i, pos: (i, 0)),
                      blk, blk, pl.BlockSpec((1, d), lambda i, pos: (0, 0))],
            out_specs=blk,
            scratch_shapes=[pltpu.VMEM((TOP_K, tb, c), jnp.uint32), pltpu.SemaphoreType.DMA],
        ),
        out_shape=jax.ShapeDtypeStruct((n_tok, d), _F32),
        compiler_params=_cparams("arbitrary"), name="moe_combine",
    )(pos_flat, ys, gate8, x, shared, gate2[None, :])


def _ffn_kernel(t_ref, wg_ref, wu_ref, wd_ref, o_ref):
    e = pl.program_id(1)
    t = t_ref[...]
    h = _silu(jnp.dot(t, wg_ref[...], preferred_element_type=_F32)) * jnp.dot(t, wu_ref[...],
                                                                               preferred_element_type=_F32)
    y = jnp.dot(h.astype(_BF16), wd_ref[...], preferred_element_type=_F32)

    @pl.when(e == 0)
    def _():
        o_ref[...] = y

    @pl.when(e > 0)
    def _():
        o_ref[...] += y


def _ffn(t, w_gate, w_up, w_down):
    n_tok, d = t.shape
    n_chunks, _, ff = w_gate.shape
    tm = _tile(n_tok, 512)
    blk = pl.BlockSpec((tm, d), lambda i, e: (i, 0))
    return pl.pallas_call(
        _ffn_kernel, grid=(n_tok // tm, n_chunks),
        in_specs=[blk,
                  pl.BlockSpec((None, d, ff), lambda i, e: (e, 0, 0)),
                  pl.BlockSpec((None, d, ff), lambda i, e: (e, 0, 0)),
                  pl.BlockSpec((None, ff, d), lambda i, e: (e, 0, 0))],
        out_specs=blk,
        out_shape=jax.ShapeDtypeStruct((n_tok, d), _F32),
        compiler_params=_cparams("parallel", "arbitrary"), name="shared_ffn",
    )(t, w_gate, w_up, w_down)


def _moe_residual(x, gain, shift, scale, gate2, w_router, router_bias, w_gate, w_up, w_down, ws_gate, ws_up,
                  ws_down):
    d = x.shape[1]
    t, tp = _norm_mod_pack(x, gain, shift, scale)
    idx8, gate8, rank8, counts = _router(t, w_router, router_bias)
    pos_flat, tile_expert, n_active, last_tile_row, n_tiles = _moe_plan(idx8, rank8, counts, _MOE_TILE_ROWS)
    xs = _dispatch(tp, pos_flat, last_tile_row, n_active, n_tiles * _MOE_TILE_ROWS, _MOE_TILE_ROWS)
    ys = _grouped_ffn(xs, tile_expert, n_active, w_gate, w_up, w_down, _MOE_TILE_ROWS)
    sff = ws_gate.shape[1]
    ff = w_gate.shape[2]
    n_sh = sff // ff
    to_chunks = lambda w: w.reshape(d, n_sh, ff).transpose(1, 0, 2).astype(_BF16)
    shared = _ffn(t, to_chunks(ws_gate), to_chunks(ws_up), ws_down.reshape(n_sh, ff, d).astype(_BF16))
    return _combine(ys, pos_flat, gate8.T, x, shared, gate2)


def _attention_layer(x, h_c, mods, mods_c, g_mix, w_in, w_out, rpb, q_gain, k_gain):
    shift1, scale1, gate1 = mods[:3]
    cshift1, cscale1 = mods_c[:2]
    n_tok = x.shape[0]
    na_w = NA_HEADS * HEAD_DIM
    gq_w = GQA_Q_HEADS * HEAD_DIM
    gkv_w = GQA_KV_HEADS * HEAD_DIM
    q_cols = na_w + gq_w
    o_nk, o_nv, o_gk, o_gv = q_cols, q_cols + na_w, q_cols + 2 * na_w, q_cols + 2 * na_w + gkv_w
    w_plain = jnp.concatenate([w_in[:, :na_w], w_in[:, o_nk:o_gk], w_in[:, o_gv:]], axis=1).astype(_BF16)
    w_normed = jnp.concatenate([w_in[:, na_w:q_cols], w_in[:, o_gk:o_gv]], axis=1).astype(_BF16)

    a_x = _norm_mod(x, g_mix, shift1, scale1, _BF16)
    a_c = _norm_mod(h_c, g_mix, cshift1, cscale1, _BF16)

    p_plain = _matmul(a_x, w_plain, _BF16)
    p_normed = _matmul(a_x, w_normed, _F32)
    c_plain = _matmul(a_c, w_plain[:, na_w:], _BF16)
    c_gk = _matmul(a_c, w_normed[:, gq_w:], _F32)

    cos, sin = _rope_tables(n_tok)
    gains = jnp.concatenate([jnp.tile(q_gain, GQA_Q_HEADS), jnp.tile(k_gain, GQA_KV_HEADS)])
    post = jnp.concatenate([jnp.full((gq_w,), HEAD_DIM ** -0.5 * np.log2(np.e), _F32), jnp.ones((gkv_w,), _F32)])
    qk = _qk_norm_rope(p_normed, gains, post, cos, sin)
    ck = _qk_norm_rope(c_gk, gains[gq_w:], post[gq_w:], None, None)

    o_a = _na_attention(p_plain, c_plain, rpb, NA_HEADS, 0, NA_HEADS, 2 * NA_HEADS, 0, NA_HEADS)
    k_all = jnp.concatenate([qk[:, gq_w:], ck], axis=0)
    v_all = jnp.concatenate([p_plain[:, 3 * na_w:], c_plain[:, 2 * na_w:]], axis=0)
    o_b = _gqa_attention(qk[:, :gq_w], k_all, v_all, GQA_KV_HEADS)
    o = jnp.concatenate([o_a, o_b], axis=1)
    return _matmul(o, w_out.astype(_BF16), res=x, gate=gate1)


def _pool_layer(x, mods, g_mix, w_pool, pool_scale):
    shift1, scale1, gate1 = mods[:3]
    diffs = _pool_diffs(x, g_mix, shift1, scale1)
    return _matmul(diffs, w_pool.astype(_BF16), grouped=True, res=x, gate=gate1, colscale=pool_scale)


def kernel(x, c, ctx, c_ctx, w_mod, b_mod, g_mix, g_ffn, attn_w_in, attn_w_out, na_rpb, q_gain, k_gain, pool_w,
           pool_scale, moe_w_router, moe_router_bias, moe_w_gate, moe_w_up, moe_w_down, moe_ws_gate, moe_ws_up,
           moe_ws_down, g_final):
    batch, seq, d = x.shape
    depth = w_mod.shape[0]
    outs = []
    for bi in range(batch):
        xb = x[bi]
        h_c = ctx[bi]
        conds = jnp.stack([c[bi], c_ctx])
        for i in range(depth):
            j = i // 2
            mod_rows = _adaln(conds, w_mod[i], b_mod[i])
            mods = jnp.split(mod_rows[0], N_MOD)
            mods_c = jnp.split(mod_rows[1], N_MOD)
            if i % 2 == 0:
                xb = _attention_layer(xb, h_c, mods, mods_c, g_mix[i], attn_w_in[j], attn_w_out[j], na_rpb[j],
                                      q_gain[j], k_gain[j])
            else:
                xb = _pool_layer(xb, mods, g_mix[i], pool_w[j], pool_scale[j])
            assert not any(l % 2 == 0 for l in range(i + 1, depth)), "context-stream update not implemented"
            xb = _moe_residual(xb, g_ffn[i], mods[3], mods[4], mods[5], moe_w_router[i], moe_router_bias[i],
                               moe_w_gate[i], moe_w_up[i], moe_w_down[i], moe_ws_gate[i], moe_ws_up[i],
                               moe_ws_down[i])
        outs.append(_norm_mod(xb, g_final, None, None, _F32))
    return jnp.stack(outs)
```

```python
import functools

import numpy as np
import jax
import jax.numpy as jnp
from jax import lax
from jax.experimental import pallas as pl
from jax.experimental.pallas import tpu as pltpu

GRID_W = 64
HEAD_DIM = 128
NA_HEADS = 16
GQA_Q_HEADS = 16
GQA_KV_HEADS = 4
NA_WIN_H = 8
NA_WIN_W = 16
ROPE_THETA = 10000.0
POOL_WINDOWS = (2, 4, 8, 16)
N_EXPERT_GROUPS = 8
TOPK_GROUPS = 4
TOP_K = 8
ROUTED_SCALE = 2.5
N_MOD = 6
RMS_EPS = 1e-6

_F32 = jnp.float32
_BF16 = jnp.bfloat16
_NEG = -1e30
_VMEM_LIMIT_BYTES = 56 * 1024 * 1024


def _cparams(*sem):
    return pltpu.CompilerParams(dimension_semantics=sem, vmem_limit_bytes=_VMEM_LIMIT_BYTES)


def _tile(n, target, mult=8):
    t = min(n, target)
    while t > mult and (n % t or t % mult):
        t -= mult
    assert n % t == 0, (n, target, mult)
    return t


def _silu(x):
    return x / (1.0 + jnp.exp(-x))


def _adaln_kernel(c_ref, w_ref, b_ref, o_ref, *, kc):
    n_rows, d, _ = c_ref.shape
    bn = w_ref.shape[1]
    accs = [jnp.zeros((1, bn), _F32) for _ in range(n_rows)]
    for k0 in range(0, d, kc):
        w = w_ref[k0:k0 + kc, :]
        for r in range(n_rows):
            cc = _silu(c_ref[r, k0:k0 + kc, :])
            accs[r] = accs[r] + jnp.sum(cc * w, axis=0, keepdims=True)
    o_ref[...] = jnp.concatenate(accs, axis=0) + b_ref[...]


def _adaln(conds, w_mod, b_mod):
    n_rows, d = conds.shape
    n = w_mod.shape[1]
    bn = _tile(n, 512, 128)
    return pl.pallas_call(
        functools.partial(_adaln_kernel, kc=_tile(d, 256)),
        grid=(n // bn,),
        in_specs=[
            pl.BlockSpec((n_rows, d, 1), lambda j: (0, 0, 0)),
            pl.BlockSpec((d, bn), lambda j: (0, j)),
            pl.BlockSpec((1, bn), lambda j: (0, j)),
        ],
        out_specs=pl.BlockSpec((n_rows, bn), lambda j: (0, j)),
        out_shape=jax.ShapeDtypeStruct((n_rows, n), _F32),
        compiler_params=_cparams("arbitrary"),
        name="adaln",
    )(conds[:, :, None], w_mod, b_mod[None, :])


def _rms(x, gain):
    return x * lax.rsqrt(jnp.mean(x * x, axis=-1, keepdims=True) + RMS_EPS) * gain


def _norm_mod_kernel(x_ref, g_ref, sh_ref, sc_ref, o_ref):
    y = _rms(x_ref[...], g_ref[...])
    o_ref[...] = (y * (1.0 + sc_ref[...]) + sh_ref[...]).astype(o_ref.dtype)


def _norm_kernel(x_ref, g_ref, o_ref):
    o_ref[...] = _rms(x_ref[...], g_ref[...]).astype(o_ref.dtype)


def _norm_mod(x, gain, shift, scale, out_dtype):
    t, d = x.shape
    tm = _tile(t, 256)
    row = pl.BlockSpec((1, d), lambda i: (0, 0))
    blk = pl.BlockSpec((tm, d), lambda i: (i, 0))
    if shift is None:
        return pl.pallas_call(
            _norm_kernel, grid=(t // tm,), in_specs=[blk, row], out_specs=blk,
            out_shape=jax.ShapeDtypeStruct((t, d), out_dtype),
            compiler_params=_cparams("parallel"), name="rmsnorm",
        )(x, gain[None, :])
    return pl.pallas_call(
        _norm_mod_kernel, grid=(t // tm,), in_specs=[blk, row, row, row], out_specs=blk,
        out_shape=jax.ShapeDtypeStruct((t, d), out_dtype),
        compiler_params=_cparams("parallel"), name="rmsnorm_modulate",
    )(x, gain[None, :], shift[None, :], scale[None, :])


def _mm_kernel(a_ref, w_ref, o_ref):
    o_ref[...] = jnp.dot(a_ref[...], w_ref[...], preferred_element_type=_F32).astype(o_ref.dtype)


def _mm_res_kernel(a_ref, w_ref, res_ref, gate_ref, cs_ref, o_ref):
    y = jnp.dot(a_ref[...], w_ref[...], preferred_element_type=_F32) * cs_ref[...]
    o_ref[...] = res_ref[...] + gate_ref[...] * y


def _matmul(a, w, out_dtype=_F32, *, grouped=False, res=None, gate=None, colscale=None, tm_target=512,
            tn_target=1024):
    m = a.shape[0]
    if grouped:
        g, kk, tn = w.shape
        n = g * tn
        a_spec = lambda tm: pl.BlockSpec((tm, kk), lambda j, i: (i, j))
        w_spec = pl.BlockSpec((None, kk, tn), lambda j, i: (j, 0, 0))
    else:
        kk, n = w.shape
        tn = _tile(n, tn_target, 128)
        a_spec = lambda tm: pl.BlockSpec((tm, kk), lambda j, i: (i, 0))
        w_spec = pl.BlockSpec((kk, tn), lambda j, i: (0, j))
    tm = _tile(m, tm_target)
    o_spec = pl.BlockSpec((tm, tn), lambda j, i: (i, j))
    row = pl.BlockSpec((1, tn), lambda j, i: (0, j))
    grid = (n // tn, m // tm)
    if res is None:
        return pl.pallas_call(
            _mm_kernel, grid=grid, in_specs=[a_spec(tm), w_spec], out_specs=o_spec,
            out_shape=jax.ShapeDtypeStruct((m, n), out_dtype),
            compiler_params=_cparams("parallel", "parallel"), name="matmul",
        )(a, w)
    if colscale is None:
        colscale = jnp.ones((n,), _F32)
    return pl.pallas_call(
        _mm_res_kernel, grid=grid, in_specs=[a_spec(tm), w_spec, o_spec, row, row], out_specs=o_spec,
        out_shape=jax.ShapeDtypeStruct((m, n), _F32),
        compiler_params=_cparams("parallel", "parallel"), name="matmul_gated_residual",
    )(a, w, res, gate[None, :], colscale[None, :])


def _mm_cols_kernel(tab_ref, a_ref, w_ref, o_ref):
    del tab_ref
    _mm_kernel(a_ref, w_ref, o_ref)


def _matmul_cols(a, w, col_tiles, tn, out_dtype):
    m, kk = a.shape
    tm = _tile(m, 512)
    n = len(col_tiles) * tn
    return pl.pallas_call(
        _mm_cols_kernel,
        grid_spec=pltpu.PrefetchScalarGridSpec(
            num_scalar_prefetch=1, grid=(len(col_tiles), m // tm),
            in_specs=[pl.BlockSpec((tm, kk), lambda j, i, tab: (i, 0)),
                      pl.BlockSpec((kk, tn), lambda j, i, tab: (0, tab[j]))],
            out_specs=pl.BlockSpec((tm, tn), lambda j, i, tab: (i, j)),
        ),
        out_shape=jax.ShapeDtypeStruct((m, n), out_dtype),
        compiler_params=_cparams("parallel", "parallel"), name="matmul_cols",
    )(jnp.asarray(col_tiles, jnp.int32), a, w)


def _qk_norm_rope_kernel(x_ref, gain_ref, post_ref, cos_ref, sin_ref, o_ref, *, rope):
    n_heads = x_ref.shape[1] // HEAD_DIM
    if rope:
        cos = cos_ref[...]
        sin = sin_ref[...]
        lane = lax.broadcasted_iota(jnp.int32, cos.shape, 1)
        first_half = (lane % (HEAD_DIM // 2)) < (HEAD_DIM // 4)
    for h in range(n_heads):
        sl = slice(h * HEAD_DIM, (h + 1) * HEAD_DIM)
        y = _rms(x_ref[:, sl], gain_ref[:, sl])
        if rope:
            partner = jnp.where(first_half, pltpu.roll(y, HEAD_DIM - HEAD_DIM // 4, 1),
                                pltpu.roll(y, HEAD_DIM // 4, 1))
            y = y * cos + partner * sin
        o_ref[:, sl] = (y * post_ref[:, sl]).astype(o_ref.dtype)


def _qk_norm_rope(x, gains, post_scale, cos, sin_signed):
    t, w = x.shape
    tm = _tile(t, 256)
    rope = cos is not None
    if not rope:
        cos = jnp.zeros((t, HEAD_DIM), _F32)
        sin_signed = cos
    blk = pl.BlockSpec((tm, w), lambda i: (i, 0))
    tab = pl.BlockSpec((tm, HEAD_DIM), lambda i: (i, 0))
    vec = pl.BlockSpec((1, w), lambda i: (0, 0))
    return pl.pallas_call(
        functools.partial(_qk_norm_rope_kernel, rope=rope), grid=(t // tm,),
        in_specs=[blk, vec, vec, tab, tab], out_specs=blk,
        out_shape=jax.ShapeDtypeStruct((t, w), _BF16),
        compiler_params=_cparams("parallel"), name="qk_norm_rope",
    )(x, gains[None, :], post_scale[None, :], cos, sin_signed)


def _rope_tables(n):
    axis_dim = HEAD_DIM // 2
    t = jnp.arange(n)
    row = (t // GRID_W).astype(_F32)
    col = (t % GRID_W).astype(_F32)
    inv = ROPE_THETA ** (-jnp.arange(0, axis_dim, 2, dtype=_F32) / axis_dim)
    ang_r = row[:, None] * inv
    ang_c = col[:, None] * inv
    cos = jnp.concatenate([jnp.cos(ang_r)] * 2 + [jnp.cos(ang_c)] * 2, axis=-1)
    sin = jnp.concatenate([-jnp.sin(ang_r), jnp.sin(ang_r), -jnp.sin(ang_c), jnp.sin(ang_c)], axis=-1)
    return cos, sin


def _dot_nt(a, b):
    return lax.dot_general(a, b, (((1,), (1,)), ((), ())), preferred_element_type=_F32)


_GQA_Q_BLOCK = 256
_GQA_KEY_BLOCK = 3328
_MXU_WIDTH = 256


def _gqa_kernel(q_ref, k_ref, v_ref, o_ref, m_ref, l_ref, acc_ref, *, tk):
    group = q_ref.shape[1] // HEAD_DIM
    m_ref[...] = jnp.full(m_ref.shape, -jnp.inf, _F32)
    l_ref[...] = jnp.zeros(l_ref.shape, _F32)
    acc_ref[...] = jnp.zeros(acc_ref.shape, _F32)

    def body(j, carry):
        start = pl.multiple_of(j * tk, tk)
        for g in range(group):
            s = _dot_nt(q_ref[:, g * HEAD_DIM:(g + 1) * HEAD_DIM], k_ref[pl.ds(start, tk), :])
            m_prev = m_ref[g]
            m_new = jnp.maximum(m_prev, jnp.max(s, axis=-1, keepdims=True))
            alpha = jnp.exp2(m_prev - m_new)
            p = jnp.exp2(s - m_new)
            l_ref[g] = alpha * l_ref[g] + jnp.sum(p, axis=-1, keepdims=True)
            acc_ref[g] = alpha * acc_ref[g] + jnp.dot(p.astype(_BF16), v_ref[pl.ds(start, tk), :],
                                                      preferred_element_type=_F32)
            m_ref[g] = m_new
        return carry

    lax.fori_loop(0, k_ref.shape[0] // tk, body, 0)
    for g in range(group):
        o_ref[:, g * HEAD_DIM:(g + 1) * HEAD_DIM] = (acc_ref[g] / l_ref[g]).astype(o_ref.dtype)


def _gqa_attention(q, k, v, n_kv_heads, n_q_heads):
    l = q.shape[0]
    qw = n_q_heads * HEAD_DIM
    s = k.shape[0]
    gw = qw // n_kv_heads
    tq = _tile(l, _GQA_Q_BLOCK)
    tk = _tile(s, _GQA_KEY_BLOCK, 128)
    group = gw // HEAD_DIM
    col = pltpu.VMEM((group, tq, 1), _F32)
    return pl.pallas_call(
        functools.partial(_gqa_kernel, tk=tk),
        grid=(n_kv_heads, l // tq),
        in_specs=[
            pl.BlockSpec((tq, gw), lambda h, i: (i, h)),
            pl.BlockSpec((s, HEAD_DIM), lambda h, i: (0, h)),
            pl.BlockSpec((s, HEAD_DIM), lambda h, i: (0, h)),
        ],
        out_specs=pl.BlockSpec((tq, gw), lambda h, i: (i, h)),
        out_shape=jax.ShapeDtypeStruct((l, qw), _BF16),
        scratch_shapes=[col, col, pltpu.VMEM((group, tq, HEAD_DIM), _F32)],
        compiler_params=_cparams("parallel", "parallel"), name="gqa_attention",
    )(q, k, v)


_NA_QROWS = 8
_NA_KROWS = 16


def _na_patterns():
    first = dict(delta=0, a=[max(i - NA_WIN_H // 2, 0) for i in range(_NA_QROWS)])
    inner = dict(delta=-(NA_WIN_H // 2), a=list(range(_NA_QROWS)))
    last = dict(delta=-(_NA_KROWS - _NA_QROWS),
                a=[min(i + NA_WIN_H // 2, _NA_KROWS - NA_WIN_H) for i in range(_NA_QROWS)])
    return first, inner, last


def _na_kernel(q_ref, k_ref, v_ref, kc_ref, vc_ref, tb_ref, o_ref, bias_ref, *, n_blocks, scale):
    b = pl.program_id(1)
    w = GRID_W
    lane_lo = lax.broadcasted_iota(jnp.int32, (w, 2 * w), 1) < w

    def build_bias(pat):
        for i in range(_NA_QROWS):
            for jj in range(_NA_KROWS // 2):
                j0 = 2 * jj
                in0 = 0 <= j0 - pat["a"][i] < NA_WIN_H
                in1 = 0 <= j0 + 1 - pat["a"][i] < NA_WIN_H
                rel0 = pat["delta"] + j0 - i + NA_WIN_H - 1
                if not (in0 or in1):
                    tile = jnp.full((w, 2 * w), _NEG, _F32)
                else:
                    tile = tb_ref[rel0 + 1]
                    if not in1:
                        tile = jnp.where(lane_lo, tile, _NEG)
                    elif not in0:
                        tile = jnp.where(lane_lo, _NEG, tile)
                bias_ref[i * w:(i + 1) * w, j0 * w:(j0 + 2) * w] = tile

    first, inner, last = _na_patterns()
    pl.when(b == 0)(functools.partial(build_bias, first))
    pl.when(b == 1)(functools.partial(build_bias, inner))
    pl.when(b == n_blocks - 1)(functools.partial(build_bias, last))

    slab_row = jnp.where(b == 0, 0, jnp.where(b == n_blocks - 1, (n_blocks - 2) * _NA_QROWS,
                                              b * _NA_QROWS - NA_WIN_H // 2))
    start = pl.multiple_of(slab_row * w, w)
    q = q_ref[...]
    s_loc = _dot_nt(q, k_ref[pl.ds(start, _NA_KROWS * w), :]) * scale + bias_ref[...]
    s_ctx = _dot_nt(q, kc_ref[...]) * scale
    m = jnp.maximum(jnp.max(s_loc, axis=-1, keepdims=True), jnp.max(s_ctx, axis=-1, keepdims=True))
    p_loc = jnp.exp(s_loc - m)
    p_ctx = jnp.exp(s_ctx - m)
    denom = jnp.sum(p_loc, axis=-1, keepdims=True) + jnp.sum(p_ctx, axis=-1, keepdims=True)
    o = (jnp.dot(p_loc.astype(_BF16), v_ref[pl.ds(start, _NA_KROWS * w), :], preferred_element_type=_F32)
         + jnp.dot(p_ctx.astype(_BF16), vc_ref[...], preferred_element_type=_F32))
    o_ref[...] = (o / denom).astype(o_ref.dtype)


def _na_bias_table(rpb):
    qc = np.arange(GRID_W)
    win_start = np.clip(qc - NA_WIN_W // 2, 0, GRID_W - NA_WIN_W)
    rel = qc[None, :] - qc[:, None]
    in_win = (qc[None, :] >= win_start[:, None]) & (qc[None, :] < win_start[:, None] + NA_WIN_W)
    rel_idx = np.clip(rel + NA_WIN_W - 1, 0, 2 * NA_WIN_W - 2)
    toe = jnp.where(in_win[None, None], rpb[:, :, rel_idx], _NEG).astype(_F32)
    neg = jnp.full_like(toe[:, :1], _NEG)
    lo = jnp.concatenate([neg, toe], axis=1)
    hi = jnp.concatenate([toe, neg], axis=1)
    return jnp.concatenate([lo, hi], axis=-1)


def _na_attention(px, pc, rpb, n_heads, q_head0, k_head0, v_head0, ck_head0, cv_head0):
    l = px.shape[0]
    lc = pc.shape[0]
    rows = l // GRID_W
    n_blocks = rows // _NA_QROWS
    assert rows % _NA_QROWS == 0 and n_blocks >= 3
    tq = _NA_QROWS * GRID_W
    tb = _na_bias_table(rpb)
    head_blk = lambda n, h0: pl.BlockSpec((n, HEAD_DIM), lambda h, b: (0, h0 + h))
    return pl.pallas_call(
        functools.partial(_na_kernel, n_blocks=n_blocks, scale=HEAD_DIM ** -0.5),
        grid=(n_heads, n_blocks),
        in_specs=[
            pl.BlockSpec((tq, HEAD_DIM), lambda h, b: (b, q_head0 + h)),
            head_blk(l, k_head0), head_blk(l, v_head0), head_blk(lc, ck_head0), head_blk(lc, cv_head0),
            pl.BlockSpec((None, 2 * NA_WIN_H, GRID_W, 2 * GRID_W), lambda h, b: (h, 0, 0, 0)),
        ],
        out_specs=pl.BlockSpec((tq, HEAD_DIM), lambda h, b: (b, h)),
        out_shape=jax.ShapeDtypeStruct((l, n_heads * HEAD_DIM), _BF16),
        scratch_shapes=[pltpu.VMEM((tq, _NA_KROWS * GRID_W), _F32)],
        compiler_params=_cparams("arbitrary", "arbitrary"), name="na_attention",
    )(px, px, px, pc, pc, tb)


_POOL_HALO = 16


def _pool_kernel(prev_ref, cur_ref, next_ref, g_ref, sh_ref, sc_ref, o_ref, *, n_tokens):
    i = pl.program_id(0)
    tm, d = cur_ref.shape
    dg = d // len(POOL_WINDOWS)
    x = jnp.concatenate([prev_ref[...], cur_ref[...], next_ref[...]], axis=0)
    a = _rms(x, g_ref[...]) * (1.0 + sc_ref[...]) + sh_ref[...]
    pos = i * tm - _POOL_HALO + lax.broadcasted_iota(jnp.int32, (tm + 2 * _POOL_HALO, 1), 0)
    a = jnp.where((pos >= 0) & (pos < n_tokens), a, 0.0)
    t = i * tm + lax.broadcasted_iota(jnp.int32, (tm, 1), 0)
    for g, w in enumerate(POOL_WINDOWS):
        ag = a[:, g * dg:(g + 1) * dg]
        tot = jnp.zeros((tm, dg), _F32)
        for off in range(-(w // 2), w - w // 2):
            tot = tot + ag[_POOL_HALO + off:_POOL_HALO + off + tm, :]
        cnt = jnp.minimum(t + (w - w // 2), n_tokens) - jnp.maximum(t - w // 2, 0)
        diff = tot / cnt.astype(_F32) - ag[_POOL_HALO:_POOL_HALO + tm, :]
        o_ref[:, g * dg:(g + 1) * dg] = diff.astype(o_ref.dtype)


def _pool_diffs(x, gain, shift, scale):
    t, d = x.shape
    tm = _tile(t, 256, _POOL_HALO)
    hb = tm // _POOL_HALO
    n_halo_blocks = t // _POOL_HALO
    row = pl.BlockSpec((1, d), lambda i: (0, 0))
    return pl.pallas_call(
        functools.partial(_pool_kernel, n_tokens=t), grid=(t // tm,),
        in_specs=[
            pl.BlockSpec((_POOL_HALO, d), lambda i: (jnp.maximum(i * hb - 1, 0), 0)),
            pl.BlockSpec((tm, d), lambda i: (i, 0)),
            pl.BlockSpec((_POOL_HALO, d), lambda i: (jnp.minimum((i + 1) * hb, n_halo_blocks - 1), 0)),
            row, row, row,
        ],
        out_specs=pl.BlockSpec((tm, d), lambda i: (i, 0)),
        out_shape=jax.ShapeDtypeStruct((t, d), _BF16),
        compiler_params=_cparams("parallel"), name="pool_diffs",
    )(x, x, x, gain[None, :], shift[None, :], scale[None, :])


def _router_kernel(t_ref, w_ref, b_ref, idx_ref, gate_ref, rank_ref, cnt_ref):
    n_exp = w_ref.shape[0]
    tm = t_ref.shape[0]
    per_group = n_exp // N_EXPERT_GROUPS

    @pl.when(pl.program_id(0) == 0)
    def _():
        cnt_ref[...] = jnp.zeros(cnt_ref.shape, _F32)

    logits = _dot_nt(w_ref[...], t_ref[...])
    scores = 1.0 / (1.0 + jnp.exp(-logits))
    biased = scores + b_ref[...]

    def first_max(vals):
        idx = lax.broadcasted_iota(jnp.int32, vals.shape, 0).astype(_F32)
        m = jnp.max(vals, axis=0, keepdims=True)
        first = jnp.min(jnp.where(vals == m, idx, float(vals.shape[0])), axis=0, keepdims=True)
        return m, idx == first

    def take_top(vals, k):
        hits = []
        for _ in range(k):
            _, hit = first_max(vals)
            hits.append(hit)
            vals = jnp.where(hit, -jnp.inf, vals)
        return hits

    def union(hits):
        sel = jnp.zeros(hits[0].shape, _F32)
        for hit in hits:
            sel = jnp.where(hit, 1.0, sel)
        return sel

    group_scores = []
    for g in range(N_EXPERT_GROUPS):
        v = biased[g * per_group:(g + 1) * per_group, :]
        m1, hit = first_max(v)
        m2 = jnp.max(jnp.where(hit, -jnp.inf, v), axis=0, keepdims=True)
        group_scores.append(m1 + m2)
    gsel = union(take_top(jnp.concatenate(group_scores, axis=0), TOPK_GROUPS))
    keep = jnp.concatenate([jnp.broadcast_to(gsel[g:g + 1, :], (per_group, tm)) for g in range(N_EXPERT_GROUPS)],
                           axis=0)
    hits = take_top(jnp.where(keep > 0.0, biased, -jnp.inf), TOP_K)
    esel = union(hits)
    wsel = esel * scores
    gates = wsel / jnp.sum(wsel, axis=0, keepdims=True) * ROUTED_SCALE

    earlier = (lax.broadcasted_iota(jnp.int32, (tm, tm), 0) < lax.broadcasted_iota(jnp.int32, (tm, tm), 1))
    rank = cnt_ref[...] + jnp.dot(esel.astype(_BF16), jnp.where(earlier, 1.0, 0.0).astype(_BF16),
                                  preferred_element_type=_F32)
    cnt_ref[...] += jnp.sum(esel, axis=1, keepdims=True)

    eidx = lax.broadcasted_iota(jnp.int32, (n_exp, tm), 0).astype(_F32)
    pick = lambda hit, v: jnp.sum(jnp.where(hit, v, 0.0), axis=0, keepdims=True)
    idx_ref[...] = jnp.concatenate([pick(h, eidx) for h in hits], axis=0).astype(jnp.int32)
    gate_ref[...] = jnp.concatenate([pick(h, gates) for h in hits], axis=0)
    rank_ref[...] = jnp.concatenate([pick(h, rank) for h in hits], axis=0).astype(jnp.int32)


def _router(t, w_router, router_bias):
    n_tok, d = t.shape
    n_exp = w_router.shape[1]
    tm = _tile(n_tok, 512, 128)
    per_tok = pl.BlockSpec((TOP_K, tm), lambda i: (0, i))
    return pl.pallas_call(
        _router_kernel, grid=(n_tok // tm,),
        in_specs=[
            pl.BlockSpec((tm, d), lambda i: (i, 0)),
            pl.BlockSpec((n_exp, d), lambda i: (0, 0)),
            pl.BlockSpec((n_exp, 1), lambda i: (0, 0)),
        ],
        out_specs=[per_tok, per_tok, per_tok, pl.BlockSpec((n_exp, 1), lambda i: (0, 0))],
        out_shape=[jax.ShapeDtypeStruct((TOP_K, n_tok), jnp.int32), jax.ShapeDtypeStruct((TOP_K, n_tok), _F32),
                   jax.ShapeDtypeStruct((TOP_K, n_tok), jnp.int32), jax.ShapeDtypeStruct((n_exp, 1), _F32)],
        compiler_params=_cparams("arbitrary"), name="moe_router",
    )(t, w_router.T.astype(_BF16), router_bias[:, None])


_MOE_TILE_ROWS = 256


def _packed_cols(d):
    return d // 2


def _pack_pairs(y):
    half = y.shape[1] // 2
    bits = lambda v: lax.bitcast_convert_type(v.astype(_BF16).astype(_F32), jnp.uint32)
    return (bits(y[:, :half]) & jnp.uint32(0xFFFF0000)) | (bits(y[:, half:]) >> jnp.uint32(16))


def _unpack_pairs(w):
    hi = lax.bitcast_convert_type(w & jnp.uint32(0xFFFF0000), _F32)
    lo = lax.bitcast_convert_type(w << jnp.uint32(16), _F32)
    return hi, lo


def _norm_mod_pack_kernel(x_ref, g_ref, sh_ref, sc_ref, o_ref, p_ref):
    y = _rms(x_ref[...], g_ref[...]) * (1.0 + sc_ref[...]) + sh_ref[...]
    o_ref[...] = y.astype(o_ref.dtype)
    p_ref[...] = _pack_pairs(y)


def _norm_mod_pack(x, gain, shift, scale):
    t, d = x.shape
    tm = _tile(t, 256)
    row = pl.BlockSpec((1, d), lambda i: (0, 0))
    blk = pl.BlockSpec((tm, d), lambda i: (i, 0))
    return pl.pallas_call(
        _norm_mod_pack_kernel, grid=(t // tm,), in_specs=[blk, row, row, row],
        out_specs=[blk, pl.BlockSpec((tm, _packed_cols(d)), lambda i: (i, 0))],
        out_shape=[jax.ShapeDtypeStruct((t, d), _BF16), jax.ShapeDtypeStruct((t, _packed_cols(d)), jnp.uint32)],
        compiler_params=_cparams("parallel"), name="rmsnorm_modulate_pack",
    )(x, gain[None, :], shift[None, :], scale[None, :])


def _moe_plan(idx8, rank8, counts, tile_rows):
    n_exp = counts.shape[0]
    n_tok = idx8.shape[1]
    cnt = counts[:, 0].astype(jnp.int32)
    padded = (cnt + tile_rows - 1) // tile_rows * tile_rows
    ends = jnp.cumsum(padded)
    starts = ends - padded
    onehot = idx8[:, :, None] == jnp.arange(n_exp, dtype=jnp.int32)
    pos8 = rank8 + jnp.sum(jnp.where(onehot, starts, 0), axis=-1)
    n_tiles = TOP_K * n_tok // tile_rows + n_exp
    tile_row0 = jnp.arange(n_tiles, dtype=jnp.int32) * tile_rows
    tile_expert = jnp.minimum(jnp.sum(tile_row0[:, None] >= ends[None, :], axis=1), n_exp - 1).astype(jnp.int32)
    n_active = (ends[-1:] // tile_rows).astype(jnp.int32)
    last_tile_row = jnp.where(padded > 0, ends - tile_rows, -1).astype(jnp.int32)
    return pos8.T.reshape(-1), tile_expert, n_active, last_tile_row, n_tiles


def _dispatch_kernel(pos_ref, last_ref, na_ref, tp_ref, xs_ref, zero_ref, sem, zsem, *, tile_rows):
    i = pl.program_id(0)
    tb = tp_ref.shape[0]
    n_exp = last_ref.shape[0]
    n_tiles = xs_ref.shape[0] // tile_rows

    @pl.when(i == 0)
    def _():
        zero_ref[...] = jnp.zeros(zero_ref.shape, zero_ref.dtype)

        def fill(row0):
            return pltpu.make_async_copy(zero_ref, xs_ref.at[pl.ds(pl.multiple_of(row0, tile_rows), tile_rows)],
                                         zsem)

        def each(e, start):
            tail = na_ref[0] + e

            def go(row0):
                cp = fill(row0)
                cp.start() if start else cp.wait()

            pl.when(last_ref[e] >= 0)(lambda: go(jnp.maximum(last_ref[e], 0)))
            pl.when(tail < n_tiles)(lambda: go(jnp.minimum(tail, n_tiles - 1) * tile_rows))

        lax.fori_loop(0, n_exp, lambda e, c: (each(e, True), c)[1], 0)
        lax.fori_loop(0, n_exp, lambda e, c: (each(e, False), c)[1], 0)

    def row(r, c):
        base = (i * tb + r) * TOP_K
        for k in range(TOP_K):
            pltpu.make_async_copy(tp_ref.at[pl.ds(r, 1)], xs_ref.at[pl.ds(pos_ref[base + k], 1)], sem).start()
        return c

    lax.fori_loop(0, tb, row, 0)
    for _ in range(TOP_K):
        pltpu.make_async_copy(tp_ref, xs_ref.at[pl.ds(0, tb)], sem).wait()


def _dispatch(tp, pos_flat, last_tile_row, n_active, n_rows, tile_rows):
    n_tok, c = tp.shape
    tb = _tile(n_tok, 256)
    return pl.pallas_call(
        functools.partial(_dispatch_kernel, tile_rows=tile_rows),
        grid_spec=pltpu.PrefetchScalarGridSpec(
            num_scalar_prefetch=3, grid=(n_tok // tb,),
            in_specs=[pl.BlockSpec((tb, c), lambda i, pos, last, na: (i, 0))],
            out_specs=pl.BlockSpec(memory_space=pl.ANY),
            scratch_shapes=[pltpu.VMEM((tile_rows, c), jnp.uint32), pltpu.SemaphoreType.DMA,
                            pltpu.SemaphoreType.DMA],
        ),
        out_shape=jax.ShapeDtypeStruct((n_rows, c), jnp.uint32),
        compiler_params=_cparams("arbitrary"), name="moe_dispatch",
    )(pos_flat, last_tile_row, n_active, tp)


def _grouped_ffn_kernel(te_ref, na_ref, xs_ref, wg_ref, wu_ref, wd_ref, ys_ref, wg_bf, wu_bf, wd_bf):
    i = pl.program_id(0)
    active = i < na_ref[0]
    new_expert = jnp.logical_or(i == 0, te_ref[i] != te_ref[jnp.maximum(i - 1, 0)])

    @pl.when(jnp.logical_and(active, new_expert))
    def _():
        wg_bf[...] = wg_ref[...].astype(_BF16)
        wu_bf[...] = wu_ref[...].astype(_BF16)
        wd_bf[...] = wd_ref[...].astype(_BF16)

    @pl.when(active)
    def _():
        half = wg_bf.shape[0] // 2
        hi, lo = _unpack_pairs(xs_ref[...])
        hi = hi.astype(_BF16)
        lo = lo.astype(_BF16)
        dot = lambda a, b: jnp.dot(a, b, preferred_element_type=_F32)
        hg = dot(hi, wg_bf[:half, :]) + dot(lo, wg_bf[half:, :])
        hu = dot(hi, wu_bf[:half, :]) + dot(lo, wu_bf[half:, :])
        h = (_silu(hg) * hu).astype(_BF16)
        ys_ref[...] = _pack_pairs(dot(h, wd_bf[...]))

    @pl.when(jnp.logical_not(active))
    def _():
        ys_ref[...] = jnp.zeros(ys_ref.shape, ys_ref.dtype)


def _grouped_ffn(xs, tile_expert, n_active, w_gate, w_up, w_down, tile_rows):
    n_rows, c = xs.shape
    n_exp, d, ff = w_gate.shape
    n_tiles = n_rows // tile_rows
    row_blk = pl.BlockSpec((tile_rows, c), lambda i, te, na: (jnp.minimum(i, jnp.maximum(na[0] - 1, 0)), 0))
    return pl.pallas_call(
        _grouped_ffn_kernel,
        grid_spec=pltpu.PrefetchScalarGridSpec(
            num_scalar_prefetch=2, grid=(n_tiles,),
            in_specs=[row_blk,
                      pl.BlockSpec((None, d, ff), lambda i, te, na: (te[i], 0, 0)),
                      pl.BlockSpec((None, d, ff), lambda i, te, na: (te[i], 0, 0)),
                      pl.BlockSpec((None, ff, d), lambda i, te, na: (te[i], 0, 0))],
            out_specs=pl.BlockSpec((tile_rows, c), lambda i, te, na: (i, 0)),
            scratch_shapes=[pltpu.VMEM((d, ff), _BF16), pltpu.VMEM((d, ff), _BF16), pltpu.VMEM((ff, d), _BF16)],
        ),
        out_shape=jax.ShapeDtypeStruct((n_rows, c), jnp.uint32),
        compiler_params=_cparams("arbitrary"), name="moe_grouped_ffn",
    )(tile_expert, n_active, xs, w_gate, w_up, w_down)


def _combine_kernel(pos_ref, ys_ref, g8_ref, x_ref, sh_ref, g2_ref, o_ref, buf, sem):
    i = pl.program_id(0)
    tb, d = x_ref.shape
    half = d // 2

    def row(r, c):
        base = (i * tb + r) * TOP_K
        for k in range(TOP_K):
            pltpu.make_async_copy(ys_ref.at[pl.ds(pos_ref[base + k], 1)], buf.at[k, pl.ds(r, 1)], sem).start()
        return c

    lax.fori_loop(0, tb, row, 0)
    for k in range(TOP_K):
        pltpu.make_async_copy(ys_ref.at[pl.ds(0, tb)], buf.at[k], sem).wait()

    g8 = g8_ref[...]
    acc_hi = jnp.zeros((tb, half), _F32)
    acc_lo = jnp.zeros((tb, half), _F32)
    for k in range(TOP_K):
        hi, lo = _unpack_pairs(buf[k])
        acc_hi = acc_hi + g8[:, k:k + 1] * hi
        acc_lo = acc_lo + g8[:, k:k + 1] * lo
    o_ref[:, :half] = x_ref[:, :half] + g2_ref[:, :half] * (sh_ref[:, :half] + acc_hi)
    o_ref[:, half:] = x_ref[:, half:] + g2_ref[:, half:] * (sh_ref[:, half:] + acc_lo)


def _combine(ys, pos_flat, gate8, x, shared, gate2):
    n_tok, d = x.shape
    c = ys.shape[1]
    tb = _tile(n_tok, 128)
    blk = pl.BlockSpec((tb, d), lambda i, pos: (i, 0))
    return pl.pallas_call(
        _combine_kernel,
        grid_spec=pltpu.PrefetchScalarGridSpec(
            num_scalar_prefetch=1, grid=(n_tok // tb,),
            in_specs=[pl.BlockSpec(memory_space=pl.ANY),
                      pl.BlockSpec((tb, TOP_K), lambda i, pos: (i, 0)),
                      blk, blk, pl.BlockSpec((1, d), lambda i, pos: (0, 0))],
            out_specs=blk,
            scratch_shapes=[pltpu.VMEM((TOP_K, tb, c), jnp.uint32), pltpu.SemaphoreType.DMA],
        ),
        out_shape=jax.ShapeDtypeStruct((n_tok, d), _F32),
        compiler_params=_cparams("arbitrary"), name="moe_combine",
    )(pos_flat, ys, gate8, x, shared, gate2[None, :])


def _ffn_kernel(t_ref, wg_ref, wu_ref, wd_ref, o_ref):
    e = pl.program_id(1)
    t = t_ref[...]
    h = _silu(jnp.dot(t, wg_ref[...], preferred_element_type=_F32)) * jnp.dot(t, wu_ref[...],
                                                                               preferred_element_type=_F32)
    y = jnp.dot(h.astype(_BF16), wd_ref[...], preferred_element_type=_F32)

    @pl.when(e == 0)
    def _():
        o_ref[...] = y

    @pl.when(e > 0)
    def _():
        o_ref[...] += y


def _ffn(t, w_gate, w_up, w_down):
    n_tok, d = t.shape
    n_chunks, _, ff = w_gate.shape
    tm = _tile(n_tok, 512)
    blk = pl.BlockSpec((tm, d), lambda i, e: (i, 0))
    return pl.pallas_call(
        _ffn_kernel, grid=(n_tok // tm, n_chunks),
        in_specs=[blk,
                  pl.BlockSpec((None, d, ff), lambda i, e: (e, 0, 0)),
                  pl.BlockSpec((None, d, ff), lambda i, e: (e, 0, 0)),
                  pl.BlockSpec((None, ff, d), lambda i, e: (e, 0, 0))],
        out_specs=blk,
        out_shape=jax.ShapeDtypeStruct((n_tok, d), _F32),
        compiler_params=_cparams("parallel", "arbitrary"), name="shared_ffn",
    )(t, w_gate, w_up, w_down)


def _moe_residual(x, gain, shift, scale, gate2, w_router, router_bias, w_gate, w_up, w_down, ws_gate, ws_up,
                  ws_down):
    d = x.shape[1]
    t, tp = _norm_mod_pack(x, gain, shift, scale)
    idx8, gate8, rank8, counts = _router(t, w_router, router_bias)
    pos_flat, tile_expert, n_active, last_tile_row, n_tiles = _moe_plan(idx8, rank8, counts, _MOE_TILE_ROWS)
    xs = _dispatch(tp, pos_flat, last_tile_row, n_active, n_tiles * _MOE_TILE_ROWS, _MOE_TILE_ROWS)
    ys = _grouped_ffn(xs, tile_expert, n_active, w_gate, w_up, w_down, _MOE_TILE_ROWS)
    sff = ws_gate.shape[1]
    ff = w_gate.shape[2]
    n_sh = sff // ff
    to_chunks = lambda w: w.reshape(d, n_sh, ff).transpose(1, 0, 2).astype(_BF16)
    shared = _ffn(t, to_chunks(ws_gate), to_chunks(ws_up), ws_down.reshape(n_sh, ff, d).astype(_BF16))
    return _combine(ys, pos_flat, gate8.T, x, shared, gate2)


def _attention_layer(x, h_c, mods, mods_c, g_mix, w_in, w_out, rpb, q_gain, k_gain):
    shift1, scale1, gate1 = mods[:3]
    cshift1, cscale1 = mods_c[:2]
    n_tok = x.shape[0]
    na_w = NA_HEADS * HEAD_DIM
    gq_w = GQA_Q_HEADS * HEAD_DIM
    gkv_w = GQA_KV_HEADS * HEAD_DIM
    q_cols = na_w + gq_w
    o_nk, o_gk, o_gv, o_end = q_cols, q_cols + 2 * na_w, q_cols + 2 * na_w + gkv_w, w_in.shape[1]
    tn = int(np.gcd.reduce([na_w, gq_w, gkv_w, 512]))
    tiles = lambda lo, hi: list(range(lo // tn, hi // tn))
    w_bf = w_in.astype(_BF16)

    a_x = _norm_mod(x, g_mix, shift1, scale1, _BF16)
    a_c = _norm_mod(h_c, g_mix, cshift1, cscale1, _BF16)

    kv_plain = tiles(o_nk, o_gk) + tiles(o_gv, o_end)
    p_plain = _matmul_cols(a_x, w_bf, tiles(0, na_w) + kv_plain, tn, _BF16)
    p_normed = _matmul_cols(a_x, w_bf, tiles(na_w, q_cols) + tiles(o_gk, o_gv), tn, _F32)
    c_plain = _matmul_cols(a_c, w_bf, kv_plain, tn, _BF16)
    c_gk = _matmul_cols(a_c, w_bf, tiles(o_gk, o_gv), tn, _F32)

    cos, sin = _rope_tables(n_tok)
    gains = jnp.concatenate([jnp.tile(q_gain, GQA_Q_HEADS), jnp.tile(k_gain, GQA_KV_HEADS)])
    post = jnp.concatenate([jnp.full((gq_w,), HEAD_DIM ** -0.5 * np.log2(np.e), _F32), jnp.ones((gkv_w,), _F32)])
    qk = _qk_norm_rope(p_normed, gains, post, cos, sin)
    ck = _qk_norm_rope(c_gk, gains[gq_w:], post[gq_w:], None, None)

    o_a = _na_attention(p_plain, c_plain, rpb, NA_HEADS, 0, NA_HEADS, 2 * NA_HEADS, 0, NA_HEADS)
    k_all = jnp.concatenate([qk[:, gq_w:], ck], axis=0)
    v_all = jnp.concatenate([p_plain[:, 3 * na_w:], c_plain[:, 2 * na_w:]], axis=0)
    o_b = _gqa_attention(qk, k_all, v_all, GQA_KV_HEADS, GQA_Q_HEADS)
    o = jnp.concatenate([o_a, o_b], axis=1)
    return _matmul(o, w_out.astype(_BF16), res=x, gate=gate1)


def _pool_layer(x, mods, g_mix, w_pool, pool_scale):
    shift1, scale1, gate1 = mods[:3]
    diffs = _pool_diffs(x, g_mix, shift1, scale1)
    return _matmul(diffs, w_pool.astype(_BF16), grouped=True, res=x, gate=gate1, colscale=pool_scale)


def kernel(x, c, ctx, c_ctx, w_mod, b_mod, g_mix, g_ffn, attn_w_in, attn_w_out, na_rpb, q_gain, k_gain, pool_w,
           pool_scale, moe_w_router, moe_router_bias, moe_w_gate, moe_w_up, moe_w_down, moe_ws_gate, moe_ws_up,
           moe_ws_down, g_final):
    batch, seq, d = x.shape
    depth = w_mod.shape[0]
    outs = []
    for bi in range(batch):
        xb = x[bi]
        h_c = ctx[bi]
        conds = jnp.stack([c[bi], c_ctx])
        for i in range(depth):
            j = i // 2
            mod_rows = _adaln(conds, w_mod[i], b_mod[i])
            mods = jnp.split(mod_rows[0], N_MOD)
            mods_c = jnp.split(mod_rows[1], N_MOD)
            if i % 2 == 0:
                xb = _attention_layer(xb, h_c, mods, mods_c, g_mix[i], attn_w_in[j], attn_w_out[j], na_rpb[j],
                                      q_gain[j], k_gain[j])
            else:
                xb = _pool_layer(xb, mods, g_mix[i], pool_w[j], pool_scale[j])
            assert not any(l % 2 == 0 for l in range(i + 1, depth)), "context-stream update not implemented"
            xb = _moe_residual(xb, g_ffn[i], mods[3], mods[4], mods[5], moe_w_router[i], moe_router_bias[i],
                               moe_w_gate[i], moe_w_up[i], moe_w_down[i], moe_ws_gate[i], moe_ws_up[i],
                               moe_ws_down[i])
        outs.append(_norm_mod(xb, g_final, None, None, _F32))
    return jnp.stack(outs)
```

```python
import functools

import numpy as np
import jax
import jax.numpy as jnp
from jax import lax
from jax.experimental import pallas as pl
from jax.experimental.pallas import tpu as pltpu

GRID_W = 64
HEAD_DIM = 128
NA_HEADS = 16
GQA_Q_HEADS = 16
GQA_KV_HEADS = 4
NA_WIN_H = 8
NA_WIN_W = 16
ROPE_THETA = 10000.0
POOL_WINDOWS = (2, 4, 8, 16)
N_EXPERT_GROUPS = 8
TOPK_GROUPS = 4
TOP_K = 8
ROUTED_SCALE = 2.5
N_MOD = 6
RMS_EPS = 1e-6

_F32 = jnp.float32
_BF16 = jnp.bfloat16
_NEG = -1e30
_VMEM_LIMIT_BYTES = 56 * 1024 * 1024


def _cparams(*sem):
    return pltpu.CompilerParams(dimension_semantics=sem, vmem_limit_bytes=_VMEM_LIMIT_BYTES)


def _tile(n, target, mult=8):
    t = min(n, target) // mult * mult
    while t > mult and n % t:
        t -= mult
    assert t > 0 and n % t == 0, (n, target, mult)
    return t


def _silu(x):
    return x / (1.0 + jnp.exp(-x))


def _adaln_kernel(c_ref, w_ref, b_ref, o_ref, *, kc):
    n_rows, d, _ = c_ref.shape
    bn = w_ref.shape[1]
    accs = [jnp.zeros((1, bn), _F32) for _ in range(n_rows)]
    for k0 in range(0, d, kc):
        w = w_ref[k0:k0 + kc, :]
        for r in range(n_rows):
            cc = _silu(c_ref[r, k0:k0 + kc, :])
            accs[r] = accs[r] + jnp.sum(cc * w, axis=0, keepdims=True)
    o_ref[...] = jnp.concatenate(accs, axis=0) + b_ref[...]


def _adaln(conds, w_mod, layer, b_mod):
    n_rows, d = conds.shape
    n = w_mod.shape[2]
    bn = _tile(n, 512, 128)
    return pl.pallas_call(
        functools.partial(_adaln_kernel, kc=_tile(d, 256)),
        grid=(n // bn,),
        in_specs=[
            pl.BlockSpec((n_rows, d, 1), lambda j: (0, 0, 0)),
            pl.BlockSpec((None, d, bn), lambda j: (layer, 0, j)),
            pl.BlockSpec((1, bn), lambda j: (0, j)),
        ],
        out_specs=pl.BlockSpec((n_rows, bn), lambda j: (0, j)),
        out_shape=jax.ShapeDtypeStruct((n_rows, n), _F32),
        compiler_params=_cparams("arbitrary"),
        name="adaln",
    )(conds[:, :, None], w_mod, b_mod[None, :])


def _rms(x, gain):
    return x * lax.rsqrt(jnp.mean(x * x, axis=-1, keepdims=True) + RMS_EPS) * gain


def _norm_mod_kernel(x_ref, g_ref, sh_ref, sc_ref, o_ref):
    y = _rms(x_ref[...], g_ref[...])
    o_ref[...] = (y * (1.0 + sc_ref[...]) + sh_ref[...]).astype(o_ref.dtype)


def _norm_mod(x, gain, shift, scale, out_dtype):
    t, d = x.shape
    tm = _tile(t, 256)
    row = pl.BlockSpec((1, d), lambda i: (0, 0))
    blk = pl.BlockSpec((tm, d), lambda i: (i, 0))
    return pl.pallas_call(
        _norm_mod_kernel, grid=(t // tm,), in_specs=[blk, row, row, row], out_specs=blk,
        out_shape=jax.ShapeDtypeStruct((t, d), out_dtype),
        compiler_params=_cparams("parallel"), name="rmsnorm_modulate",
    )(x, gain[None, :], shift[None, :], scale[None, :])


def _mm_kernel(a_ref, w_ref, o_ref):
    o_ref[...] = jnp.dot(a_ref[...], w_ref[...], preferred_element_type=_F32).astype(o_ref.dtype)


def _mm_res_kernel(a_ref, w_ref, res_ref, gate_ref, cs_ref, o_ref):
    y = jnp.dot(a_ref[...], w_ref[...], preferred_element_type=_F32) * cs_ref[...]
    o_ref[...] = res_ref[...] + gate_ref[...] * y


def _matmul(a, w, out_dtype=_F32, *, grouped=False, res=None, gate=None, colscale=None, tm_target=512,
            tn_target=1024):
    m = a.shape[0]
    if grouped:
        g, kk, tn = w.shape
        n = g * tn
        a_spec = lambda tm: pl.BlockSpec((tm, kk), lambda j, i: (i, j))
        w_spec = pl.BlockSpec((None, kk, tn), lambda j, i: (j, 0, 0))
    else:
        kk, n = w.shape
        tn = _tile(n, tn_target, 128)
        a_spec = lambda tm: pl.BlockSpec((tm, kk), lambda j, i: (i, 0))
        w_spec = pl.BlockSpec((kk, tn), lambda j, i: (0, j))
    tm = _tile(m, tm_target)
    o_spec = pl.BlockSpec((tm, tn), lambda j, i: (i, j))
    row = pl.BlockSpec((1, tn), lambda j, i: (0, j))
    grid = (n // tn, m // tm)
    if res is None:
        return pl.pallas_call(
            _mm_kernel, grid=grid, in_specs=[a_spec(tm), w_spec], out_specs=o_spec,
            out_shape=jax.ShapeDtypeStruct((m, n), out_dtype),
            compiler_params=_cparams("parallel", "parallel"), name="matmul",
        )(a, w)
    if colscale is None:
        colscale = jnp.ones((n,), _F32)
    return pl.pallas_call(
        _mm_res_kernel, grid=grid, in_specs=[a_spec(tm), w_spec, o_spec, row, row], out_specs=o_spec,
        out_shape=jax.ShapeDtypeStruct((m, n), _F32),
        compiler_params=_cparams("parallel", "parallel"), name="matmul_gated_residual",
    )(a, w, res, gate[None, :], colscale[None, :])


def _mm2_res_kernel(a1_ref, a2_ref, w1_ref, w2_ref, res_ref, gate_ref, o_ref):
    y = (jnp.dot(a1_ref[...], w1_ref[...], preferred_element_type=_F32)
         + jnp.dot(a2_ref[...], w2_ref[...], preferred_element_type=_F32))
    o_ref[...] = res_ref[...] + gate_ref[...] * y


def _matmul2_res(a1, a2, w, res, gate):
    m, k1 = a1.shape
    assert a2.shape == (m, k1) and w.shape[0] == 2 * k1
    n = w.shape[1]
    tm = _tile(m, 512)
    tn = _tile(n, 1024, 128)
    a_spec = pl.BlockSpec((tm, k1), lambda j, i: (i, 0))
    o_spec = pl.BlockSpec((tm, tn), lambda j, i: (i, j))
    return pl.pallas_call(
        _mm2_res_kernel, grid=(n // tn, m // tm),
        in_specs=[a_spec, a_spec, pl.BlockSpec((k1, tn), lambda j, i: (0, j)),
                  pl.BlockSpec((k1, tn), lambda j, i: (1, j)), o_spec, pl.BlockSpec((1, tn), lambda j, i: (0, j))],
        out_specs=o_spec, out_shape=jax.ShapeDtypeStruct((m, n), _F32),
        compiler_params=_cparams("parallel", "parallel"), name="matmul2_gated_residual",
    )(a1, a2, w, w, res, gate[None, :])


def _mm_cols_kernel(tab_ref, a_ref, w_ref, o_ref):
    del tab_ref
    _mm_kernel(a_ref, w_ref, o_ref)


def _matmul_cols(a, w, col_tiles, tn, out_dtype):
    m, kk = a.shape
    tm = _tile(m, 512)
    n = len(col_tiles) * tn
    return pl.pallas_call(
        _mm_cols_kernel,
        grid_spec=pltpu.PrefetchScalarGridSpec(
            num_scalar_prefetch=1, grid=(len(col_tiles), m // tm),
            in_specs=[pl.BlockSpec((tm, kk), lambda j, i, tab: (i, 0)),
                      pl.BlockSpec((kk, tn), lambda j, i, tab: (0, tab[j]))],
            out_specs=pl.BlockSpec((tm, tn), lambda j, i, tab: (i, j)),
        ),
        out_shape=jax.ShapeDtypeStruct((m, n), out_dtype),
        compiler_params=_cparams("parallel", "parallel"), name="matmul_cols",
    )(jnp.asarray(col_tiles, jnp.int32), a, w)


def _qk_norm_rope_kernel(x_ref, gain_ref, post_ref, cos_ref, sin_ref, o_ref, *, rope):
    n_heads = x_ref.shape[1] // HEAD_DIM
    if rope:
        cos = cos_ref[...]
        sin = sin_ref[...]
        lane = lax.broadcasted_iota(jnp.int32, cos.shape, 1)
        first_half = (lane % (HEAD_DIM // 2)) < (HEAD_DIM // 4)
    for h in range(n_heads):
        sl = slice(h * HEAD_DIM, (h + 1) * HEAD_DIM)
        y = _rms(x_ref[:, sl], gain_ref[:, sl])
        if rope:
            partner = jnp.where(first_half, pltpu.roll(y, HEAD_DIM - HEAD_DIM // 4, 1),
                                pltpu.roll(y, HEAD_DIM // 4, 1))
            y = y * cos + partner * sin
        o_ref[:, sl] = (y * post_ref[:, sl]).astype(o_ref.dtype)


def _qk_norm_rope(x, gains, post_scale, cos, sin_signed):
    t, w = x.shape
    tm = _tile(t, 256)
    rope = cos is not None
    if not rope:
        cos = jnp.zeros((t, HEAD_DIM), _F32)
        sin_signed = cos
    blk = pl.BlockSpec((tm, w), lambda i: (i, 0))
    tab = pl.BlockSpec((tm, HEAD_DIM), lambda i: (i, 0))
    vec = pl.BlockSpec((1, w), lambda i: (0, 0))
    return pl.pallas_call(
        functools.partial(_qk_norm_rope_kernel, rope=rope), grid=(t // tm,),
        in_specs=[blk, vec, vec, tab, tab], out_specs=blk,
        out_shape=jax.ShapeDtypeStruct((t, w), _BF16),
        compiler_params=_cparams("parallel"), name="qk_norm_rope",
    )(x, gains[None, :], post_scale[None, :], cos, sin_signed)


def _rope_tables(n):
    axis_dim = HEAD_DIM // 2
    t = jnp.arange(n)
    row = (t // GRID_W).astype(_F32)
    col = (t % GRID_W).astype(_F32)
    inv = ROPE_THETA ** (-jnp.arange(0, axis_dim, 2, dtype=_F32) / axis_dim)
    ang_r = row[:, None] * inv
    ang_c = col[:, None] * inv
    cos = jnp.concatenate([jnp.cos(ang_r)] * 2 + [jnp.cos(ang_c)] * 2, axis=-1)
    sin = jnp.concatenate([-jnp.sin(ang_r), jnp.sin(ang_r), -jnp.sin(ang_c), jnp.sin(ang_c)], axis=-1)
    return cos, sin


def _dot_nt(a, b):
    return lax.dot_general(a, b, (((1,), (1,)), ((), ())), preferred_element_type=_F32)


_GQA_Q_BLOCK = 256
_GQA_KEY_BLOCK = 3328
_MXU_WIDTH = 256


def _gqa_kernel(q_ref, k_ref, v_ref, o_ref, m_ref, acc_ref, *, tk):
    group = q_ref.shape[1] // HEAD_DIM
    m_ref[...] = jnp.full(m_ref.shape, -jnp.inf, _F32)
    acc_ref[...] = jnp.zeros(acc_ref.shape, _F32)

    def body(j, carry):
        start = pl.multiple_of(j * tk, tk)
        for g in range(group):
            s = _dot_nt(q_ref[:, g * HEAD_DIM:(g + 1) * HEAD_DIM], k_ref[pl.ds(start, tk), :])
            m_prev = m_ref[g]
            m_new = jnp.maximum(m_prev, jnp.max(s, axis=-1, keepdims=True))
            p = jnp.exp2(s - m_new)
            acc_ref[g] = jnp.exp2(m_prev - m_new) * acc_ref[g] + jnp.dot(
                p.astype(_BF16), v_ref[pl.ds(start, tk), :], preferred_element_type=_F32)
            m_ref[g] = m_new
        return carry

    lax.fori_loop(0, k_ref.shape[0] // tk, body, 0)
    for g in range(group):
        acc = acc_ref[g]
        o_ref[:, g * HEAD_DIM:(g + 1) * HEAD_DIM] = (acc[:, :HEAD_DIM] / acc[:, HEAD_DIM:HEAD_DIM + 1]).astype(
            o_ref.dtype)


def _gqa_attention(q, k, v, n_kv_heads, n_q_heads):
    l = q.shape[0]
    qw = n_q_heads * HEAD_DIM
    s = k.shape[0]
    gw = qw // n_kv_heads
    tq = _tile(l, _GQA_Q_BLOCK)
    tk = _tile(s, _GQA_KEY_BLOCK, 128)
    group = gw // HEAD_DIM
    ones_col = jnp.zeros((s, n_kv_heads, HEAD_DIM), v.dtype).at[:, :, 0].set(1)
    v_aug = jnp.concatenate([v.reshape(s, n_kv_heads, HEAD_DIM), ones_col], axis=-1).reshape(s, -1)
    return pl.pallas_call(
        functools.partial(_gqa_kernel, tk=tk),
        grid=(n_kv_heads, l // tq),
        in_specs=[
            pl.BlockSpec((tq, gw), lambda h, i: (i, h)),
            pl.BlockSpec((s, HEAD_DIM), lambda h, i: (0, h), pipeline_mode=pl.Buffered(1)),
            pl.BlockSpec((s, 2 * HEAD_DIM), lambda h, i: (0, h), pipeline_mode=pl.Buffered(1)),
        ],
        out_specs=pl.BlockSpec((tq, gw), lambda h, i: (i, h)),
        out_shape=jax.ShapeDtypeStruct((l, qw), _BF16),
        scratch_shapes=[pltpu.VMEM((group, tq, 1), _F32), pltpu.VMEM((group, tq, 2 * HEAD_DIM), _F32)],
        compiler_params=_cparams("arbitrary", "arbitrary"), name="gqa_attention",
    )(q, k, v_aug)


_NA_QROWS = 8
_NA_KROWS = 16


def _na_patterns():
    first = dict(delta=0, a=[max(i - NA_WIN_H // 2, 0) for i in range(_NA_QROWS)])
    inner = dict(delta=-(NA_WIN_H // 2), a=list(range(_NA_QROWS)))
    last = dict(delta=-(_NA_KROWS - _NA_QROWS),
                a=[min(i + NA_WIN_H // 2, _NA_KROWS - NA_WIN_H) for i in range(_NA_QROWS)])
    return first, inner, last


def _na_kernel(q_ref, k_ref, v_ref, kc_ref, vc_ref, tb_ref, o_ref, bias_ref, *, n_blocks, scale):
    b = pl.program_id(1)
    w = GRID_W
    lane_lo = lax.broadcasted_iota(jnp.int32, (w, 2 * w), 1) < w

    def build_bias(pat):
        for i in range(_NA_QROWS):
            for jj in range(_NA_KROWS // 2):
                j0 = 2 * jj
                in0 = 0 <= j0 - pat["a"][i] < NA_WIN_H
                in1 = 0 <= j0 + 1 - pat["a"][i] < NA_WIN_H
                rel0 = pat["delta"] + j0 - i + NA_WIN_H - 1
                if not (in0 or in1):
                    tile = jnp.full((w, 2 * w), _NEG, _F32)
                else:
                    tile = tb_ref[rel0 + 1]
                    if not in1:
                        tile = jnp.where(lane_lo, tile, _NEG)
                    elif not in0:
                        tile = jnp.where(lane_lo, _NEG, tile)
                bias_ref[i * w:(i + 1) * w, j0 * w:(j0 + 2) * w] = tile

    first, inner, last = _na_patterns()
    pl.when(b == 0)(functools.partial(build_bias, first))
    pl.when(b == 1)(functools.partial(build_bias, inner))
    pl.when(b == n_blocks - 1)(functools.partial(build_bias, last))

    slab_row = jnp.where(b == 0, 0, jnp.where(b == n_blocks - 1, (n_blocks - 2) * _NA_QROWS,
                                              b * _NA_QROWS - NA_WIN_H // 2))
    start = pl.multiple_of(slab_row * w, w)
    q = q_ref[...]
    s_loc = _dot_nt(q, k_ref[pl.ds(start, _NA_KROWS * w), :]) * scale + bias_ref[...]
    s_ctx = _dot_nt(q, kc_ref[...]) * scale
    m = jnp.maximum(jnp.max(s_loc, axis=-1, keepdims=True), jnp.max(s_ctx, axis=-1, keepdims=True))
    p_loc = jnp.exp(s_loc - m)
    p_ctx = jnp.exp(s_ctx - m)
    denom = jnp.sum(p_loc, axis=-1, keepdims=True) + jnp.sum(p_ctx, axis=-1, keepdims=True)
    o = (jnp.dot(p_loc.astype(_BF16), v_ref[pl.ds(start, _NA_KROWS * w), :], preferred_element_type=_F32)
         + jnp.dot(p_ctx.astype(_BF16), vc_ref[...], preferred_element_type=_F32))
    o_ref[...] = (o / denom).astype(o_ref.dtype)


def _na_bias_table(rpb):
    qc = np.arange(GRID_W)
    win_start = np.clip(qc - NA_WIN_W // 2, 0, GRID_W - NA_WIN_W)
    rel = qc[None, :] - qc[:, None]
    in_win = (qc[None, :] >= win_start[:, None]) & (qc[None, :] < win_start[:, None] + NA_WIN_W)
    rel_idx = np.clip(rel + NA_WIN_W - 1, 0, 2 * NA_WIN_W - 2)
    toe = jnp.where(in_win[None, None], rpb[:, :, rel_idx], _NEG).astype(_F32)
    neg = jnp.full_like(toe[:, :1], _NEG)
    lo = jnp.concatenate([neg, toe], axis=1)
    hi = jnp.concatenate([toe, neg], axis=1)
    return jnp.concatenate([lo, hi], axis=-1)


def _na_attention(px, pc, rpb, n_heads, q_head0, k_head0, v_head0, ck_head0, cv_head0):
    l = px.shape[0]
    lc = pc.shape[0]
    rows = l // GRID_W
    n_blocks = rows // _NA_QROWS
    assert rows % _NA_QROWS == 0 and n_blocks >= 3
    tq = _NA_QROWS * GRID_W
    tb = _na_bias_table(rpb)
    head_blk = lambda n, h0: pl.BlockSpec((n, HEAD_DIM), lambda h, b: (0, h0 + h))
    return pl.pallas_call(
        functools.partial(_na_kernel, n_blocks=n_blocks, scale=HEAD_DIM ** -0.5),
        grid=(n_heads, n_blocks),
        in_specs=[
            pl.BlockSpec((tq, HEAD_DIM), lambda h, b: (b, q_head0 + h)),
            head_blk(l, k_head0), head_blk(l, v_head0), head_blk(lc, ck_head0), head_blk(lc, cv_head0),
            pl.BlockSpec((None, 2 * NA_WIN_H, GRID_W, 2 * GRID_W), lambda h, b: (h, 0, 0, 0)),
        ],
        out_specs=pl.BlockSpec((tq, HEAD_DIM), lambda h, b: (b, h)),
        out_shape=jax.ShapeDtypeStruct((l, n_heads * HEAD_DIM), _BF16),
        scratch_shapes=[pltpu.VMEM((tq, _NA_KROWS * GRID_W), _F32)],
        compiler_params=_cparams("arbitrary", "arbitrary"), name="na_attention",
    )(px, px, px, pc, pc, tb)


_POOL_HALO = 16


def _pool_kernel(prev_ref, cur_ref, next_ref, g_ref, sh_ref, sc_ref, o_ref, *, n_tokens):
    i = pl.program_id(0)
    tm, d = cur_ref.shape
    dg = d // len(POOL_WINDOWS)
    x = jnp.concatenate([prev_ref[...], cur_ref[...], next_ref[...]], axis=0)
    a = _rms(x, g_ref[...]) * (1.0 + sc_ref[...]) + sh_ref[...]
    pos = i * tm - _POOL_HALO + lax.broadcasted_iota(jnp.int32, (tm + 2 * _POOL_HALO, 1), 0)
    a = jnp.where((pos >= 0) & (pos < n_tokens), a, 0.0)
    t = i * tm + lax.broadcasted_iota(jnp.int32, (tm, 1), 0)
    for g, w in enumerate(POOL_WINDOWS):
        ag = a[:, g * dg:(g + 1) * dg]
        tot = jnp.zeros((tm, dg), _F32)
        for off in range(-(w // 2), w - w // 2):
            tot = tot + ag[_POOL_HALO + off:_POOL_HALO + off + tm, :]
        cnt = jnp.minimum(t + (w - w // 2), n_tokens) - jnp.maximum(t - w // 2, 0)
        diff = tot / cnt.astype(_F32) - ag[_POOL_HALO:_POOL_HALO + tm, :]
        o_ref[:, g * dg:(g + 1) * dg] = diff.astype(o_ref.dtype)


def _pool_diffs(x, gain, shift, scale):
    t, d = x.shape
    tm = _tile(t, 256, _POOL_HALO)
    hb = tm // _POOL_HALO
    n_halo_blocks = t // _POOL_HALO
    row = pl.BlockSpec((1, d), lambda i: (0, 0))
    return pl.pallas_call(
        functools.partial(_pool_kernel, n_tokens=t), grid=(t // tm,),
        in_specs=[
            pl.BlockSpec((_POOL_HALO, d), lambda i: (jnp.maximum(i * hb - 1, 0), 0)),
            pl.BlockSpec((tm, d), lambda i: (i, 0)),
            pl.BlockSpec((_POOL_HALO, d), lambda i: (jnp.minimum((i + 1) * hb, n_halo_blocks - 1), 0)),
            row, row, row,
        ],
        out_specs=pl.BlockSpec((tm, d), lambda i: (i, 0)),
        out_shape=jax.ShapeDtypeStruct((t, d), _BF16),
        compiler_params=_cparams("parallel"), name="pool_diffs",
    )(x, x, x, gain[None, :], shift[None, :], scale[None, :])


def _router_kernel(t_ref, w_ref, b_ref, idx_ref, gate_ref, rank_ref, cnt_ref):
    n_exp = w_ref.shape[0]
    tm = t_ref.shape[0]
    per_group = n_exp // N_EXPERT_GROUPS

    @pl.when(pl.program_id(0) == 0)
    def _():
        cnt_ref[...] = jnp.zeros(cnt_ref.shape, _F32)

    logits = _dot_nt(w_ref[...], t_ref[...])
    scores = 1.0 / (1.0 + jnp.exp(-logits))
    biased = scores + b_ref[...]

    def first_max(vals):
        idx = lax.broadcasted_iota(jnp.int32, vals.shape, 0).astype(_F32)
        m = jnp.max(vals, axis=0, keepdims=True)
        first = jnp.min(jnp.where(vals == m, idx, float(vals.shape[0])), axis=0, keepdims=True)
        return m, idx == first

    def take_top(vals, k):
        hits = []
        for _ in range(k):
            _, hit = first_max(vals)
            hits.append(hit)
            vals = jnp.where(hit, -jnp.inf, vals)
        return hits

    def union(hits):
        sel = jnp.zeros(hits[0].shape, _F32)
        for hit in hits:
            sel = jnp.where(hit, 1.0, sel)
        return sel

    group_scores = []
    for g in range(N_EXPERT_GROUPS):
        v = biased[g * per_group:(g + 1) * per_group, :]
        m1, hit = first_max(v)
        m2 = jnp.max(jnp.where(hit, -jnp.inf, v), axis=0, keepdims=True)
        group_scores.append(m1 + m2)
    gsel = union(take_top(jnp.concatenate(group_scores, axis=0), TOPK_GROUPS))
    keep = jnp.concatenate([jnp.broadcast_to(gsel[g:g + 1, :], (per_group, tm)) for g in range(N_EXPERT_GROUPS)],
                           axis=0)
    hits = take_top(jnp.where(keep > 0.0, biased, -jnp.inf), TOP_K)
    esel = union(hits)
    wsel = esel * scores
    gates = wsel / jnp.sum(wsel, axis=0, keepdims=True) * ROUTED_SCALE

    earlier = (lax.broadcasted_iota(jnp.int32, (tm, tm), 0) < lax.broadcasted_iota(jnp.int32, (tm, tm), 1))
    rank = cnt_ref[...] + jnp.dot(esel.astype(_BF16), jnp.where(earlier, 1.0, 0.0).astype(_BF16),
                                  preferred_element_type=_F32)
    cnt_ref[...] += jnp.sum(esel, axis=1, keepdims=True)

    eidx = lax.broadcasted_iota(jnp.int32, (n_exp, tm), 0).astype(_F32)
    pick = lambda hit, v: jnp.sum(jnp.where(hit, v, 0.0), axis=0, keepdims=True)
    idx_ref[...] = jnp.concatenate([pick(h, eidx) for h in hits], axis=0).astype(jnp.int32)
    gate_ref[...] = jnp.concatenate([pick(h, gates) for h in hits], axis=0)
    rank_ref[...] = jnp.concatenate([pick(h, rank) for h in hits], axis=0).astype(jnp.int32)


def _router(t, w_router, router_bias):
    n_tok, d = t.shape
    n_exp = w_router.shape[1]
    tm = _tile(n_tok, 512, 128)
    per_tok = pl.BlockSpec((TOP_K, tm), lambda i: (0, i))
    return pl.pallas_call(
        _router_kernel, grid=(n_tok // tm,),
        in_specs=[
            pl.BlockSpec((tm, d), lambda i: (i, 0)),
            pl.BlockSpec((n_exp, d), lambda i: (0, 0)),
            pl.BlockSpec((n_exp, 1), lambda i: (0, 0)),
        ],
        out_specs=[per_tok, per_tok, per_tok, pl.BlockSpec((n_exp, 1), lambda i: (0, 0))],
        out_shape=[jax.ShapeDtypeStruct((TOP_K, n_tok), jnp.int32), jax.ShapeDtypeStruct((TOP_K, n_tok), _F32),
                   jax.ShapeDtypeStruct((TOP_K, n_tok), jnp.int32), jax.ShapeDtypeStruct((n_exp, 1), _F32)],
        compiler_params=_cparams("arbitrary"), name="moe_router",
    )(t, w_router.T.astype(_BF16), router_bias[:, None])


_MOE_TILE_ROWS = 256


def _packed_cols(d):
    return d // 2


def _pack_pairs(y):
    half = y.shape[1] // 2
    bits = lambda v: lax.bitcast_convert_type(v.astype(_BF16).astype(_F32), jnp.uint32)
    return (bits(y[:, :half]) & jnp.uint32(0xFFFF0000)) | (bits(y[:, half:]) >> jnp.uint32(16))


def _unpack_pairs(w):
    hi = lax.bitcast_convert_type(w & jnp.uint32(0xFFFF0000), _F32)
    lo = lax.bitcast_convert_type(w << jnp.uint32(16), _F32)
    return hi, lo


def _norm_mod_pack_kernel(x_ref, g_ref, sh_ref, sc_ref, o_ref, p_ref):
    y = _rms(x_ref[...], g_ref[...]) * (1.0 + sc_ref[...]) + sh_ref[...]
    o_ref[...] = y.astype(o_ref.dtype)
    p_ref[...] = _pack_pairs(y)


def _norm_mod_pack(x, gain, shift, scale):
    t, d = x.shape
    tm = _tile(t, 256)
    row = pl.BlockSpec((1, d), lambda i: (0, 0))
    blk = pl.BlockSpec((tm, d), lambda i: (i, 0))
    return pl.pallas_call(
        _norm_mod_pack_kernel, grid=(t // tm,), in_specs=[blk, row, row, row],
        out_specs=[blk, pl.BlockSpec((tm, _packed_cols(d)), lambda i: (i, 0))],
        out_shape=[jax.ShapeDtypeStruct((t, d), _BF16), jax.ShapeDtypeStruct((t, _packed_cols(d)), jnp.uint32)],
        compiler_params=_cparams("parallel"), name="rmsnorm_modulate_pack",
    )(x, gain[None, :], shift[None, :], scale[None, :])


def _moe_plan(idx8, rank8, counts, tile_rows):
    n_exp = counts.shape[0]
    n_tok = idx8.shape[1]
    cnt = counts[:, 0].astype(jnp.int32)
    padded = (cnt + tile_rows - 1) // tile_rows * tile_rows
    ends = jnp.cumsum(padded)
    starts = ends - padded
    onehot = idx8[:, :, None] == jnp.arange(n_exp, dtype=jnp.int32)
    pos8 = rank8 + jnp.sum(jnp.where(onehot, starts, 0), axis=-1)
    n_tiles = TOP_K * n_tok // tile_rows + n_exp
    tile_row0 = jnp.arange(n_tiles, dtype=jnp.int32) * tile_rows
    tile_expert = jnp.minimum(jnp.sum(tile_row0[:, None] >= ends[None, :], axis=1), n_exp - 1).astype(jnp.int32)
    n_active = (ends[-1:] // tile_rows).astype(jnp.int32)
    last_tile_row = jnp.where(padded > 0, ends - tile_rows, -1).astype(jnp.int32)
    return pos8.T.reshape(-1), tile_expert, n_active, last_tile_row, n_tiles


def _dispatch_kernel(pos_ref, last_ref, na_ref, tp_ref, xs_ref, zero_ref, sem, zsem, *, tile_rows):
    i = pl.program_id(0)
    tb = tp_ref.shape[0]
    n_exp = last_ref.shape[0]
    n_tiles = xs_ref.shape[0] // tile_rows

    @pl.when(i == 0)
    def _():
        zero_ref[...] = jnp.zeros(zero_ref.shape, zero_ref.dtype)

        def fill(row0):
            return pltpu.make_async_copy(zero_ref, xs_ref.at[pl.ds(pl.multiple_of(row0, tile_rows), tile_rows)],
                                         zsem)

        def each(e, start):
            tail = na_ref[0] + e

            def go(row0):
                cp = fill(row0)
                cp.start() if start else cp.wait()

            pl.when(last_ref[e] >= 0)(lambda: go(jnp.maximum(last_ref[e], 0)))
            pl.when(tail < n_tiles)(lambda: go(jnp.minimum(tail, n_tiles - 1) * tile_rows))

        lax.fori_loop(0, n_exp, lambda e, c: (each(e, True), c)[1], 0)
        lax.fori_loop(0, n_exp, lambda e, c: (each(e, False), c)[1], 0)

    def row(r, c):
        base = (i * tb + r) * TOP_K
        for k in range(TOP_K):
            pltpu.make_async_copy(tp_ref.at[pl.ds(r, 1)], xs_ref.at[pl.ds(pos_ref[base + k], 1)], sem).start()
        return c

    lax.fori_loop(0, tb, row, 0)
    for _ in range(TOP_K):
        pltpu.make_async_copy(tp_ref, xs_ref.at[pl.ds(0, tb)], sem).wait()


def _dispatch(tp, pos_flat, last_tile_row, n_active, n_rows, tile_rows):
    n_tok, c = tp.shape
    tb = _tile(n_tok, 256)
    return pl.pallas_call(
        functools.partial(_dispatch_kernel, tile_rows=tile_rows),
        grid_spec=pltpu.PrefetchScalarGridSpec(
            num_scalar_prefetch=3, grid=(n_tok // tb,),
            in_specs=[pl.BlockSpec((tb, c), lambda i, pos, last, na: (i, 0))],
            out_specs=pl.BlockSpec(memory_space=pl.ANY),
            scratch_shapes=[pltpu.VMEM((tile_rows, c), jnp.uint32), pltpu.SemaphoreType.DMA,
                            pltpu.SemaphoreType.DMA],
        ),
        out_shape=jax.ShapeDtypeStruct((n_rows, c), jnp.uint32),
        compiler_params=_cparams("arbitrary"), name="moe_dispatch",
    )(pos_flat, last_tile_row, n_active, tp)


def _grouped_ffn_kernel(te_ref, na_ref, xs_ref, wg_ref, wu_ref, wd_ref, ys_ref, wg_bf, wu_bf, wd_bf):
    i = pl.program_id(0)
    active = i < na_ref[0]
    new_expert = jnp.logical_or(i == 0, te_ref[i] != te_ref[jnp.maximum(i - 1, 0)])

    @pl.when(jnp.logical_and(active, new_expert))
    def _():
        wg_bf[...] = wg_ref[...].astype(_BF16)
        wu_bf[...] = wu_ref[...].astype(_BF16)
        wd_bf[...] = wd_ref[...].astype(_BF16)

    @pl.when(active)
    def _():
        half = wg_bf.shape[0] // 2
        hi, lo = _unpack_pairs(xs_ref[...])
        hi = hi.astype(_BF16)
        lo = lo.astype(_BF16)
        dot = lambda a, b: jnp.dot(a, b, preferred_element_type=_F32)
        hg = dot(hi, wg_bf[:half, :]) + dot(lo, wg_bf[half:, :])
        hu = dot(hi, wu_bf[:half, :]) + dot(lo, wu_bf[half:, :])
        h = (_silu(hg) * hu).astype(_BF16)
        ys_ref[...] = _pack_pairs(dot(h, wd_bf[...]))

    @pl.when(jnp.logical_not(active))
    def _():
        ys_ref[...] = jnp.zeros(ys_ref.shape, ys_ref.dtype)


def _grouped_ffn(xs, tile_expert, n_active, w_gate, w_up, w_down, layer, tile_rows):
    n_rows, c = xs.shape
    _, n_exp, d, ff = w_gate.shape
    n_tiles = n_rows // tile_rows
    row_blk = pl.BlockSpec((tile_rows, c), lambda i, te, na: (jnp.minimum(i, jnp.maximum(na[0] - 1, 0)), 0))
    return pl.pallas_call(
        _grouped_ffn_kernel,
        grid_spec=pltpu.PrefetchScalarGridSpec(
            num_scalar_prefetch=2, grid=(n_tiles,),
            in_specs=[row_blk,
                      pl.BlockSpec((None, None, d, ff), lambda i, te, na: (layer, te[i], 0, 0)),
                      pl.BlockSpec((None, None, d, ff), lambda i, te, na: (layer, te[i], 0, 0)),
                      pl.BlockSpec((None, None, ff, d), lambda i, te, na: (layer, te[i], 0, 0))],
            out_specs=pl.BlockSpec((tile_rows, c), lambda i, te, na: (i, 0)),
            scratch_shapes=[pltpu.VMEM((d, ff), _BF16), pltpu.VMEM((d, ff), _BF16), pltpu.VMEM((ff, d), _BF16)],
        ),
        out_shape=jax.ShapeDtypeStruct((n_rows, c), jnp.uint32),
        compiler_params=_cparams("arbitrary"), name="moe_grouped_ffn",
    )(tile_expert, n_active, xs, w_gate, w_up, w_down)


def _combine_kernel(pos_ref, ys_ref, g8_ref, x_ref, sh_ref, g2_ref, fg_ref, o_ref, buf, sem, *, final_norm):
    i = pl.program_id(0)
    tb, d = x_ref.shape
    half = d // 2

    def row(r, c):
        base = (i * tb + r) * TOP_K
        for k in range(TOP_K):
            pltpu.make_async_copy(ys_ref.at[pl.ds(pos_ref[base + k], 1)], buf.at[k, pl.ds(r, 1)], sem).start()
        return c

    lax.fori_loop(0, tb, row, 0)
    for k in range(TOP_K):
        pltpu.make_async_copy(ys_ref.at[pl.ds(0, tb)], buf.at[k], sem).wait()

    g8 = g8_ref[...]
    acc_hi = jnp.zeros((tb, half), _F32)
    acc_lo = jnp.zeros((tb, half), _F32)
    for k in range(TOP_K):
        hi, lo = _unpack_pairs(buf[k])
        acc_hi = acc_hi + g8[:, k:k + 1] * hi
        acc_lo = acc_lo + g8[:, k:k + 1] * lo
    out_hi = x_ref[:, :half] + g2_ref[:, :half] * (sh_ref[:, :half] + acc_hi)
    out_lo = x_ref[:, half:] + g2_ref[:, half:] * (sh_ref[:, half:] + acc_lo)
    if final_norm:
        ssq = jnp.sum(out_hi * out_hi, axis=-1, keepdims=True) + jnp.sum(out_lo * out_lo, axis=-1, keepdims=True)
        inv = lax.rsqrt(ssq / d + RMS_EPS)
        out_hi = out_hi * inv * fg_ref[:, :half]
        out_lo = out_lo * inv * fg_ref[:, half:]
    o_ref[:, :half] = out_hi
    o_ref[:, half:] = out_lo


def _combine(ys, pos_flat, gate8, x, shared, gate2, final_gain):
    n_tok, d = x.shape
    c = ys.shape[1]
    tb = _tile(n_tok, 128)
    blk = pl.BlockSpec((tb, d), lambda i, pos: (i, 0))
    vec = pl.BlockSpec((1, d), lambda i, pos: (0, 0))
    final_norm = final_gain is not None
    if not final_norm:
        final_gain = jnp.ones((d,), _F32)
    return pl.pallas_call(
        functools.partial(_combine_kernel, final_norm=final_norm),
        grid_spec=pltpu.PrefetchScalarGridSpec(
            num_scalar_prefetch=1, grid=(n_tok // tb,),
            in_specs=[pl.BlockSpec(memory_space=pl.ANY),
                      pl.BlockSpec((tb, TOP_K), lambda i, pos: (i, 0)),
                      blk, blk, vec, vec],
            out_specs=blk,
            scratch_shapes=[pltpu.VMEM((TOP_K, tb, c), jnp.uint32), pltpu.SemaphoreType.DMA],
        ),
        out_shape=jax.ShapeDtypeStruct((n_tok, d), _F32),
        compiler_params=_cparams("arbitrary"), name="moe_combine",
    )(pos_flat, ys, gate8, x, shared, gate2[None, :], final_gain[None, :])


def _ffn_kernel(t_ref, wg_ref, wu_ref, wd_ref, o_ref):
    e = pl.program_id(1)
    t = t_ref[...]
    h = _silu(jnp.dot(t, wg_ref[...], preferred_element_type=_F32)) * jnp.dot(t, wu_ref[...],
                                                                               preferred_element_type=_F32)
    y = jnp.dot(h.astype(_BF16), wd_ref[...], preferred_element_type=_F32)

    @pl.when(e == 0)
    def _():
        o_ref[...] = y

    @pl.when(e > 0)
    def _():
        o_ref[...] += y


def _ffn(t, w_gate, w_up, w_down):
    n_tok, d = t.shape
    n_chunks, _, ff = w_gate.shape
    tm = _tile(n_tok, 512)
    blk = pl.BlockSpec((tm, d), lambda i, e: (i, 0))
    return pl.pallas_call(
        _ffn_kernel, grid=(n_tok // tm, n_chunks),
        in_specs=[blk,
                  pl.BlockSpec((None, d, ff), lambda i, e: (e, 0, 0)),
                  pl.BlockSpec((None, d, ff), lambda i, e: (e, 0, 0)),
                  pl.BlockSpec((None, ff, d), lambda i, e: (e, 0, 0))],
        out_specs=blk,
        out_shape=jax.ShapeDtypeStruct((n_tok, d), _F32),
        compiler_params=_cparams("parallel", "arbitrary"), name="shared_ffn",
    )(t, w_gate, w_up, w_down)


def _moe_residual(x, gain, shift, scale, gate2, w_router, router_bias, w_gate, w_up, w_down, layer, ws_gate, ws_up,
                  ws_down, final_gain):
    d = x.shape[1]
    t, tp = _norm_mod_pack(x, gain, shift, scale)
    idx8, gate8, rank8, counts = _router(t, w_router, router_bias)
    pos_flat, tile_expert, n_active, last_tile_row, n_tiles = _moe_plan(idx8, rank8, counts, _MOE_TILE_ROWS)
    xs = _dispatch(tp, pos_flat, last_tile_row, n_active, n_tiles * _MOE_TILE_ROWS, _MOE_TILE_ROWS)
    ys = _grouped_ffn(xs, tile_expert, n_active, w_gate, w_up, w_down, layer, _MOE_TILE_ROWS)
    sff = ws_gate.shape[1]
    ff = w_gate.shape[3]
    n_sh = sff // ff
    to_chunks = lambda w: w.reshape(d, n_sh, ff).transpose(1, 0, 2).astype(_BF16)
    shared = _ffn(t, to_chunks(ws_gate), to_chunks(ws_up), ws_down.reshape(n_sh, ff, d).astype(_BF16))
    return _combine(ys, pos_flat, gate8.T, x, shared, gate2, final_gain)


def _attention_layer(x, h_c, mods, mods_c, g_mix, w_in, w_out, rpb, q_gain, k_gain):
    shift1, scale1, gate1 = mods[:3]
    cshift1, cscale1 = mods_c[:2]
    n_tok = x.shape[0]
    na_w = NA_HEADS * HEAD_DIM
    gq_w = GQA_Q_HEADS * HEAD_DIM
    gkv_w = GQA_KV_HEADS * HEAD_DIM
    q_cols = na_w + gq_w
    o_nk, o_gk, o_gv, o_end = q_cols, q_cols + 2 * na_w, q_cols + 2 * na_w + gkv_w, w_in.shape[1]
    tn = int(np.gcd.reduce([na_w, gq_w, gkv_w, 512]))
    tiles = lambda lo, hi: list(range(lo // tn, hi // tn))
    w_bf = w_in.astype(_BF16)

    a_x = _norm_mod(x, g_mix, shift1, scale1, _BF16)
    a_c = _norm_mod(h_c, g_mix, cshift1, cscale1, _BF16)

    kv_plain = tiles(o_nk, o_gk) + tiles(o_gv, o_end)
    p_plain = _matmul_cols(a_x, w_bf, tiles(0, na_w) + kv_plain, tn, _BF16)
    p_normed = _matmul_cols(a_x, w_bf, tiles(na_w, q_cols) + tiles(o_gk, o_gv), tn, _F32)
    c_plain = _matmul_cols(a_c, w_bf, kv_plain, tn, _BF16)
    c_gk = _matmul_cols(a_c, w_bf, tiles(o_gk, o_gv), tn, _F32)

    cos, sin = _rope_tables(n_tok)
    gains = jnp.concatenate([jnp.tile(q_gain, GQA_Q_HEADS), jnp.tile(k_gain, GQA_KV_HEADS)])
    post = jnp.concatenate([jnp.full((gq_w,), HEAD_DIM ** -0.5 * np.log2(np.e), _F32), jnp.ones((gkv_w,), _F32)])
    qk = _qk_norm_rope(p_normed, gains, post, cos, sin)
    ck = _qk_norm_rope(c_gk, gains[gq_w:], post[gq_w:], None, None)

    o_a = _na_attention(p_plain, c_plain, rpb, NA_HEADS, 0, NA_HEADS, 2 * NA_HEADS, 0, NA_HEADS)
    k_all = jnp.concatenate([qk[:, gq_w:], ck], axis=0)
    v_all = jnp.concatenate([p_plain[:, 3 * na_w:], c_plain[:, 2 * na_w:]], axis=0)
    o_b = _gqa_attention(qk, k_all, v_all, GQA_KV_HEADS, GQA_Q_HEADS)
    return _matmul2_res(o_a, o_b, w_out.astype(_BF16), x, gate1)


def _pool_layer(x, mods, g_mix, w_pool, pool_scale):
    shift1, scale1, gate1 = mods[:3]
    diffs = _pool_diffs(x, g_mix, shift1, scale1)
    return _matmul(diffs, w_pool.astype(_BF16), grouped=True, res=x, gate=gate1, colscale=pool_scale)


def kernel(x, c, ctx, c_ctx, w_mod, b_mod, g_mix, g_ffn, attn_w_in, attn_w_out, na_rpb, q_gain, k_gain, pool_w,
           pool_scale, moe_w_router, moe_router_bias, moe_w_gate, moe_w_up, moe_w_down, moe_ws_gate, moe_ws_up,
           moe_ws_down, g_final):
    batch, seq, d = x.shape
    depth = w_mod.shape[0]
    outs = []
    for bi in range(batch):
        xb = x[bi]
        h_c = ctx[bi]
        conds = jnp.stack([c[bi], c_ctx])
        for i in range(depth):
            j = i // 2
            mod_rows = _adaln(conds, w_mod, i, b_mod[i])
            mods = jnp.split(mod_rows[0], N_MOD)
            mods_c = jnp.split(mod_rows[1], N_MOD)
            if i % 2 == 0:
                xb = _attention_layer(xb, h_c, mods, mods_c, g_mix[i], attn_w_in[j], attn_w_out[j], na_rpb[j],
                                      q_gain[j], k_gain[j])
            else:
                xb = _pool_layer(xb, mods, g_mix[i], pool_w[j], pool_scale[j])
            assert not any(l % 2 == 0 for l in range(i + 1, depth)), "context-stream update not implemented"
            xb = _moe_residual(xb, g_ffn[i], mods[3], mods[4], mods[5], moe_w_router[i], moe_router_bias[i],
                               moe_w_gate, moe_w_up, moe_w_down, i, moe_ws_gate[i], moe_ws_up[i], moe_ws_down[i],
                               g_final if i == depth - 1 else None)
        outs.append(xb)
    return jnp.stack(outs)
```

```python
import functools

import numpy as np
import jax
import jax.numpy as jnp
from jax import lax
from jax.experimental import pallas as pl
from jax.experimental.pallas import tpu as pltpu

GRID_W = 64
HEAD_DIM = 128
NA_HEADS = 16
GQA_Q_HEADS = 16
GQA_KV_HEADS = 4
NA_WIN_H = 8
NA_WIN_W = 16
ROPE_THETA = 10000.0
POOL_WINDOWS = (2, 4, 8, 16)
N_EXPERT_GROUPS = 8
TOPK_GROUPS = 4
TOP_K = 8
ROUTED_SCALE = 2.5
N_MOD = 6
RMS_EPS = 1e-6

_F32 = jnp.float32
_BF16 = jnp.bfloat16
_NEG = -1e30
_VMEM_LIMIT_BYTES = 56 * 1024 * 1024


def _cparams(*sem):
    return pltpu.CompilerParams(dimension_semantics=sem, vmem_limit_bytes=_VMEM_LIMIT_BYTES)


def _tile(n, target, mult=8):
    t = min(n, target) // mult * mult
    while t > mult and n % t:
        t -= mult
    assert t > 0 and n % t == 0, (n, target, mult)
    return t


def _silu(x):
    return x / (1.0 + jnp.exp(-x))


def _adaln_kernel(c_ref, w_ref, b_ref, o_ref, *, kc):
    n_rows, d, _ = c_ref.shape
    bn = w_ref.shape[1]
    accs = [jnp.zeros((1, bn), _F32) for _ in range(n_rows)]
    for k0 in range(0, d, kc):
        w = w_ref[k0:k0 + kc, :]
        for r in range(n_rows):
            cc = _silu(c_ref[r, k0:k0 + kc, :])
            accs[r] = accs[r] + jnp.sum(cc * w, axis=0, keepdims=True)
    o_ref[...] = jnp.concatenate(accs, axis=0) + b_ref[...]


def _adaln(conds, w_mod, layer, b_mod):
    n_rows, d = conds.shape
    n = w_mod.shape[2]
    bn = _tile(n, 512, 128)
    return pl.pallas_call(
        functools.partial(_adaln_kernel, kc=_tile(d, 256)),
        grid=(n // bn,),
        in_specs=[
            pl.BlockSpec((n_rows, d, 1), lambda j: (0, 0, 0)),
            pl.BlockSpec((None, d, bn), lambda j: (layer, 0, j)),
            pl.BlockSpec((1, bn), lambda j: (0, j)),
        ],
        out_specs=pl.BlockSpec((n_rows, bn), lambda j: (0, j)),
        out_shape=jax.ShapeDtypeStruct((n_rows, n), _F32),
        compiler_params=_cparams("arbitrary"),
        name="adaln",
    )(conds[:, :, None], w_mod, b_mod[None, :])


def _rms(x, gain):
    return x * lax.rsqrt(jnp.mean(x * x, axis=-1, keepdims=True) + RMS_EPS) * gain


def _norm_mod_kernel(x_ref, g_ref, sh_ref, sc_ref, o_ref):
    y = _rms(x_ref[...], g_ref[...])
    o_ref[...] = (y * (1.0 + sc_ref[...]) + sh_ref[...]).astype(o_ref.dtype)


def _norm_mod(x, gain, shift, scale, out_dtype):
    t, d = x.shape
    tm = _tile(t, 256)
    row = pl.BlockSpec((1, d), lambda i: (0, 0))
    blk = pl.BlockSpec((tm, d), lambda i: (i, 0))
    return pl.pallas_call(
        _norm_mod_kernel, grid=(t // tm,), in_specs=[blk, row, row, row], out_specs=blk,
        out_shape=jax.ShapeDtypeStruct((t, d), out_dtype),
        compiler_params=_cparams("parallel"), name="rmsnorm_modulate",
    )(x, gain[None, :], shift[None, :], scale[None, :])


def _mm_kernel(a_ref, w_ref, o_ref):
    o_ref[...] = jnp.dot(a_ref[...], w_ref[...], preferred_element_type=_F32).astype(o_ref.dtype)


def _mm_res_kernel(a_ref, w_ref, res_ref, gate_ref, cs_ref, o_ref):
    y = jnp.dot(a_ref[...], w_ref[...], preferred_element_type=_F32) * cs_ref[...]
    o_ref[...] = res_ref[...] + gate_ref[...] * y


def _matmul(a, w, out_dtype=_F32, *, grouped=False, res=None, gate=None, colscale=None, tm_target=512,
            tn_target=1024):
    m = a.shape[0]
    if grouped:
        g, kk, tn = w.shape
        n = g * tn
        a_spec = lambda tm: pl.BlockSpec((tm, kk), lambda j, i: (i, j))
        w_spec = pl.BlockSpec((None, kk, tn), lambda j, i: (j, 0, 0))
    else:
        kk, n = w.shape
        tn = _tile(n, tn_target, 128)
        a_spec = lambda tm: pl.BlockSpec((tm, kk), lambda j, i: (i, 0))
        w_spec = pl.BlockSpec((kk, tn), lambda j, i: (0, j))
    tm = _tile(m, tm_target)
    o_spec = pl.BlockSpec((tm, tn), lambda j, i: (i, j))
    row = pl.BlockSpec((1, tn), lambda j, i: (0, j))
    grid = (n // tn, m // tm)
    if res is None:
        return pl.pallas_call(
            _mm_kernel, grid=grid, in_specs=[a_spec(tm), w_spec], out_specs=o_spec,
            out_shape=jax.ShapeDtypeStruct((m, n), out_dtype),
            compiler_params=_cparams("parallel", "parallel"), name="matmul",
        )(a, w)
    if colscale is None:
        colscale = jnp.ones((n,), _F32)
    return pl.pallas_call(
        _mm_res_kernel, grid=grid, in_specs=[a_spec(tm), w_spec, o_spec, row, row], out_specs=o_spec,
        out_shape=jax.ShapeDtypeStruct((m, n), _F32),
        compiler_params=_cparams("parallel", "parallel"), name="matmul_gated_residual",
    )(a, w, res, gate[None, :], colscale[None, :])


def _mm2_res_kernel(a1_ref, a2_ref, w1_ref, w2_ref, res_ref, gate_ref, o_ref):
    y = (jnp.dot(a1_ref[...], w1_ref[...], preferred_element_type=_F32)
         + jnp.dot(a2_ref[...], w2_ref[...], preferred_element_type=_F32))
    o_ref[...] = res_ref[...] + gate_ref[...] * y


def _matmul2_res(a1, a2, w, res, gate):
    m, k1 = a1.shape
    assert a2.shape == (m, k1) and w.shape[0] == 2 * k1
    n = w.shape[1]
    tm = _tile(m, 512)
    tn = _tile(n, 1024, 128)
    a_spec = pl.BlockSpec((tm, k1), lambda j, i: (i, 0))
    o_spec = pl.BlockSpec((tm, tn), lambda j, i: (i, j))
    return pl.pallas_call(
        _mm2_res_kernel, grid=(n // tn, m // tm),
        in_specs=[a_spec, a_spec, pl.BlockSpec((k1, tn), lambda j, i: (0, j)),
                  pl.BlockSpec((k1, tn), lambda j, i: (1, j)), o_spec, pl.BlockSpec((1, tn), lambda j, i: (0, j))],
        out_specs=o_spec, out_shape=jax.ShapeDtypeStruct((m, n), _F32),
        compiler_params=_cparams("parallel", "parallel"), name="matmul2_gated_residual",
    )(a1, a2, w, w, res, gate[None, :])


def _mm_cols_kernel(tab_ref, a_ref, w_ref, o_ref):
    del tab_ref
    _mm_kernel(a_ref, w_ref, o_ref)


def _matmul_cols(a, w, col_tiles, tn, out_dtype):
    m, kk = a.shape
    tm = _tile(m, 512)
    n = len(col_tiles) * tn
    return pl.pallas_call(
        _mm_cols_kernel,
        grid_spec=pltpu.PrefetchScalarGridSpec(
            num_scalar_prefetch=1, grid=(len(col_tiles), m // tm),
            in_specs=[pl.BlockSpec((tm, kk), lambda j, i, tab: (i, 0)),
                      pl.BlockSpec((kk, tn), lambda j, i, tab: (0, tab[j]))],
            out_specs=pl.BlockSpec((tm, tn), lambda j, i, tab: (i, j)),
        ),
        out_shape=jax.ShapeDtypeStruct((m, n), out_dtype),
        compiler_params=_cparams("parallel", "parallel"), name="matmul_cols",
    )(jnp.asarray(col_tiles, jnp.int32), a, w)


def _qk_norm_rope_kernel(x_ref, gain_ref, post_ref, cos_ref, sin_ref, o_ref, *, rope):
    n_heads = x_ref.shape[1] // HEAD_DIM
    if rope:
        cos = cos_ref[...]
        sin = sin_ref[...]
        lane = lax.broadcasted_iota(jnp.int32, cos.shape, 1)
        first_half = (lane % (HEAD_DIM // 2)) < (HEAD_DIM // 4)
    for h in range(n_heads):
        sl = slice(h * HEAD_DIM, (h + 1) * HEAD_DIM)
        y = _rms(x_ref[:, sl], gain_ref[:, sl])
        if rope:
            partner = jnp.where(first_half, pltpu.roll(y, HEAD_DIM - HEAD_DIM // 4, 1),
                                pltpu.roll(y, HEAD_DIM // 4, 1))
            y = y * cos + partner * sin
        o_ref[:, sl] = (y * post_ref[:, sl]).astype(o_ref.dtype)


def _qk_norm_rope(x, gains, post_scale, cos, sin_signed):
    t, w = x.shape
    tm = _tile(t, 256)
    rope = cos is not None
    if not rope:
        cos = jnp.zeros((t, HEAD_DIM), _F32)
        sin_signed = cos
    blk = pl.BlockSpec((tm, w), lambda i: (i, 0))
    tab = pl.BlockSpec((tm, HEAD_DIM), lambda i: (i, 0))
    vec = pl.BlockSpec((1, w), lambda i: (0, 0))
    return pl.pallas_call(
        functools.partial(_qk_norm_rope_kernel, rope=rope), grid=(t // tm,),
        in_specs=[blk, vec, vec, tab, tab], out_specs=blk,
        out_shape=jax.ShapeDtypeStruct((t, w), _BF16),
        compiler_params=_cparams("parallel"), name="qk_norm_rope",
    )(x, gains[None, :], post_scale[None, :], cos, sin_signed)


def _rope_tables(n):
    axis_dim = HEAD_DIM // 2
    t = jnp.arange(n)
    row = (t // GRID_W).astype(_F32)
    col = (t % GRID_W).astype(_F32)
    inv = ROPE_THETA ** (-jnp.arange(0, axis_dim, 2, dtype=_F32) / axis_dim)
    ang_r = row[:, None] * inv
    ang_c = col[:, None] * inv
    cos = jnp.concatenate([jnp.cos(ang_r)] * 2 + [jnp.cos(ang_c)] * 2, axis=-1)
    sin = jnp.concatenate([-jnp.sin(ang_r), jnp.sin(ang_r), -jnp.sin(ang_c), jnp.sin(ang_c)], axis=-1)
    return cos, sin


def _dot_nt(a, b):
    return lax.dot_general(a, b, (((1,), (1,)), ((), ())), preferred_element_type=_F32)


_GQA_Q_BLOCK = 256
_GQA_KEY_BLOCK = 8320
_MXU_WIDTH = 256


def _gqa_kernel(q_ref, k_ref, v_ref, o_ref, m_ref, acc_ref, *, tk):
    group = q_ref.shape[1] // HEAD_DIM
    m_ref[...] = jnp.full(m_ref.shape, -jnp.inf, _F32)
    acc_ref[...] = jnp.zeros(acc_ref.shape, _F32)

    def body(j, carry):
        start = pl.multiple_of(j * tk, tk)
        for g in range(group):
            s = _dot_nt(q_ref[:, g * HEAD_DIM:(g + 1) * HEAD_DIM], k_ref[pl.ds(start, tk), :])
            m_prev = m_ref[g]
            m_new = jnp.maximum(m_prev, jnp.max(s, axis=-1, keepdims=True))
            p = jnp.exp2(s - m_new)
            acc_ref[g] = jnp.exp2(m_prev - m_new) * acc_ref[g] + jnp.dot(
                p.astype(_BF16), v_ref[pl.ds(start, tk), :], preferred_element_type=_F32)
            m_ref[g] = m_new
        return carry

    lax.fori_loop(0, k_ref.shape[0] // tk, body, 0)
    for g in range(group):
        acc = acc_ref[g]
        o_ref[:, g * HEAD_DIM:(g + 1) * HEAD_DIM] = (acc[:, :HEAD_DIM] / acc[:, HEAD_DIM:HEAD_DIM + 1]).astype(
            o_ref.dtype)


def _gqa_attention(q, k, v, n_kv_heads, n_q_heads):
    l = q.shape[0]
    qw = n_q_heads * HEAD_DIM
    s = k.shape[0]
    gw = qw // n_kv_heads
    tq = _tile(l, _GQA_Q_BLOCK)
    tk = _tile(s, _GQA_KEY_BLOCK, 128)
    group = gw // HEAD_DIM
    ones_col = jnp.zeros((s, n_kv_heads, HEAD_DIM), v.dtype).at[:, :, 0].set(1)
    v_aug = jnp.concatenate([v.reshape(s, n_kv_heads, HEAD_DIM), ones_col], axis=-1).reshape(s, -1)
    return pl.pallas_call(
        functools.partial(_gqa_kernel, tk=tk),
        grid=(n_kv_heads, l // tq),
        in_specs=[
            pl.BlockSpec((tq, gw), lambda h, i: (i, h)),
            pl.BlockSpec((s, HEAD_DIM), lambda h, i: (0, h), pipeline_mode=pl.Buffered(1)),
            pl.BlockSpec((s, 2 * HEAD_DIM), lambda h, i: (0, h), pipeline_mode=pl.Buffered(1)),
        ],
        out_specs=pl.BlockSpec((tq, gw), lambda h, i: (i, h)),
        out_shape=jax.ShapeDtypeStruct((l, qw), _BF16),
        scratch_shapes=[pltpu.VMEM((group, tq, 1), _F32), pltpu.VMEM((group, tq, 2 * HEAD_DIM), _F32)],
        compiler_params=_cparams("arbitrary", "arbitrary"), name="gqa_attention",
    )(q, k, v_aug)


_NA_QROWS = 8
_NA_KROWS = 16


def _na_patterns():
    first = dict(delta=0, a=[max(i - NA_WIN_H // 2, 0) for i in range(_NA_QROWS)])
    inner = dict(delta=-(NA_WIN_H // 2), a=list(range(_NA_QROWS)))
    last = dict(delta=-(_NA_KROWS - _NA_QROWS),
                a=[min(i + NA_WIN_H // 2, _NA_KROWS - NA_WIN_H) for i in range(_NA_QROWS)])
    return first, inner, last


def _na_kernel(q_ref, k_ref, v_ref, kc_ref, vc_ref, tb_ref, o_ref, bias_even, bias_odd, *, n_blocks, scale):
    step = pl.program_id(1)
    n_steps = n_blocks // 2
    w = GRID_W
    tq = _NA_QROWS * w
    lane_lo = lax.broadcasted_iota(jnp.int32, (w, 2 * w), 1) < w

    def build_bias(bias_ref, pat):
        for i in range(_NA_QROWS):
            for jj in range(_NA_KROWS // 2):
                j0 = 2 * jj
                in0 = 0 <= j0 - pat["a"][i] < NA_WIN_H
                in1 = 0 <= j0 + 1 - pat["a"][i] < NA_WIN_H
                rel0 = pat["delta"] + j0 - i + NA_WIN_H - 1
                if not (in0 or in1):
                    tile = jnp.full((w, 2 * w), _NEG, _F32)
                else:
                    tile = tb_ref[rel0 + 1]
                    if not in1:
                        tile = jnp.where(lane_lo, tile, _NEG)
                    elif not in0:
                        tile = jnp.where(lane_lo, _NEG, tile)
                bias_ref[i * w:(i + 1) * w, j0 * w:(j0 + 2) * w] = tile

    first, inner, last = _na_patterns()
    pl.when(step == 0)(functools.partial(build_bias, bias_even, first))
    pl.when(step == 0)(functools.partial(build_bias, bias_odd, inner))
    pl.when(step == 1)(functools.partial(build_bias, bias_even, inner))
    pl.when(step == n_steps - 1)(functools.partial(build_bias, bias_odd, last))

    kc = kc_ref[...]
    vc = vc_ref[...]
    for half, bias_ref in enumerate((bias_even, bias_odd)):
        b = 2 * step + half
        slab_row = jnp.where(b == 0, 0, jnp.where(b == n_blocks - 1, (n_blocks - 2) * _NA_QROWS,
                                                  b * _NA_QROWS - NA_WIN_H // 2))
        start = pl.multiple_of(slab_row * w, w)
        q = q_ref[half * tq:(half + 1) * tq, :]
        s_loc = _dot_nt(q, k_ref[pl.ds(start, _NA_KROWS * w), :]) * scale + bias_ref[...]
        s_ctx = _dot_nt(q, kc) * scale
        m = jnp.maximum(jnp.max(s_loc, axis=-1, keepdims=True), jnp.max(s_ctx, axis=-1, keepdims=True))
        p_loc = jnp.exp(s_loc - m)
        p_ctx = jnp.exp(s_ctx - m)
        denom = jnp.sum(p_loc, axis=-1, keepdims=True) + jnp.sum(p_ctx, axis=-1, keepdims=True)
        o = (jnp.dot(p_loc.astype(_BF16), v_ref[pl.ds(start, _NA_KROWS * w), :], preferred_element_type=_F32)
             + jnp.dot(p_ctx.astype(_BF16), vc, preferred_element_type=_F32))
        o_ref[half * tq:(half + 1) * tq, :] = (o / denom).astype(o_ref.dtype)


def _na_bias_table(rpb):
    qc = np.arange(GRID_W)
    win_start = np.clip(qc - NA_WIN_W // 2, 0, GRID_W - NA_WIN_W)
    rel = qc[None, :] - qc[:, None]
    in_win = (qc[None, :] >= win_start[:, None]) & (qc[None, :] < win_start[:, None] + NA_WIN_W)
    rel_idx = np.clip(rel + NA_WIN_W - 1, 0, 2 * NA_WIN_W - 2)
    toe = jnp.where(in_win[None, None], rpb[:, :, rel_idx], _NEG).astype(_F32)
    neg = jnp.full_like(toe[:, :1], _NEG)
    lo = jnp.concatenate([neg, toe], axis=1)
    hi = jnp.concatenate([toe, neg], axis=1)
    return jnp.concatenate([lo, hi], axis=-1)


def _na_attention(px, pc, rpb, n_heads, q_head0, k_head0, v_head0, ck_head0, cv_head0):
    l = px.shape[0]
    lc = pc.shape[0]
    rows = l // GRID_W
    n_blocks = rows // _NA_QROWS
    assert rows % (2 * _NA_QROWS) == 0 and n_blocks >= 4
    tq = 2 * _NA_QROWS * GRID_W
    tb = _na_bias_table(rpb)
    head_blk = lambda n, h0: pl.BlockSpec((n, HEAD_DIM), lambda h, b: (0, h0 + h))
    bias_scratch = pltpu.VMEM((_NA_QROWS * GRID_W, _NA_KROWS * GRID_W), _F32)
    return pl.pallas_call(
        functools.partial(_na_kernel, n_blocks=n_blocks, scale=HEAD_DIM ** -0.5),
        grid=(n_heads, n_blocks // 2),
        in_specs=[
            pl.BlockSpec((tq, HEAD_DIM), lambda h, b: (b, q_head0 + h)),
            head_blk(l, k_head0), head_blk(l, v_head0), head_blk(lc, ck_head0), head_blk(lc, cv_head0),
            pl.BlockSpec((None, 2 * NA_WIN_H, GRID_W, 2 * GRID_W), lambda h, b: (h, 0, 0, 0)),
        ],
        out_specs=pl.BlockSpec((tq, HEAD_DIM), lambda h, b: (b, h)),
        out_shape=jax.ShapeDtypeStruct((l, n_heads * HEAD_DIM), _BF16),
        scratch_shapes=[bias_scratch, bias_scratch],
        compiler_params=_cparams("arbitrary", "arbitrary"), name="na_attention",
    )(px, px, px, pc, pc, tb)


_POOL_HALO = 16


def _pool_kernel(prev_ref, cur_ref, next_ref, g_ref, sh_ref, sc_ref, o_ref, *, n_tokens):
    i = pl.program_id(0)
    tm, d = cur_ref.shape
    dg = d // len(POOL_WINDOWS)
    x = jnp.concatenate([prev_ref[...], cur_ref[...], next_ref[...]], axis=0)
    a = _rms(x, g_ref[...]) * (1.0 + sc_ref[...]) + sh_ref[...]
    pos = i * tm - _POOL_HALO + lax.broadcasted_iota(jnp.int32, (tm + 2 * _POOL_HALO, 1), 0)
    a = jnp.where((pos >= 0) & (pos < n_tokens), a, 0.0)
    t = i * tm + lax.broadcasted_iota(jnp.int32, (tm, 1), 0)
    for g, w in enumerate(POOL_WINDOWS):
        ag = a[:, g * dg:(g + 1) * dg]
        tot = jnp.zeros((tm, dg), _F32)
        for off in range(-(w // 2), w - w // 2):
            tot = tot + ag[_POOL_HALO + off:_POOL_HALO + off + tm, :]
        cnt = jnp.minimum(t + (w - w // 2), n_tokens) - jnp.maximum(t - w // 2, 0)
        diff = tot / cnt.astype(_F32) - ag[_POOL_HALO:_POOL_HALO + tm, :]
        o_ref[:, g * dg:(g + 1) * dg] = diff.astype(o_ref.dtype)


def _pool_diffs(x, gain, shift, scale):
    t, d = x.shape
    tm = _tile(t, 256, _POOL_HALO)
    hb = tm // _POOL_HALO
    n_halo_blocks = t // _POOL_HALO
    row = pl.BlockSpec((1, d), lambda i: (0, 0))
    return pl.pallas_call(
        functools.partial(_pool_kernel, n_tokens=t), grid=(t // tm,),
        in_specs=[
            pl.BlockSpec((_POOL_HALO, d), lambda i: (jnp.maximum(i * hb - 1, 0), 0)),
            pl.BlockSpec((tm, d), lambda i: (i, 0)),
            pl.BlockSpec((_POOL_HALO, d), lambda i: (jnp.minimum((i + 1) * hb, n_halo_blocks - 1), 0)),
            row, row, row,
        ],
        out_specs=pl.BlockSpec((tm, d), lambda i: (i, 0)),
        out_shape=jax.ShapeDtypeStruct((t, d), _BF16),
        compiler_params=_cparams("parallel"), name="pool_diffs",
    )(x, x, x, gain[None, :], shift[None, :], scale[None, :])


def _router_kernel(t_ref, w_ref, b_ref, idx_ref, gate_ref, rank_ref, cnt_ref):
    n_exp = w_ref.shape[0]
    tm = t_ref.shape[0]
    per_group = n_exp // N_EXPERT_GROUPS

    @pl.when(pl.program_id(0) == 0)
    def _():
        cnt_ref[...] = jnp.zeros(cnt_ref.shape, _F32)

    logits = _dot_nt(w_ref[...], t_ref[...])
    scores = 1.0 / (1.0 + jnp.exp(-logits))
    biased = scores + b_ref[...]

    def first_max(vals):
        idx = lax.broadcasted_iota(jnp.int32, vals.shape, 0).astype(_F32)
        m = jnp.max(vals, axis=0, keepdims=True)
        first = jnp.min(jnp.where(vals == m, idx, float(vals.shape[0])), axis=0, keepdims=True)
        return m, idx == first

    def take_top(vals, k):
        hits = []
        for _ in range(k):
            _, hit = first_max(vals)
            hits.append(hit)
            vals = jnp.where(hit, -jnp.inf, vals)
        return hits

    def union(hits):
        sel = jnp.zeros(hits[0].shape, _F32)
        for hit in hits:
            sel = jnp.where(hit, 1.0, sel)
        return sel

    group_scores = []
    for g in range(N_EXPERT_GROUPS):
        v = biased[g * per_group:(g + 1) * per_group, :]
        m1, hit = first_max(v)
        m2 = jnp.max(jnp.where(hit, -jnp.inf, v), axis=0, keepdims=True)
        group_scores.append(m1 + m2)
    gsel = union(take_top(jnp.concatenate(group_scores, axis=0), TOPK_GROUPS))
    keep = jnp.concatenate([jnp.broadcast_to(gsel[g:g + 1, :], (per_group, tm)) for g in range(N_EXPERT_GROUPS)],
                           axis=0)
    hits = take_top(jnp.where(keep > 0.0, biased, -jnp.inf), TOP_K)
    esel = union(hits)
    wsel = esel * scores
    gates = wsel / jnp.sum(wsel, axis=0, keepdims=True) * ROUTED_SCALE

    earlier = (lax.broadcasted_iota(jnp.int32, (tm, tm), 0) < lax.broadcasted_iota(jnp.int32, (tm, tm), 1))
    rank = cnt_ref[...] + jnp.dot(esel.astype(_BF16), jnp.where(earlier, 1.0, 0.0).astype(_BF16),
                                  preferred_element_type=_F32)
    cnt_ref[...] += jnp.sum(esel, axis=1, keepdims=True)

    eidx = lax.broadcasted_iota(jnp.int32, (n_exp, tm), 0).astype(_F32)
    pick = lambda hit, v: jnp.sum(jnp.where(hit, v, 0.0), axis=0, keepdims=True)
    idx_ref[...] = jnp.concatenate([pick(h, eidx) for h in hits], axis=0).astype(jnp.int32)
    gate_ref[...] = jnp.concatenate([pick(h, gates) for h in hits], axis=0)
    rank_ref[...] = jnp.concatenate([pick(h, rank) for h in hits], axis=0).astype(jnp.int32)


def _router(t, w_router, router_bias):
    n_tok, d = t.shape
    n_exp = w_router.shape[1]
    tm = _tile(n_tok, 512, 128)
    per_tok = pl.BlockSpec((TOP_K, tm), lambda i: (0, i))
    return pl.pallas_call(
        _router_kernel, grid=(n_tok // tm,),
        in_specs=[
            pl.BlockSpec((tm, d), lambda i: (i, 0)),
            pl.BlockSpec((n_exp, d), lambda i: (0, 0)),
            pl.BlockSpec((n_exp, 1), lambda i: (0, 0)),
        ],
        out_specs=[per_tok, per_tok, per_tok, pl.BlockSpec((n_exp, 1), lambda i: (0, 0))],
        out_shape=[jax.ShapeDtypeStruct((TOP_K, n_tok), jnp.int32), jax.ShapeDtypeStruct((TOP_K, n_tok), _F32),
                   jax.ShapeDtypeStruct((TOP_K, n_tok), jnp.int32), jax.ShapeDtypeStruct((n_exp, 1), _F32)],
        compiler_params=_cparams("arbitrary"), name="moe_router",
    )(t, w_router.T.astype(_BF16), router_bias[:, None])


_MOE_TILE_ROWS = 256


def _packed_cols(d):
    return d // 2


def _pack_pairs(y):
    half = y.shape[1] // 2
    bits = lambda v: lax.bitcast_convert_type(v.astype(_BF16).astype(_F32), jnp.uint32)
    return (bits(y[:, :half]) & jnp.uint32(0xFFFF0000)) | (bits(y[:, half:]) >> jnp.uint32(16))


def _unpack_pairs(w):
    hi = lax.bitcast_convert_type(w & jnp.uint32(0xFFFF0000), _F32)
    lo = lax.bitcast_convert_type(w << jnp.uint32(16), _F32)
    return hi, lo


def _norm_mod_pack_kernel(x_ref, g_ref, sh_ref, sc_ref, o_ref, p_ref):
    y = _rms(x_ref[...], g_ref[...]) * (1.0 + sc_ref[...]) + sh_ref[...]
    o_ref[...] = y.astype(o_ref.dtype)
    p_ref[...] = _pack_pairs(y)


def _norm_mod_pack(x, gain, shift, scale):
    t, d = x.shape
    tm = _tile(t, 256)
    row = pl.BlockSpec((1, d), lambda i: (0, 0))
    blk = pl.BlockSpec((tm, d), lambda i: (i, 0))
    return pl.pallas_call(
        _norm_mod_pack_kernel, grid=(t // tm,), in_specs=[blk, row, row, row],
        out_specs=[blk, pl.BlockSpec((tm, _packed_cols(d)), lambda i: (i, 0))],
        out_shape=[jax.ShapeDtypeStruct((t, d), _BF16), jax.ShapeDtypeStruct((t, _packed_cols(d)), jnp.uint32)],
        compiler_params=_cparams("parallel"), name="rmsnorm_modulate_pack",
    )(x, gain[None, :], shift[None, :], scale[None, :])


def _moe_plan(idx8, rank8, counts, tile_rows):
    n_exp = counts.shape[0]
    n_tok = idx8.shape[1]
    cnt = counts[:, 0].astype(jnp.int32)
    padded = (cnt + tile_rows - 1) // tile_rows * tile_rows
    ends = jnp.cumsum(padded)
    starts = ends - padded
    onehot = idx8[:, :, None] == jnp.arange(n_exp, dtype=jnp.int32)
    pos8 = rank8 + jnp.sum(jnp.where(onehot, starts, 0), axis=-1)
    n_tiles = TOP_K * n_tok // tile_rows + n_exp
    tile_row0 = jnp.arange(n_tiles, dtype=jnp.int32) * tile_rows
    tile_expert = jnp.minimum(jnp.sum(tile_row0[:, None] >= ends[None, :], axis=1), n_exp - 1).astype(jnp.int32)
    n_active = (ends[-1:] // tile_rows).astype(jnp.int32)
    last_tile_row = jnp.where(padded > 0, ends - tile_rows, -1).astype(jnp.int32)
    return pos8.T.reshape(-1), tile_expert, n_active, last_tile_row, n_tiles


def _dispatch_kernel(pos_ref, last_ref, na_ref, tp_ref, xs_ref, zero_ref, sem, zsem, *, tile_rows):
    i = pl.program_id(0)
    tb = tp_ref.shape[0]
    n_exp = last_ref.shape[0]
    n_tiles = xs_ref.shape[0] // tile_rows

    @pl.when(i == 0)
    def _():
        zero_ref[...] = jnp.zeros(zero_ref.shape, zero_ref.dtype)

        def fill(row0):
            return pltpu.make_async_copy(zero_ref, xs_ref.at[pl.ds(pl.multiple_of(row0, tile_rows), tile_rows)],
                                         zsem)

        def each(e, start):
            tail = na_ref[0] + e

            def go(row0):
                cp = fill(row0)
                cp.start() if start else cp.wait()

            pl.when(last_ref[e] >= 0)(lambda: go(jnp.maximum(last_ref[e], 0)))
            pl.when(tail < n_tiles)(lambda: go(jnp.minimum(tail, n_tiles - 1) * tile_rows))

        lax.fori_loop(0, n_exp, lambda e, c: (each(e, True), c)[1], 0)
        lax.fori_loop(0, n_exp, lambda e, c: (each(e, False), c)[1], 0)

    def row(r, c):
        base = (i * tb + r) * TOP_K
        for k in range(TOP_K):
            pltpu.make_async_copy(tp_ref.at[pl.ds(r, 1)], xs_ref.at[pl.ds(pos_ref[base + k], 1)], sem).start()
        return c

    lax.fori_loop(0, tb, row, 0)
    for _ in range(TOP_K):
        pltpu.make_async_copy(tp_ref, xs_ref.at[pl.ds(0, tb)], sem).wait()


def _dispatch(tp, pos_flat, last_tile_row, n_active, n_rows, tile_rows):
    n_tok, c = tp.shape
    tb = _tile(n_tok, 256)
    return pl.pallas_call(
        functools.partial(_dispatch_kernel, tile_rows=tile_rows),
        grid_spec=pltpu.PrefetchScalarGridSpec(
            num_scalar_prefetch=3, grid=(n_tok // tb,),
            in_specs=[pl.BlockSpec((tb, c), lambda i, pos, last, na: (i, 0))],
            out_specs=pl.BlockSpec(memory_space=pl.ANY),
            scratch_shapes=[pltpu.VMEM((tile_rows, c), jnp.uint32), pltpu.SemaphoreType.DMA,
                            pltpu.SemaphoreType.DMA],
        ),
        out_shape=jax.ShapeDtypeStruct((n_rows, c), jnp.uint32),
        compiler_params=_cparams("arbitrary"), name="moe_dispatch",
    )(pos_flat, last_tile_row, n_active, tp)


def _grouped_ffn_kernel(te_ref, na_ref, xs_ref, wg_ref, wu_ref, wd_ref, ys_ref, wg_bf, wu_bf, wd_bf):
    i = pl.program_id(0)
    active = i < na_ref[0]
    new_expert = jnp.logical_or(i == 0, te_ref[i] != te_ref[jnp.maximum(i - 1, 0)])

    @pl.when(jnp.logical_and(active, new_expert))
    def _():
        wg_bf[...] = wg_ref[...].astype(_BF16)
        wu_bf[...] = wu_ref[...].astype(_BF16)
        wd_bf[...] = wd_ref[...].astype(_BF16)

    @pl.when(active)
    def _():
        half = wg_bf.shape[0] // 2
        hi, lo = _unpack_pairs(xs_ref[...])
        hi = hi.astype(_BF16)
        lo = lo.astype(_BF16)
        dot = lambda a, b: jnp.dot(a, b, preferred_element_type=_F32)
        hg = dot(hi, wg_bf[:half, :]) + dot(lo, wg_bf[half:, :])
        hu = dot(hi, wu_bf[:half, :]) + dot(lo, wu_bf[half:, :])
        h = (_silu(hg) * hu).astype(_BF16)
        ys_ref[...] = _pack_pairs(dot(h, wd_bf[...]))

    @pl.when(jnp.logical_not(active))
    def _():
        ys_ref[...] = jnp.zeros(ys_ref.shape, ys_ref.dtype)


def _grouped_ffn(xs, tile_expert, n_active, w_gate, w_up, w_down, layer, tile_rows):
    n_rows, c = xs.shape
    _, n_exp, d, ff = w_gate.shape
    n_tiles = n_rows // tile_rows
    row_blk = pl.BlockSpec((tile_rows, c), lambda i, te, na: (jnp.minimum(i, jnp.maximum(na[0] - 1, 0)), 0))
    return pl.pallas_call(
        _grouped_ffn_kernel,
        grid_spec=pltpu.PrefetchScalarGridSpec(
            num_scalar_prefetch=2, grid=(n_tiles,),
            in_specs=[row_blk,
                      pl.BlockSpec((None, None, d, ff), lambda i, te, na: (layer, te[i], 0, 0)),
                      pl.BlockSpec((None, None, d, ff), lambda i, te, na: (layer, te[i], 0, 0)),
                      pl.BlockSpec((None, None, ff, d), lambda i, te, na: (layer, te[i], 0, 0))],
            out_specs=pl.BlockSpec((tile_rows, c), lambda i, te, na: (i, 0)),
            scratch_shapes=[pltpu.VMEM((d, ff), _BF16), pltpu.VMEM((d, ff), _BF16), pltpu.VMEM((ff, d), _BF16)],
        ),
        out_shape=jax.ShapeDtypeStruct((n_rows, c), jnp.uint32),
        compiler_params=_cparams("arbitrary"), name="moe_grouped_ffn",
    )(tile_expert, n_active, xs, w_gate, w_up, w_down)


def _combine_kernel(pos_ref, ys_ref, g8_ref, x_ref, sh_ref, g2_ref, fg_ref, o_ref, buf, sem, *, final_norm):
    i = pl.program_id(0)
    tb, d = x_ref.shape
    half = d // 2

    def row(r, c):
        base = (i * tb + r) * TOP_K
        for k in range(TOP_K):
            pltpu.make_async_copy(ys_ref.at[pl.ds(pos_ref[base + k], 1)], buf.at[k, pl.ds(r, 1)], sem).start()
        return c

    lax.fori_loop(0, tb, row, 0)
    for k in range(TOP_K):
        pltpu.make_async_copy(ys_ref.at[pl.ds(0, tb)], buf.at[k], sem).wait()

    g8 = g8_ref[...]
    acc_hi = jnp.zeros((tb, half), _F32)
    acc_lo = jnp.zeros((tb, half), _F32)
    for k in range(TOP_K):
        hi, lo = _unpack_pairs(buf[k])
        acc_hi = acc_hi + g8[:, k:k + 1] * hi
        acc_lo = acc_lo + g8[:, k:k + 1] * lo
    out_hi = x_ref[:, :half] + g2_ref[:, :half] * (sh_ref[:, :half] + acc_hi)
    out_lo = x_ref[:, half:] + g2_ref[:, half:] * (sh_ref[:, half:] + acc_lo)
    if final_norm:
        ssq = jnp.sum(out_hi * out_hi, axis=-1, keepdims=True) + jnp.sum(out_lo * out_lo, axis=-1, keepdims=True)
        inv = lax.rsqrt(ssq / d + RMS_EPS)
        out_hi = out_hi * inv * fg_ref[:, :half]
        out_lo = out_lo * inv * fg_ref[:, half:]
    o_ref[:, :half] = out_hi
    o_ref[:, half:] = out_lo


def _combine(ys, pos_flat, gate8, x, shared, gate2, final_gain):
    n_tok, d = x.shape
    c = ys.shape[1]
    tb = _tile(n_tok, 128)
    blk = pl.BlockSpec((tb, d), lambda i, pos: (i, 0))
    vec = pl.BlockSpec((1, d), lambda i, pos: (0, 0))
    final_norm = final_gain is not None
    if not final_norm:
        final_gain = jnp.ones((d,), _F32)
    return pl.pallas_call(
        functools.partial(_combine_kernel, final_norm=final_norm),
        grid_spec=pltpu.PrefetchScalarGridSpec(
            num_scalar_prefetch=1, grid=(n_tok // tb,),
            in_specs=[pl.BlockSpec(memory_space=pl.ANY),
                      pl.BlockSpec((tb, TOP_K), lambda i, pos: (i, 0)),
                      blk, blk, vec, vec],
            out_specs=blk,
            scratch_shapes=[pltpu.VMEM((TOP_K, tb, c), jnp.uint32), pltpu.SemaphoreType.DMA],
        ),
        out_shape=jax.ShapeDtypeStruct((n_tok, d), _F32),
        compiler_params=_cparams("arbitrary"), name="moe_combine",
    )(pos_flat, ys, gate8, x, shared, gate2[None, :], final_gain[None, :])


def _ffn_kernel(t_ref, wg_ref, wu_ref, wd_ref, o_ref):
    e = pl.program_id(1)
    t = t_ref[...]
    h = _silu(jnp.dot(t, wg_ref[...], preferred_element_type=_F32)) * jnp.dot(t, wu_ref[...],
                                                                               preferred_element_type=_F32)
    y = jnp.dot(h.astype(_BF16), wd_ref[...], preferred_element_type=_F32)

    @pl.when(e == 0)
    def _():
        o_ref[...] = y

    @pl.when(e > 0)
    def _():
        o_ref[...] += y


def _ffn(t, w_gate, w_up, w_down):
    n_tok, d = t.shape
    n_chunks, _, ff = w_gate.shape
    tm = _tile(n_tok, 512)
    blk = pl.BlockSpec((tm, d), lambda i, e: (i, 0))
    return pl.pallas_call(
        _ffn_kernel, grid=(n_tok // tm, n_chunks),
        in_specs=[blk,
                  pl.BlockSpec((None, d, ff), lambda i, e: (e, 0, 0)),
                  pl.BlockSpec((None, d, ff), lambda i, e: (e, 0, 0)),
                  pl.BlockSpec((None, ff, d), lambda i, e: (e, 0, 0))],
        out_specs=blk,
        out_shape=jax.ShapeDtypeStruct((n_tok, d), _F32),
        compiler_params=_cparams("parallel", "arbitrary"), name="shared_ffn",
    )(t, w_gate, w_up, w_down)


def _moe_residual(x, gain, shift, scale, gate2, w_router, router_bias, w_gate, w_up, w_down, layer, ws_gate, ws_up,
                  ws_down, final_gain):
    d = x.shape[1]
    t, tp = _norm_mod_pack(x, gain, shift, scale)
    idx8, gate8, rank8, counts = _router(t, w_router, router_bias)
    pos_flat, tile_expert, n_active, last_tile_row, n_tiles = _moe_plan(idx8, rank8, counts, _MOE_TILE_ROWS)
    xs = _dispatch(tp, pos_flat, last_tile_row, n_active, n_tiles * _MOE_TILE_ROWS, _MOE_TILE_ROWS)
    ys = _grouped_ffn(xs, tile_expert, n_active, w_gate, w_up, w_down, layer, _MOE_TILE_ROWS)
    sff = ws_gate.shape[1]
    ff = w_gate.shape[3]
    n_sh = sff // ff
    to_chunks = lambda w: w.reshape(d, n_sh, ff).transpose(1, 0, 2).astype(_BF16)
    shared = _ffn(t, to_chunks(ws_gate), to_chunks(ws_up), ws_down.reshape(n_sh, ff, d).astype(_BF16))
    return _combine(ys, pos_flat, gate8.T, x, shared, gate2, final_gain)


def _attention_layer(x, h_c, mods, mods_c, g_mix, w_in, w_out, rpb, q_gain, k_gain):
    shift1, scale1, gate1 = mods[:3]
    cshift1, cscale1 = mods_c[:2]
    n_tok = x.shape[0]
    na_w = NA_HEADS * HEAD_DIM
    gq_w = GQA_Q_HEADS * HEAD_DIM
    gkv_w = GQA_KV_HEADS * HEAD_DIM
    q_cols = na_w + gq_w
    o_nk, o_gk, o_gv, o_end = q_cols, q_cols + 2 * na_w, q_cols + 2 * na_w + gkv_w, w_in.shape[1]
    tn = int(np.gcd.reduce([na_w, gq_w, gkv_w, 512]))
    tiles = lambda lo, hi: list(range(lo // tn, hi // tn))
    w_bf = w_in.astype(_BF16)

    a_x = _norm_mod(x, g_mix, shift1, scale1, _BF16)
    a_c = _norm_mod(h_c, g_mix, cshift1, cscale1, _BF16)

    kv_plain = tiles(o_nk, o_gk) + tiles(o_gv, o_end)
    p_plain = _matmul_cols(a_x, w_bf, tiles(0, na_w) + kv_plain, tn, _BF16)
    p_normed = _matmul_cols(a_x, w_bf, tiles(na_w, q_cols) + tiles(o_gk, o_gv), tn, _F32)
    c_plain = _matmul_cols(a_c, w_bf, kv_plain, tn, _BF16)
    c_gk = _matmul_cols(a_c, w_bf, tiles(o_gk, o_gv), tn, _F32)

    cos, sin = _rope_tables(n_tok)
    gains = jnp.concatenate([jnp.tile(q_gain, GQA_Q_HEADS), jnp.tile(k_gain, GQA_KV_HEADS)])
    post = jnp.concatenate([jnp.full((gq_w,), HEAD_DIM ** -0.5 * np.log2(np.e), _F32), jnp.ones((gkv_w,), _F32)])
    qk = _qk_norm_rope(p_normed, gains, post, cos, sin)
    ck = _qk_norm_rope(c_gk, gains[gq_w:], post[gq_w:], None, None)

    o_a = _na_attention(p_plain, c_plain, rpb, NA_HEADS, 0, NA_HEADS, 2 * NA_HEADS, 0, NA_HEADS)
    k_all = jnp.concatenate([qk[:, gq_w:], ck], axis=0)
    v_all = jnp.concatenate([p_plain[:, 3 * na_w:], c_plain[:, 2 * na_w:]], axis=0)
    o_b = _gqa_attention(qk, k_all, v_all, GQA_KV_HEADS, GQA_Q_HEADS)
    return _matmul2_res(o_a, o_b, w_out.astype(_BF16), x, gate1)


def _pool_layer(x, mods, g_mix, w_pool, pool_scale):
    shift1, scale1, gate1 = mods[:3]
    diffs = _pool_diffs(x, g_mix, shift1, scale1)
    return _matmul(diffs, w_pool.astype(_BF16), grouped=True, res=x, gate=gate1, colscale=pool_scale)


def kernel(x, c, ctx, c_ctx, w_mod, b_mod, g_mix, g_ffn, attn_w_in, attn_w_out, na_rpb, q_gain, k_gain, pool_w,
           pool_scale, moe_w_router, moe_router_bias, moe_w_gate, moe_w_up, moe_w_down, moe_ws_gate, moe_ws_up,
           moe_ws_down, g_final):
    batch, seq, d = x.shape
    depth = w_mod.shape[0]
    outs = []
    for bi in range(batch):
        xb = x[bi]
        h_c = ctx[bi]
        conds = jnp.stack([c[bi], c_ctx])
        for i in range(depth):
            j = i // 2
            mod_rows = _adaln(conds, w_mod, i, b_mod[i])
            mods = jnp.split(mod_rows[0], N_MOD)
            mods_c = jnp.split(mod_rows[1], N_MOD)
            if i % 2 == 0:
                xb = _attention_layer(xb, h_c, mods, mods_c, g_mix[i], attn_w_in[j], attn_w_out[j], na_rpb[j],
                                      q_gain[j], k_gain[j])
            else:
                xb = _pool_layer(xb, mods, g_mix[i], pool_w[j], pool_scale[j])
            assert not any(l % 2 == 0 for l in range(i + 1, depth)), "context-stream update not implemented"
            xb = _moe_residual(xb, g_ffn[i], mods[3], mods[4], mods[5], moe_w_router[i], moe_router_bias[i],
                               moe_w_gate, moe_w_up, moe_w_down, i, moe_ws_gate[i], moe_ws_up[i], moe_ws_down[i],
                               g_final if i == depth - 1 else None)
        outs.append(xb)
    return jnp.stack(outs)
```

```python
import functools

import numpy as np
import jax
import jax.numpy as jnp
from jax import lax
from jax.experimental import pallas as pl
from jax.experimental.pallas import tpu as pltpu

GRID_W = 64
HEAD_DIM = 128
NA_HEADS = 16
GQA_Q_HEADS = 16
GQA_KV_HEADS = 4
NA_WIN_H = 8
NA_WIN_W = 16
ROPE_THETA = 10000.0
POOL_WINDOWS = (2, 4, 8, 16)
N_EXPERT_GROUPS = 8
TOPK_GROUPS = 4
TOP_K = 8
ROUTED_SCALE = 2.5
N_MOD = 6
RMS_EPS = 1e-6

_F32 = jnp.float32
_BF16 = jnp.bfloat16
_NEG = -1e30
_VMEM_LIMIT_BYTES = 56 * 1024 * 1024


def _cparams(*sem):
    return pltpu.CompilerParams(dimension_semantics=sem, vmem_limit_bytes=_VMEM_LIMIT_BYTES)


def _tile(n, target, mult=8):
    t = min(n, target) // mult * mult
    while t > mult and n % t:
        t -= mult
    assert t > 0 and n % t == 0, (n, target, mult)
    return t


def _silu(x):
    return x / (1.0 + jnp.exp(-x))


def _adaln_kernel(c_ref, w_ref, b_ref, o_ref, *, kc):
    n_rows, d, _ = c_ref.shape
    bn = w_ref.shape[1]
    accs = [jnp.zeros((1, bn), _F32) for _ in range(n_rows)]
    for k0 in range(0, d, kc):
        w = w_ref[k0:k0 + kc, :]
        for r in range(n_rows):
            cc = _silu(c_ref[r, k0:k0 + kc, :])
            accs[r] = accs[r] + jnp.sum(cc * w, axis=0, keepdims=True)
    o_ref[...] = jnp.concatenate(accs, axis=0) + b_ref[...]


def _adaln(conds, w_mod, layer, b_mod):
    n_rows, d = conds.shape
    n = w_mod.shape[2]
    bn = _tile(n, 512, 128)
    return pl.pallas_call(
        functools.partial(_adaln_kernel, kc=_tile(d, 256)),
        grid=(n // bn,),
        in_specs=[
            pl.BlockSpec((n_rows, d, 1), lambda j: (0, 0, 0)),
            pl.BlockSpec((None, d, bn), lambda j: (layer, 0, j)),
            pl.BlockSpec((1, bn), lambda j: (0, j)),
        ],
        out_specs=pl.BlockSpec((n_rows, bn), lambda j: (0, j)),
        out_shape=jax.ShapeDtypeStruct((n_rows, n), _F32),
        compiler_params=_cparams("arbitrary"),
        name="adaln",
    )(conds[:, :, None], w_mod, b_mod[None, :])


def _rms(x, gain):
    return x * lax.rsqrt(jnp.mean(x * x, axis=-1, keepdims=True) + RMS_EPS) * gain


def _norm_mod_kernel(x_ref, g_ref, sh_ref, sc_ref, o_ref):
    y = _rms(x_ref[...], g_ref[...])
    o_ref[...] = (y * (1.0 + sc_ref[...]) + sh_ref[...]).astype(o_ref.dtype)


def _norm_mod(x, gain, shift, scale, out_dtype):
    t, d = x.shape
    tm = _tile(t, 256)
    row = pl.BlockSpec((1, d), lambda i: (0, 0))
    blk = pl.BlockSpec((tm, d), lambda i: (i, 0))
    return pl.pallas_call(
        _norm_mod_kernel, grid=(t // tm,), in_specs=[blk, row, row, row], out_specs=blk,
        out_shape=jax.ShapeDtypeStruct((t, d), out_dtype),
        compiler_params=_cparams("parallel"), name="rmsnorm_modulate",
    )(x, gain[None, :], shift[None, :], scale[None, :])


def _mm_kernel(a_ref, w_ref, o_ref):
    o_ref[...] = jnp.dot(a_ref[...], w_ref[...], preferred_element_type=_F32).astype(o_ref.dtype)


def _mm_res_kernel(a_ref, w_ref, res_ref, gate_ref, cs_ref, o_ref):
    y = jnp.dot(a_ref[...], w_ref[...], preferred_element_type=_F32) * cs_ref[...]
    o_ref[...] = res_ref[...] + gate_ref[...] * y


def _matmul(a, w, out_dtype=_F32, *, grouped=False, res=None, gate=None, colscale=None, tm_target=512,
            tn_target=1024):
    m = a.shape[0]
    if grouped:
        g, kk, tn = w.shape
        n = g * tn
        a_spec = lambda tm: pl.BlockSpec((tm, kk), lambda j, i: (i, j))
        w_spec = pl.BlockSpec((None, kk, tn), lambda j, i: (j, 0, 0))
    else:
        kk, n = w.shape
        tn = _tile(n, tn_target, 128)
        a_spec = lambda tm: pl.BlockSpec((tm, kk), lambda j, i: (i, 0))
        w_spec = pl.BlockSpec((kk, tn), lambda j, i: (0, j))
    tm = _tile(m, tm_target)
    o_spec = pl.BlockSpec((tm, tn), lambda j, i: (i, j))
    row = pl.BlockSpec((1, tn), lambda j, i: (0, j))
    grid = (n // tn, m // tm)
    if res is None:
        return pl.pallas_call(
            _mm_kernel, grid=grid, in_specs=[a_spec(tm), w_spec], out_specs=o_spec,
            out_shape=jax.ShapeDtypeStruct((m, n), out_dtype),
            compiler_params=_cparams("parallel", "parallel"), name="matmul",
        )(a, w)
    if colscale is None:
        colscale = jnp.ones((n,), _F32)
    return pl.pallas_call(
        _mm_res_kernel, grid=grid, in_specs=[a_spec(tm), w_spec, o_spec, row, row], out_specs=o_spec,
        out_shape=jax.ShapeDtypeStruct((m, n), _F32),
        compiler_params=_cparams("parallel", "parallel"), name="matmul_gated_residual",
    )(a, w, res, gate[None, :], colscale[None, :])


def _mm2_res_kernel(a1_ref, a2_ref, w1_ref, w2_ref, res_ref, gate_ref, o_ref):
    y = (jnp.dot(a1_ref[...], w1_ref[...], preferred_element_type=_F32)
         + jnp.dot(a2_ref[...], w2_ref[...], preferred_element_type=_F32))
    o_ref[...] = res_ref[...] + gate_ref[...] * y


def _matmul2_res(a1, a2, w, res, gate):
    m, k1 = a1.shape
    assert a2.shape == (m, k1) and w.shape[0] == 2 * k1
    n = w.shape[1]
    tm = _tile(m, 512)
    tn = _tile(n, 1024, 128)
    a_spec = pl.BlockSpec((tm, k1), lambda j, i: (i, 0))
    o_spec = pl.BlockSpec((tm, tn), lambda j, i: (i, j))
    return pl.pallas_call(
        _mm2_res_kernel, grid=(n // tn, m // tm),
        in_specs=[a_spec, a_spec, pl.BlockSpec((k1, tn), lambda j, i: (0, j)),
                  pl.BlockSpec((k1, tn), lambda j, i: (1, j)), o_spec, pl.BlockSpec((1, tn), lambda j, i: (0, j))],
        out_specs=o_spec, out_shape=jax.ShapeDtypeStruct((m, n), _F32),
        compiler_params=_cparams("parallel", "parallel"), name="matmul2_gated_residual",
    )(a1, a2, w, w, res, gate[None, :])


def _mm_cols_kernel(tab_ref, a_ref, w_ref, o_ref):
    del tab_ref
    _mm_kernel(a_ref, w_ref, o_ref)


def _matmul_cols(a, w, col_tiles, tn, out_dtype):
    m, kk = a.shape
    tm = _tile(m, 512)
    n = len(col_tiles) * tn
    return pl.pallas_call(
        _mm_cols_kernel,
        grid_spec=pltpu.PrefetchScalarGridSpec(
            num_scalar_prefetch=1, grid=(len(col_tiles), m // tm),
            in_specs=[pl.BlockSpec((tm, kk), lambda j, i, tab: (i, 0)),
                      pl.BlockSpec((kk, tn), lambda j, i, tab: (0, tab[j]))],
            out_specs=pl.BlockSpec((tm, tn), lambda j, i, tab: (i, j)),
        ),
        out_shape=jax.ShapeDtypeStruct((m, n), out_dtype),
        compiler_params=_cparams("parallel", "parallel"), name="matmul_cols",
    )(jnp.asarray(col_tiles, jnp.int32), a, w)


def _mm_cols_norm_rope_kernel(tab_ref, a_ref, w_ref, gain_ref, post_ref, cos_ref, sin_ref, o_ref, *, rope):
    del tab_ref
    x = jnp.dot(a_ref[...], w_ref[...], preferred_element_type=_F32)
    if rope:
        cos = cos_ref[...]
        sin = sin_ref[...]
        lane = lax.broadcasted_iota(jnp.int32, cos.shape, 1)
        first_half = (lane % (HEAD_DIM // 2)) < (HEAD_DIM // 4)
    for h in range(x.shape[1] // HEAD_DIM):
        sl = slice(h * HEAD_DIM, (h + 1) * HEAD_DIM)
        y = _rms(x[:, sl], gain_ref[:, sl])
        if rope:
            partner = jnp.where(first_half, pltpu.roll(y, HEAD_DIM - HEAD_DIM // 4, 1),
                                pltpu.roll(y, HEAD_DIM // 4, 1))
            y = y * cos + partner * sin
        o_ref[:, sl] = (y * post_ref[:, sl]).astype(o_ref.dtype)


def _matmul_cols_norm_rope(a, w, col_tiles, tn, gains, post_scale, cos, sin_signed):
    m, kk = a.shape
    tm = _tile(m, 512)
    n = len(col_tiles) * tn
    rope = cos is not None
    if not rope:
        cos = jnp.zeros((m, HEAD_DIM), _F32)
        sin_signed = cos
    vec = pl.BlockSpec((1, tn), lambda j, i, tab: (0, j))
    table = pl.BlockSpec((tm, HEAD_DIM), lambda j, i, tab: (i, 0))
    return pl.pallas_call(
        functools.partial(_mm_cols_norm_rope_kernel, rope=rope),
        grid_spec=pltpu.PrefetchScalarGridSpec(
            num_scalar_prefetch=1, grid=(len(col_tiles), m // tm),
            in_specs=[pl.BlockSpec((tm, kk), lambda j, i, tab: (i, 0)),
                      pl.BlockSpec((kk, tn), lambda j, i, tab: (0, tab[j])), vec, vec, table, table],
            out_specs=pl.BlockSpec((tm, tn), lambda j, i, tab: (i, j)),
        ),
        out_shape=jax.ShapeDtypeStruct((m, n), _BF16),
        compiler_params=_cparams("parallel", "parallel"), name="matmul_cols_norm_rope",
    )(jnp.asarray(col_tiles, jnp.int32), a, w, gains[None, :], post_scale[None, :], cos, sin_signed)


def _rope_tables(n):
    axis_dim = HEAD_DIM // 2
    t = jnp.arange(n)
    row = (t // GRID_W).astype(_F32)
    col = (t % GRID_W).astype(_F32)
    inv = ROPE_THETA ** (-jnp.arange(0, axis_dim, 2, dtype=_F32) / axis_dim)
    ang_r = row[:, None] * inv
    ang_c = col[:, None] * inv
    cos = jnp.concatenate([jnp.cos(ang_r)] * 2 + [jnp.cos(ang_c)] * 2, axis=-1)
    sin = jnp.concatenate([-jnp.sin(ang_r), jnp.sin(ang_r), -jnp.sin(ang_c), jnp.sin(ang_c)], axis=-1)
    return cos, sin


def _dot_nt(a, b):
    return lax.dot_general(a, b, (((1,), (1,)), ((), ())), preferred_element_type=_F32)


_GQA_Q_BLOCK = 256
_GQA_KEY_BLOCK = 8320
_MXU_WIDTH = 256


def _gqa_kernel(q_ref, k_ref, v_ref, o_ref, m_ref, acc_ref, *, tk):
    group = q_ref.shape[1] // HEAD_DIM
    m_ref[...] = jnp.full(m_ref.shape, -jnp.inf, _F32)
    acc_ref[...] = jnp.zeros(acc_ref.shape, _F32)

    def body(j, carry):
        start = pl.multiple_of(j * tk, tk)
        for g in range(group):
            s = _dot_nt(q_ref[:, g * HEAD_DIM:(g + 1) * HEAD_DIM], k_ref[pl.ds(start, tk), :])
            m_prev = m_ref[g]
            m_new = jnp.maximum(m_prev, jnp.max(s, axis=-1, keepdims=True))
            p = jnp.exp2(s - m_new)
            acc_ref[g] = jnp.exp2(m_prev - m_new) * acc_ref[g] + jnp.dot(
                p.astype(_BF16), v_ref[pl.ds(start, tk), :], preferred_element_type=_F32)
            m_ref[g] = m_new
        return carry

    lax.fori_loop(0, k_ref.shape[0] // tk, body, 0)
    for g in range(group):
        acc = acc_ref[g]
        o_ref[:, g * HEAD_DIM:(g + 1) * HEAD_DIM] = (acc[:, :HEAD_DIM] / acc[:, HEAD_DIM:HEAD_DIM + 1]).astype(
            o_ref.dtype)


def _gqa_attention(q, k, v, n_kv_heads, n_q_heads):
    l = q.shape[0]
    qw = n_q_heads * HEAD_DIM
    s = k.shape[0]
    gw = qw // n_kv_heads
    tq = _tile(l, _GQA_Q_BLOCK)
    tk = _tile(s, _GQA_KEY_BLOCK, 128)
    group = gw // HEAD_DIM
    ones_col = jnp.zeros((s, n_kv_heads, HEAD_DIM), v.dtype).at[:, :, 0].set(1)
    v_aug = jnp.concatenate([v.reshape(s, n_kv_heads, HEAD_DIM), ones_col], axis=-1).reshape(s, -1)
    return pl.pallas_call(
        functools.partial(_gqa_kernel, tk=tk),
        grid=(n_kv_heads, l // tq),
        in_specs=[
            pl.BlockSpec((tq, gw), lambda h, i: (i, h)),
            pl.BlockSpec((s, HEAD_DIM), lambda h, i: (0, h), pipeline_mode=pl.Buffered(1)),
            pl.BlockSpec((s, 2 * HEAD_DIM), lambda h, i: (0, h), pipeline_mode=pl.Buffered(1)),
        ],
        out_specs=pl.BlockSpec((tq, gw), lambda h, i: (i, h)),
        out_shape=jax.ShapeDtypeStruct((l, qw), _BF16),
        scratch_shapes=[pltpu.VMEM((group, tq, 1), _F32), pltpu.VMEM((group, tq, 2 * HEAD_DIM), _F32)],
        compiler_params=_cparams("arbitrary", "arbitrary"), name="gqa_attention",
    )(q, k, v_aug)


_NA_QROWS = 8
_NA_KROWS = 16
_NA_CHAINS = 4


def _na_patterns():
    first = dict(delta=0, a=[max(i - NA_WIN_H // 2, 0) for i in range(_NA_QROWS)])
    inner = dict(delta=-(NA_WIN_H // 2), a=list(range(_NA_QROWS)))
    last = dict(delta=-(_NA_KROWS - _NA_QROWS),
                a=[min(i + NA_WIN_H // 2, _NA_KROWS - NA_WIN_H) for i in range(_NA_QROWS)])
    return first, inner, last


def _na_kernel(q_ref, k_ref, v_ref, kc_ref, vc_ref, tb_ref, o_ref, bias_head, bias_inner, bias_tail, *, n_blocks,
               scale):
    step = pl.program_id(1)
    n_steps = n_blocks // _NA_CHAINS
    w = GRID_W
    tq = _NA_QROWS * w
    lane_lo = lax.broadcasted_iota(jnp.int32, (w, 2 * w), 1) < w

    def build_bias(bias_ref, pat):
        for i in range(_NA_QROWS):
            for jj in range(_NA_KROWS // 2):
                j0 = 2 * jj
                in0 = 0 <= j0 - pat["a"][i] < NA_WIN_H
                in1 = 0 <= j0 + 1 - pat["a"][i] < NA_WIN_H
                rel0 = pat["delta"] + j0 - i + NA_WIN_H - 1
                if not (in0 or in1):
                    tile = jnp.full((w, 2 * w), _NEG, _F32)
                else:
                    tile = tb_ref[rel0 + 1]
                    if not in1:
                        tile = jnp.where(lane_lo, tile, _NEG)
                    elif not in0:
                        tile = jnp.where(lane_lo, _NEG, tile)
                bias_ref[i * w:(i + 1) * w, j0 * w:(j0 + 2) * w] = tile

    first, inner, last = _na_patterns()
    pl.when(step == 0)(functools.partial(build_bias, bias_head, first))
    pl.when(step == 0)(functools.partial(build_bias, bias_inner, inner))
    pl.when(step == 0)(functools.partial(build_bias, bias_tail, inner))
    pl.when(step == 1)(functools.partial(build_bias, bias_head, inner))
    pl.when(step == n_steps - 1)(functools.partial(build_bias, bias_tail, last))

    kc = kc_ref[...]
    vc = vc_ref[...]
    chain_bias = [bias_head] + [bias_inner] * (_NA_CHAINS - 2) + [bias_tail]
    for half, bias_ref in enumerate(chain_bias):
        b = _NA_CHAINS * step + half
        slab_row = jnp.where(b == 0, 0, jnp.where(b == n_blocks - 1, (n_blocks - 2) * _NA_QROWS,
                                                  b * _NA_QROWS - NA_WIN_H // 2))
        start = pl.multiple_of(slab_row * w, w)
        q = q_ref[half * tq:(half + 1) * tq, :]
        s_loc = _dot_nt(q, k_ref[pl.ds(start, _NA_KROWS * w), :]) * scale + bias_ref[...]
        s_ctx = _dot_nt(q, kc) * scale
        m = jnp.maximum(jnp.max(s_loc, axis=-1, keepdims=True), jnp.max(s_ctx, axis=-1, keepdims=True))
        p_loc = jnp.exp(s_loc - m)
        p_ctx = jnp.exp(s_ctx - m)
        denom = jnp.sum(p_loc, axis=-1, keepdims=True) + jnp.sum(p_ctx, axis=-1, keepdims=True)
        o = (jnp.dot(p_loc.astype(_BF16), v_ref[pl.ds(start, _NA_KROWS * w), :], preferred_element_type=_F32)
             + jnp.dot(p_ctx.astype(_BF16), vc, preferred_element_type=_F32))
        o_ref[half * tq:(half + 1) * tq, :] = (o / denom).astype(o_ref.dtype)


def _na_bias_table(rpb):
    qc = np.arange(GRID_W)
    win_start = np.clip(qc - NA_WIN_W // 2, 0, GRID_W - NA_WIN_W)
    rel = qc[None, :] - qc[:, None]
    in_win = (qc[None, :] >= win_start[:, None]) & (qc[None, :] < win_start[:, None] + NA_WIN_W)
    rel_idx = np.clip(rel + NA_WIN_W - 1, 0, 2 * NA_WIN_W - 2)
    toe = jnp.where(in_win[None, None], rpb[:, :, rel_idx], _NEG).astype(_F32)
    neg = jnp.full_like(toe[:, :1], _NEG)
    lo = jnp.concatenate([neg, toe], axis=1)
    hi = jnp.concatenate([toe, neg], axis=1)
    return jnp.concatenate([lo, hi], axis=-1)


def _na_attention(px, pc, rpb, n_heads, q_head0, k_head0, v_head0, ck_head0, cv_head0):
    l = px.shape[0]
    lc = pc.shape[0]
    rows = l // GRID_W
    n_blocks = rows // _NA_QROWS
    assert rows % (_NA_CHAINS * _NA_QROWS) == 0 and n_blocks >= 3
    tq = _NA_CHAINS * _NA_QROWS * GRID_W
    tb = _na_bias_table(rpb)
    head_blk = lambda n, h0: pl.BlockSpec((n, HEAD_DIM), lambda h, b: (0, h0 + h))
    bias_scratch = pltpu.VMEM((_NA_QROWS * GRID_W, _NA_KROWS * GRID_W), _F32)
    return pl.pallas_call(
        functools.partial(_na_kernel, n_blocks=n_blocks, scale=HEAD_DIM ** -0.5),
        grid=(n_heads, n_blocks // _NA_CHAINS),
        in_specs=[
            pl.BlockSpec((tq, HEAD_DIM), lambda h, b: (b, q_head0 + h)),
            head_blk(l, k_head0), head_blk(l, v_head0), head_blk(lc, ck_head0), head_blk(lc, cv_head0),
            pl.BlockSpec((None, 2 * NA_WIN_H, GRID_W, 2 * GRID_W), lambda h, b: (h, 0, 0, 0)),
        ],
        out_specs=pl.BlockSpec((tq, HEAD_DIM), lambda h, b: (b, h)),
        out_shape=jax.ShapeDtypeStruct((l, n_heads * HEAD_DIM), _BF16),
        scratch_shapes=[bias_scratch, bias_scratch, bias_scratch],
        compiler_params=_cparams("arbitrary", "arbitrary"), name="na_attention",
    )(px, px, px, pc, pc, tb)


_POOL_HALO = 16


def _pool_kernel(prev_ref, cur_ref, next_ref, g_ref, sh_ref, sc_ref, o_ref, *, n_tokens):
    i = pl.program_id(0)
    tm, d = cur_ref.shape
    dg = d // len(POOL_WINDOWS)
    x = jnp.concatenate([prev_ref[...], cur_ref[...], next_ref[...]], axis=0)
    a = _rms(x, g_ref[...]) * (1.0 + sc_ref[...]) + sh_ref[...]
    pos = i * tm - _POOL_HALO + lax.broadcasted_iota(jnp.int32, (tm + 2 * _POOL_HALO, 1), 0)
    a = jnp.where((pos >= 0) & (pos < n_tokens), a, 0.0)
    t = i * tm + lax.broadcasted_iota(jnp.int32, (tm, 1), 0)
    n_ext = tm + 2 * _POOL_HALO
    for g, w in enumerate(POOL_WINDOWS):
        ag = a[:, g * dg:(g + 1) * dg]
        run, span = ag, 1
        while span < w:
            run = run[:n_ext - 2 * span + 1, :] + run[span:n_ext - span + 1, :]
            span *= 2
        tot = run[_POOL_HALO - w // 2:_POOL_HALO - w // 2 + tm, :]
        cnt = jnp.minimum(t + (w - w // 2), n_tokens) - jnp.maximum(t - w // 2, 0)
        diff = tot / cnt.astype(_F32) - ag[_POOL_HALO:_POOL_HALO + tm, :]
        o_ref[:, g * dg:(g + 1) * dg] = diff.astype(o_ref.dtype)


def _pool_diffs(x, gain, shift, scale):
    t, d = x.shape
    assert all(w & (w - 1) == 0 and w // 2 <= _POOL_HALO for w in POOL_WINDOWS)
    tm = _tile(t, 256, _POOL_HALO)
    hb = tm // _POOL_HALO
    n_halo_blocks = t // _POOL_HALO
    row = pl.BlockSpec((1, d), lambda i: (0, 0))
    return pl.pallas_call(
        functools.partial(_pool_kernel, n_tokens=t), grid=(t // tm,),
        in_specs=[
            pl.BlockSpec((_POOL_HALO, d), lambda i: (jnp.maximum(i * hb - 1, 0), 0)),
            pl.BlockSpec((tm, d), lambda i: (i, 0)),
            pl.BlockSpec((_POOL_HALO, d), lambda i: (jnp.minimum((i + 1) * hb, n_halo_blocks - 1), 0)),
            row, row, row,
        ],
        out_specs=pl.BlockSpec((tm, d), lambda i: (i, 0)),
        out_shape=jax.ShapeDtypeStruct((t, d), _BF16),
        compiler_params=_cparams("parallel"), name="pool_diffs",
    )(x, x, x, gain[None, :], shift[None, :], scale[None, :])


def _router_kernel(t_ref, w_ref, b_ref, idx_ref, gate_ref, rank_ref, cnt_ref):
    n_exp = w_ref.shape[0]
    tm = t_ref.shape[0]
    per_group = n_exp // N_EXPERT_GROUPS

    @pl.when(pl.program_id(0) == 0)
    def _():
        cnt_ref[...] = jnp.zeros(cnt_ref.shape, _F32)

    logits = _dot_nt(w_ref[...], t_ref[...])
    scores = 1.0 / (1.0 + jnp.exp(-logits))
    biased = scores + b_ref[...]

    def first_max(vals):
        idx = lax.broadcasted_iota(jnp.int32, vals.shape, 0).astype(_F32)
        m = jnp.max(vals, axis=0, keepdims=True)
        first = jnp.min(jnp.where(vals == m, idx, float(vals.shape[0])), axis=0, keepdims=True)
        return m, idx == first

    def take_top(vals, k):
        hits = []
        for _ in range(k):
            _, hit = first_max(vals)
            hits.append(hit)
            vals = jnp.where(hit, -jnp.inf, vals)
        return hits

    def union(hits):
        sel = jnp.zeros(hits[0].shape, _F32)
        for hit in hits:
            sel = jnp.where(hit, 1.0, sel)
        return sel

    group_scores = []
    for g in range(N_EXPERT_GROUPS):
        v = biased[g * per_group:(g + 1) * per_group, :]
        m1, hit = first_max(v)
        m2 = jnp.max(jnp.where(hit, -jnp.inf, v), axis=0, keepdims=True)
        group_scores.append(m1 + m2)
    gsel = union(take_top(jnp.concatenate(group_scores, axis=0), TOPK_GROUPS))
    keep = jnp.concatenate([jnp.broadcast_to(gsel[g:g + 1, :], (per_group, tm)) for g in range(N_EXPERT_GROUPS)],
                           axis=0)
    hits = take_top(jnp.where(keep > 0.0, biased, -jnp.inf), TOP_K)
    esel = union(hits)
    wsel = esel * scores
    gates = wsel / jnp.sum(wsel, axis=0, keepdims=True) * ROUTED_SCALE

    earlier = (lax.broadcasted_iota(jnp.int32, (tm, tm), 0) < lax.broadcasted_iota(jnp.int32, (tm, tm), 1))
    rank = cnt_ref[...] + jnp.dot(esel.astype(_BF16), jnp.where(earlier, 1.0, 0.0).astype(_BF16),
                                  preferred_element_type=_F32)
    cnt_ref[...] += jnp.sum(esel, axis=1, keepdims=True)

    eidx = lax.broadcasted_iota(jnp.int32, (n_exp, tm), 0).astype(_F32)
    pick = lambda hit, v: jnp.sum(jnp.where(hit, v, 0.0), axis=0, keepdims=True)
    idx_ref[...] = jnp.concatenate([pick(h, eidx) for h in hits], axis=0).astype(jnp.int32)
    gate_ref[...] = jnp.concatenate([pick(h, gates) for h in hits], axis=0)
    rank_ref[...] = jnp.concatenate([pick(h, rank) for h in hits], axis=0).astype(jnp.int32)


def _router(t, w_router, router_bias):
    n_tok, d = t.shape
    n_exp = w_router.shape[1]
    tm = _tile(n_tok, 512, 128)
    per_tok = pl.BlockSpec((TOP_K, tm), lambda i: (0, i))
    return pl.pallas_call(
        _router_kernel, grid=(n_tok // tm,),
        in_specs=[
            pl.BlockSpec((tm, d), lambda i: (i, 0)),
            pl.BlockSpec((n_exp, d), lambda i: (0, 0)),
            pl.BlockSpec((n_exp, 1), lambda i: (0, 0)),
        ],
        out_specs=[per_tok, per_tok, per_tok, pl.BlockSpec((n_exp, 1), lambda i: (0, 0))],
        out_shape=[jax.ShapeDtypeStruct((TOP_K, n_tok), jnp.int32), jax.ShapeDtypeStruct((TOP_K, n_tok), _F32),
                   jax.ShapeDtypeStruct((TOP_K, n_tok), jnp.int32), jax.ShapeDtypeStruct((n_exp, 1), _F32)],
        compiler_params=_cparams("arbitrary"), name="moe_router",
    )(t, w_router.T.astype(_BF16), router_bias[:, None])


_MOE_TILE_ROWS = 256


def _packed_cols(d):
    return d // 2


def _pack_pairs(y):
    half = y.shape[1] // 2
    bits = lambda v: lax.bitcast_convert_type(v.astype(_BF16).astype(_F32), jnp.uint32)
    return (bits(y[:, :half]) & jnp.uint32(0xFFFF0000)) | (bits(y[:, half:]) >> jnp.uint32(16))


def _unpack_pairs(w):
    hi = lax.bitcast_convert_type(w & jnp.uint32(0xFFFF0000), _F32)
    lo = lax.bitcast_convert_type(w << jnp.uint32(16), _F32)
    return hi, lo


def _norm_mod_pack_kernel(x_ref, g_ref, sh_ref, sc_ref, o_ref, p_ref):
    y = _rms(x_ref[...], g_ref[...]) * (1.0 + sc_ref[...]) + sh_ref[...]
    o_ref[...] = y.astype(o_ref.dtype)
    p_ref[...] = _pack_pairs(y)


def _norm_mod_pack(x, gain, shift, scale):
    t, d = x.shape
    tm = _tile(t, 256)
    row = pl.BlockSpec((1, d), lambda i: (0, 0))
    blk = pl.BlockSpec((tm, d), lambda i: (i, 0))
    return pl.pallas_call(
        _norm_mod_pack_kernel, grid=(t // tm,), in_specs=[blk, row, row, row],
        out_specs=[blk, pl.BlockSpec((tm, _packed_cols(d)), lambda i: (i, 0))],
        out_shape=[jax.ShapeDtypeStruct((t, d), _BF16), jax.ShapeDtypeStruct((t, _packed_cols(d)), jnp.uint32)],
        compiler_params=_cparams("parallel"), name="rmsnorm_modulate_pack",
    )(x, gain[None, :], shift[None, :], scale[None, :])


def _moe_plan(idx8, rank8, counts, tile_rows):
    n_exp = counts.shape[0]
    n_tok = idx8.shape[1]
    cnt = counts[:, 0].astype(jnp.int32)
    padded = (cnt + tile_rows - 1) // tile_rows * tile_rows
    ends = jnp.cumsum(padded)
    starts = ends - padded
    onehot = idx8[:, :, None] == jnp.arange(n_exp, dtype=jnp.int32)
    pos8 = rank8 + jnp.sum(jnp.where(onehot, starts, 0), axis=-1)
    n_tiles = TOP_K * n_tok // tile_rows + n_exp
    tile_row0 = jnp.arange(n_tiles, dtype=jnp.int32) * tile_rows
    tile_expert = jnp.minimum(jnp.sum(tile_row0[:, None] >= ends[None, :], axis=1), n_exp - 1).astype(jnp.int32)
    n_active = (ends[-1:] // tile_rows).astype(jnp.int32)
    last_tile_row = jnp.where(padded > 0, ends - tile_rows, -1).astype(jnp.int32)
    return pos8.T.reshape(-1), tile_expert, n_active, last_tile_row, n_tiles


def _dispatch_kernel(pos_ref, last_ref, na_ref, tp_ref, xs_ref, zero_ref, sem, zsem, *, tile_rows):
    i = pl.program_id(0)
    tb = tp_ref.shape[0]
    n_exp = last_ref.shape[0]
    n_tiles = xs_ref.shape[0] // tile_rows

    @pl.when(i == 0)
    def _():
        zero_ref[...] = jnp.zeros(zero_ref.shape, zero_ref.dtype)

        def fill(row0):
            return pltpu.make_async_copy(zero_ref, xs_ref.at[pl.ds(pl.multiple_of(row0, tile_rows), tile_rows)],
                                         zsem)

        def each(e, start):
            tail = na_ref[0] + e

            def go(row0):
                cp = fill(row0)
                cp.start() if start else cp.wait()

            pl.when(last_ref[e] >= 0)(lambda: go(jnp.maximum(last_ref[e], 0)))
            pl.when(tail < n_tiles)(lambda: go(jnp.minimum(tail, n_tiles - 1) * tile_rows))

        lax.fori_loop(0, n_exp, lambda e, c: (each(e, True), c)[1], 0)
        lax.fori_loop(0, n_exp, lambda e, c: (each(e, False), c)[1], 0)

    def row(r, c):
        base = (i * tb + r) * TOP_K
        for k in range(TOP_K):
            pltpu.make_async_copy(tp_ref.at[pl.ds(r, 1)], xs_ref.at[pl.ds(pos_ref[base + k], 1)], sem).start()
        return c

    lax.fori_loop(0, tb, row, 0)
    for _ in range(TOP_K):
        pltpu.make_async_copy(tp_ref, xs_ref.at[pl.ds(0, tb)], sem).wait()


def _dispatch(tp, pos_flat, last_tile_row, n_active, n_rows, tile_rows):
    n_tok, c = tp.shape
    tb = _tile(n_tok, 256)
    return pl.pallas_call(
        functools.partial(_dispatch_kernel, tile_rows=tile_rows),
        grid_spec=pltpu.PrefetchScalarGridSpec(
            num_scalar_prefetch=3, grid=(n_tok // tb,),
            in_specs=[pl.BlockSpec((tb, c), lambda i, pos, last, na: (i, 0))],
            out_specs=pl.BlockSpec(memory_space=pl.ANY),
            scratch_shapes=[pltpu.VMEM((tile_rows, c), jnp.uint32), pltpu.SemaphoreType.DMA,
                            pltpu.SemaphoreType.DMA],
        ),
        out_shape=jax.ShapeDtypeStruct((n_rows, c), jnp.uint32),
        compiler_params=_cparams("arbitrary"), name="moe_dispatch",
    )(pos_flat, last_tile_row, n_active, tp)


def _grouped_ffn_kernel(te_ref, na_ref, xs_ref, wg_ref, wu_ref, wd_ref, ys_ref, wg_bf, wu_bf, wd_bf):
    i = pl.program_id(0)
    active = i < na_ref[0]
    new_expert = jnp.logical_or(i == 0, te_ref[i] != te_ref[jnp.maximum(i - 1, 0)])

    @pl.when(jnp.logical_and(active, new_expert))
    def _():
        wg_bf[...] = wg_ref[...].astype(_BF16)
        wu_bf[...] = wu_ref[...].astype(_BF16)
        wd_bf[...] = wd_ref[...].astype(_BF16)

    @pl.when(active)
    def _():
        half = wg_bf.shape[0] // 2
        hi, lo = _unpack_pairs(xs_ref[...])
        hi = hi.astype(_BF16)
        lo = lo.astype(_BF16)
        dot = lambda a, b: jnp.dot(a, b, preferred_element_type=_F32)
        hg = dot(hi, wg_bf[:half, :]) + dot(lo, wg_bf[half:, :])
        hu = dot(hi, wu_bf[:half, :]) + dot(lo, wu_bf[half:, :])
        h = (_silu(hg) * hu).astype(_BF16)
        ys_ref[...] = _pack_pairs(dot(h, wd_bf[...]))

    @pl.when(jnp.logical_not(active))
    def _():
        ys_ref[...] = jnp.zeros(ys_ref.shape, ys_ref.dtype)


def _grouped_ffn(xs, tile_expert, n_active, w_gate, w_up, w_down, layer, tile_rows):
    n_rows, c = xs.shape
    _, n_exp, d, ff = w_gate.shape
    n_tiles = n_rows // tile_rows
    row_blk = pl.BlockSpec((tile_rows, c), lambda i, te, na: (jnp.minimum(i, jnp.maximum(na[0] - 1, 0)), 0))
    return pl.pallas_call(
        _grouped_ffn_kernel,
        grid_spec=pltpu.PrefetchScalarGridSpec(
            num_scalar_prefetch=2, grid=(n_tiles,),
            in_specs=[row_blk,
                      pl.BlockSpec((None, None, d, ff), lambda i, te, na: (layer, te[i], 0, 0)),
                      pl.BlockSpec((None, None, d, ff), lambda i, te, na: (layer, te[i], 0, 0)),
                      pl.BlockSpec((None, None, ff, d), lambda i, te, na: (layer, te[i], 0, 0))],
            out_specs=pl.BlockSpec((tile_rows, c), lambda i, te, na: (i, 0)),
            scratch_shapes=[pltpu.VMEM((d, ff), _BF16), pltpu.VMEM((d, ff), _BF16), pltpu.VMEM((ff, d), _BF16)],
        ),
        out_shape=jax.ShapeDtypeStruct((n_rows, c), jnp.uint32),
        compiler_params=_cparams("arbitrary"), name="moe_grouped_ffn",
    )(tile_expert, n_active, xs, w_gate, w_up, w_down)


def _combine_kernel(pos_ref, ys_ref, g8_ref, x_ref, sh_ref, g2_ref, fg_ref, o_ref, buf, sem, *, final_norm):
    i = pl.program_id(0)
    tb, d = x_ref.shape
    half = d // 2

    def row(r, c):
        base = (i * tb + r) * TOP_K
        for k in range(TOP_K):
            pltpu.make_async_copy(ys_ref.at[pl.ds(pos_ref[base + k], 1)], buf.at[k, pl.ds(r, 1)], sem).start()
        return c

    lax.fori_loop(0, tb, row, 0)
    for k in range(TOP_K):
        pltpu.make_async_copy(ys_ref.at[pl.ds(0, tb)], buf.at[k], sem).wait()

    g8 = g8_ref[...]
    acc_hi = jnp.zeros((tb, half), _F32)
    acc_lo = jnp.zeros((tb, half), _F32)
    for k in range(TOP_K):
        hi, lo = _unpack_pairs(buf[k])
        acc_hi = acc_hi + g8[:, k:k + 1] * hi
        acc_lo = acc_lo + g8[:, k:k + 1] * lo
    out_hi = x_ref[:, :half] + g2_ref[:, :half] * (sh_ref[:, :half] + acc_hi)
    out_lo = x_ref[:, half:] + g2_ref[:, half:] * (sh_ref[:, half:] + acc_lo)
    if final_norm:
        ssq = jnp.sum(out_hi * out_hi, axis=-1, keepdims=True) + jnp.sum(out_lo * out_lo, axis=-1, keepdims=True)
        inv = lax.rsqrt(ssq / d + RMS_EPS)
        out_hi = out_hi * inv * fg_ref[:, :half]
        out_lo = out_lo * inv * fg_ref[:, half:]
    o_ref[:, :half] = out_hi
    o_ref[:, half:] = out_lo


def _combine(ys, pos_flat, gate8, x, shared, gate2, final_gain):
    n_tok, d = x.shape
    c = ys.shape[1]
    tb = _tile(n_tok, 128)
    blk = pl.BlockSpec((tb, d), lambda i, pos: (i, 0))
    vec = pl.BlockSpec((1, d), lambda i, pos: (0, 0))
    final_norm = final_gain is not None
    if not final_norm:
        final_gain = jnp.ones((d,), _F32)
    return pl.pallas_call(
        functools.partial(_combine_kernel, final_norm=final_norm),
        grid_spec=pltpu.PrefetchScalarGridSpec(
            num_scalar_prefetch=1, grid=(n_tok // tb,),
            in_specs=[pl.BlockSpec(memory_space=pl.ANY),
                      pl.BlockSpec((tb, TOP_K), lambda i, pos: (i, 0)),
                      blk, blk, vec, vec],
            out_specs=blk,
            scratch_shapes=[pltpu.VMEM((TOP_K, tb, c), jnp.uint32), pltpu.SemaphoreType.DMA],
        ),
        out_shape=jax.ShapeDtypeStruct((n_tok, d), _F32),
        compiler_params=_cparams("arbitrary"), name="moe_combine",
    )(pos_flat, ys, gate8, x, shared, gate2[None, :], final_gain[None, :])


def _ffn_kernel(t_ref, wg_ref, wu_ref, wd_ref, o_ref):
    e = pl.program_id(1)
    t = t_ref[...]
    h = _silu(jnp.dot(t, wg_ref[...], preferred_element_type=_F32)) * jnp.dot(t, wu_ref[...],
                                                                               preferred_element_type=_F32)
    y = jnp.dot(h.astype(_BF16), wd_ref[...], preferred_element_type=_F32)

    @pl.when(e == 0)
    def _():
        o_ref[...] = y

    @pl.when(e > 0)
    def _():
        o_ref[...] += y


def _ffn(t, w_gate, w_up, w_down):
    n_tok, d = t.shape
    n_chunks, _, ff = w_gate.shape
    tm = _tile(n_tok, 512)
    blk = pl.BlockSpec((tm, d), lambda i, e: (i, 0))
    return pl.pallas_call(
        _ffn_kernel, grid=(n_tok // tm, n_chunks),
        in_specs=[blk,
                  pl.BlockSpec((None, d, ff), lambda i, e: (e, 0, 0)),
                  pl.BlockSpec((None, d, ff), lambda i, e: (e, 0, 0)),
                  pl.BlockSpec((None, ff, d), lambda i, e: (e, 0, 0))],
        out_specs=blk,
        out_shape=jax.ShapeDtypeStruct((n_tok, d), _F32),
        compiler_params=_cparams("parallel", "arbitrary"), name="shared_ffn",
    )(t, w_gate, w_up, w_down)


def _moe_residual(x, gain, shift, scale, gate2, w_router, router_bias, w_gate, w_up, w_down, layer, ws_gate, ws_up,
                  ws_down, final_gain):
    d = x.shape[1]
    t, tp = _norm_mod_pack(x, gain, shift, scale)
    idx8, gate8, rank8, counts = _router(t, w_router, router_bias)
    pos_flat, tile_expert, n_active, last_tile_row, n_tiles = _moe_plan(idx8, rank8, counts, _MOE_TILE_ROWS)
    xs = _dispatch(tp, pos_flat, last_tile_row, n_active, n_tiles * _MOE_TILE_ROWS, _MOE_TILE_ROWS)
    ys = _grouped_ffn(xs, tile_expert, n_active, w_gate, w_up, w_down, layer, _MOE_TILE_ROWS)
    sff = ws_gate.shape[1]
    ff = w_gate.shape[3]
    n_sh = sff // ff
    to_chunks = lambda w: w.reshape(d, n_sh, ff).transpose(1, 0, 2).astype(_BF16)
    shared = _ffn(t, to_chunks(ws_gate), to_chunks(ws_up), ws_down.reshape(n_sh, ff, d).astype(_BF16))
    return _combine(ys, pos_flat, gate8.T, x, shared, gate2, final_gain)


def _attention_layer(x, h_c, mods, mods_c, g_mix, w_in, w_out, rpb, q_gain, k_gain):
    shift1, scale1, gate1 = mods[:3]
    cshift1, cscale1 = mods_c[:2]
    n_tok = x.shape[0]
    na_w = NA_HEADS * HEAD_DIM
    gq_w = GQA_Q_HEADS * HEAD_DIM
    gkv_w = GQA_KV_HEADS * HEAD_DIM
    q_cols = na_w + gq_w
    o_nk, o_gk, o_gv, o_end = q_cols, q_cols + 2 * na_w, q_cols + 2 * na_w + gkv_w, w_in.shape[1]
    tn = int(np.gcd.reduce([na_w, gq_w, gkv_w, 512]))
    tiles = lambda lo, hi: list(range(lo // tn, hi // tn))
    w_bf = w_in.astype(_BF16)

    a_x = _norm_mod(x, g_mix, shift1, scale1, _BF16)
    a_c = _norm_mod(h_c, g_mix, cshift1, cscale1, _BF16)

    kv_plain = tiles(o_nk, o_gk) + tiles(o_gv, o_end)
    p_plain = _matmul_cols(a_x, w_bf, tiles(0, na_w) + kv_plain, tn, _BF16)
    c_plain = _matmul_cols(a_c, w_bf, kv_plain, tn, _BF16)

    cos, sin = _rope_tables(n_tok)
    gains = jnp.concatenate([jnp.tile(q_gain, GQA_Q_HEADS), jnp.tile(k_gain, GQA_KV_HEADS)])
    post = jnp.concatenate([jnp.full((gq_w,), HEAD_DIM ** -0.5 * np.log2(np.e), _F32), jnp.ones((gkv_w,), _F32)])
    qk = _matmul_cols_norm_rope(a_x, w_bf, tiles(na_w, q_cols) + tiles(o_gk, o_gv), tn, gains, post, cos,
                                sin)
    ck = _matmul_cols_norm_rope(a_c, w_bf, tiles(o_gk, o_gv), tn, gains[gq_w:], post[gq_w:], None, None)

    o_a = _na_attention(p_plain, c_plain, rpb, NA_HEADS, 0, NA_HEADS, 2 * NA_HEADS, 0, NA_HEADS)
    k_all = jnp.concatenate([qk[:, gq_w:], ck], axis=0)
    v_all = jnp.concatenate([p_plain[:, 3 * na_w:], c_plain[:, 2 * na_w:]], axis=0)
    o_b = _gqa_attention(qk, k_all, v_all, GQA_KV_HEADS, GQA_Q_HEADS)
    return _matmul2_res(o_a, o_b, w_out.astype(_BF16), x, gate1)


def _pool_layer(x, mods, g_mix, w_pool, pool_scale):
    shift1, scale1, gate1 = mods[:3]
    diffs = _pool_diffs(x, g_mix, shift1, scale1)
    return _matmul(diffs, w_pool.astype(_BF16), grouped=True, res=x, gate=gate1, colscale=pool_scale)


def kernel(x, c, ctx, c_ctx, w_mod, b_mod, g_mix, g_ffn, attn_w_in, attn_w_out, na_rpb, q_gain, k_gain, pool_w,
           pool_scale, moe_w_router, moe_router_bias, moe_w_gate, moe_w_up, moe_w_down, moe_ws_gate, moe_ws_up,
           moe_ws_down, g_final):
    batch, seq, d = x.shape
    depth = w_mod.shape[0]
    outs = []
    for bi in range(batch):
        xb = x[bi]
        h_c = ctx[bi]
        conds = jnp.stack([c[bi], c_ctx])
        for i in range(depth):
            j = i // 2
            mod_rows = _adaln(conds, w_mod, i, b_mod[i])
            mods = jnp.split(mod_rows[0], N_MOD)
            mods_c = jnp.split(mod_rows[1], N_MOD)
            if i % 2 == 0:
                xb = _attention_layer(xb, h_c, mods, mods_c, g_mix[i], attn_w_in[j], attn_w_out[j], na_rpb[j],
                                      q_gain[j], k_gain[j])
            else:
                xb = _pool_layer(xb, mods, g_mix[i], pool_w[j], pool_scale[j])
            assert not any(l % 2 == 0 for l in range(i + 1, depth)), "context-stream update not implemented"
            xb = _moe_residual(xb, g_ffn[i], mods[3], mods[4], mods[5], moe_w_router[i], moe_router_bias[i],
                               moe_w_gate, moe_w_up, moe_w_down, i, moe_ws_gate[i], moe_ws_up[i], moe_ws_down[i],
                               g_final if i == depth - 1 else None)
        outs.append(xb)
    return jnp.stack(outs)
```

```python
import functools

import numpy as np
import jax
import jax.numpy as jnp
from jax import lax
from jax.experimental import pallas as pl
from jax.experimental.pallas import tpu as pltpu

GRID_W = 64
HEAD_DIM = 128
NA_HEADS = 16
GQA_Q_HEADS = 16
GQA_KV_HEADS = 4
NA_WIN_H = 8
NA_WIN_W = 16
ROPE_THETA = 10000.0
POOL_WINDOWS = (2, 4, 8, 16)
N_EXPERT_GROUPS = 8
TOPK_GROUPS = 4
TOP_K = 8
ROUTED_SCALE = 2.5
N_MOD = 6
RMS_EPS = 1e-6

_F32 = jnp.float32
_BF16 = jnp.bfloat16
_NEG = -1e30
_VMEM_LIMIT_BYTES = 56 * 1024 * 1024


def _cparams(*sem):
    return pltpu.CompilerParams(dimension_semantics=sem, vmem_limit_bytes=_VMEM_LIMIT_BYTES)


def _tile(n, target, mult=8):
    t = min(n, target) // mult * mult
    while t > mult and n % t:
        t -= mult
    assert t > 0 and n % t == 0, (n, target, mult)
    return t


def _silu(x):
    return x / (1.0 + jnp.exp(-x))


def _adaln_kernel(c_ref, w_ref, b_ref, o_ref, *, kc):
    n_rows, d, _ = c_ref.shape
    bn = w_ref.shape[1]
    accs = [jnp.zeros((1, bn), _F32) for _ in range(n_rows)]
    for k0 in range(0, d, kc):
        w = w_ref[k0:k0 + kc, :]
        for r in range(n_rows):
            cc = _silu(c_ref[r, k0:k0 + kc, :])
            accs[r] = accs[r] + jnp.sum(cc * w, axis=0, keepdims=True)
    o_ref[...] = jnp.concatenate(accs, axis=0) + b_ref[...]


def _adaln(conds, w_mod, layer, b_mod):
    n_rows, d = conds.shape
    n = w_mod.shape[2]
    bn = _tile(n, 512, 128)
    return pl.pallas_call(
        functools.partial(_adaln_kernel, kc=_tile(d, 256)),
        grid=(n // bn,),
        in_specs=[
            pl.BlockSpec((n_rows, d, 1), lambda j: (0, 0, 0)),
            pl.BlockSpec((None, d, bn), lambda j: (layer, 0, j)),
            pl.BlockSpec((1, bn), lambda j: (0, j)),
        ],
        out_specs=pl.BlockSpec((n_rows, bn), lambda j: (0, j)),
        out_shape=jax.ShapeDtypeStruct((n_rows, n), _F32),
        compiler_params=_cparams("arbitrary"),
        name="adaln",
    )(conds[:, :, None], w_mod, b_mod[None, :])


def _rms(x, gain):
    return x * lax.rsqrt(jnp.mean(x * x, axis=-1, keepdims=True) + RMS_EPS) * gain


def _norm_mod_kernel(x_ref, g_ref, sh_ref, sc_ref, o_ref):
    y = _rms(x_ref[...], g_ref[...])
    o_ref[...] = (y * (1.0 + sc_ref[...]) + sh_ref[...]).astype(o_ref.dtype)


def _norm_mod(x, gain, shift, scale, out_dtype):
    t, d = x.shape
    tm = _tile(t, 256)
    row = pl.BlockSpec((1, d), lambda i: (0, 0))
    blk = pl.BlockSpec((tm, d), lambda i: (i, 0))
    return pl.pallas_call(
        _norm_mod_kernel, grid=(t // tm,), in_specs=[blk, row, row, row], out_specs=blk,
        out_shape=jax.ShapeDtypeStruct((t, d), out_dtype),
        compiler_params=_cparams("parallel"), name="rmsnorm_modulate",
    )(x, gain[None, :], shift[None, :], scale[None, :])


def _mm_kernel(a_ref, w_ref, o_ref):
    o_ref[...] = jnp.dot(a_ref[...], w_ref[...], preferred_element_type=_F32).astype(o_ref.dtype)


def _mm_res_kernel(a_ref, w_ref, res_ref, gate_ref, cs_ref, o_ref):
    y = jnp.dot(a_ref[...], w_ref[...], preferred_element_type=_F32) * cs_ref[...]
    o_ref[...] = res_ref[...] + gate_ref[...] * y


def _matmul(a, w, out_dtype=_F32, *, grouped=False, res=None, gate=None, colscale=None, tm_target=512,
            tn_target=1024):
    m = a.shape[0]
    if grouped:
        g, kk, tn = w.shape
        n = g * tn
        a_spec = lambda tm: pl.BlockSpec((tm, kk), lambda j, i: (i, j))
        w_spec = pl.BlockSpec((None, kk, tn), lambda j, i: (j, 0, 0))
    else:
        kk, n = w.shape
        tn = _tile(n, tn_target, 128)
        a_spec = lambda tm: pl.BlockSpec((tm, kk), lambda j, i: (i, 0))
        w_spec = pl.BlockSpec((kk, tn), lambda j, i: (0, j))
    tm = _tile(m, tm_target)
    o_spec = pl.BlockSpec((tm, tn), lambda j, i: (i, j))
    row = pl.BlockSpec((1, tn), lambda j, i: (0, j))
    grid = (n // tn, m // tm)
    if res is None:
        return pl.pallas_call(
            _mm_kernel, grid=grid, in_specs=[a_spec(tm), w_spec], out_specs=o_spec,
            out_shape=jax.ShapeDtypeStruct((m, n), out_dtype),
            compiler_params=_cparams("parallel", "parallel"), name="matmul",
        )(a, w)
    if colscale is None:
        colscale = jnp.ones((n,), _F32)
    return pl.pallas_call(
        _mm_res_kernel, grid=grid, in_specs=[a_spec(tm), w_spec, o_spec, row, row], out_specs=o_spec,
        out_shape=jax.ShapeDtypeStruct((m, n), _F32),
        compiler_params=_cparams("parallel", "parallel"), name="matmul_gated_residual",
    )(a, w, res, gate[None, :], colscale[None, :])


def _mm2_res_kernel(a1_ref, a2_ref, w1_ref, w2_ref, res_ref, gate_ref, o_ref):
    y = (jnp.dot(a1_ref[...], w1_ref[...], preferred_element_type=_F32)
         + jnp.dot(a2_ref[...], w2_ref[...], preferred_element_type=_F32))
    o_ref[...] = res_ref[...] + gate_ref[...] * y


def _matmul2_res(a1, a2, w, res, gate):
    m, k1 = a1.shape
    assert a2.shape == (m, k1) and w.shape[0] == 2 * k1
    n = w.shape[1]
    tm = _tile(m, 512)
    tn = _tile(n, 1024, 128)
    a_spec = pl.BlockSpec((tm, k1), lambda j, i: (i, 0))
    o_spec = pl.BlockSpec((tm, tn), lambda j, i: (i, j))
    return pl.pallas_call(
        _mm2_res_kernel, grid=(n // tn, m // tm),
        in_specs=[a_spec, a_spec, pl.BlockSpec((k1, tn), lambda j, i: (0, j)),
                  pl.BlockSpec((k1, tn), lambda j, i: (1, j)), o_spec, pl.BlockSpec((1, tn), lambda j, i: (0, j))],
        out_specs=o_spec, out_shape=jax.ShapeDtypeStruct((m, n), _F32),
        compiler_params=_cparams("parallel", "parallel"), name="matmul2_gated_residual",
    )(a1, a2, w, w, res, gate[None, :])


def _mm_cols_kernel(tab_ref, a_ref, w_ref, o_ref):
    del tab_ref
    _mm_kernel(a_ref, w_ref, o_ref)


def _matmul_cols(a, w, col_tiles, tn, out_dtype):
    m, kk = a.shape
    tm = _tile(m, 512)
    n = len(col_tiles) * tn
    return pl.pallas_call(
        _mm_cols_kernel,
        grid_spec=pltpu.PrefetchScalarGridSpec(
            num_scalar_prefetch=1, grid=(len(col_tiles), m // tm),
            in_specs=[pl.BlockSpec((tm, kk), lambda j, i, tab: (i, 0)),
                      pl.BlockSpec((kk, tn), lambda j, i, tab: (0, tab[j]))],
            out_specs=pl.BlockSpec((tm, tn), lambda j, i, tab: (i, j)),
        ),
        out_shape=jax.ShapeDtypeStruct((m, n), out_dtype),
        compiler_params=_cparams("parallel", "parallel"), name="matmul_cols",
    )(jnp.asarray(col_tiles, jnp.int32), a, w)


def _mm_cols_norm_rope_kernel(tab_ref, a_ref, w_ref, gain_ref, post_ref, cos_ref, sin_ref, o_ref, *, rope):
    del tab_ref
    x = jnp.dot(a_ref[...], w_ref[...], preferred_element_type=_F32)
    if rope:
        cos = cos_ref[...]
        sin = sin_ref[...]
        lane = lax.broadcasted_iota(jnp.int32, cos.shape, 1)
        first_half = (lane % (HEAD_DIM // 2)) < (HEAD_DIM // 4)
    for h in range(x.shape[1] // HEAD_DIM):
        sl = slice(h * HEAD_DIM, (h + 1) * HEAD_DIM)
        y = _rms(x[:, sl], gain_ref[:, sl])
        if rope:
            partner = jnp.where(first_half, pltpu.roll(y, HEAD_DIM - HEAD_DIM // 4, 1),
                                pltpu.roll(y, HEAD_DIM // 4, 1))
            y = y * cos + partner * sin
        o_ref[:, sl] = (y * post_ref[:, sl]).astype(o_ref.dtype)


def _matmul_cols_norm_rope(a, w, col_tiles, tn, gains, post_scale, cos, sin_signed):
    m, kk = a.shape
    tm = _tile(m, 512)
    n = len(col_tiles) * tn
    rope = cos is not None
    if not rope:
        cos = jnp.zeros((m, HEAD_DIM), _F32)
        sin_signed = cos
    vec = pl.BlockSpec((1, tn), lambda j, i, tab: (0, j))
    table = pl.BlockSpec((tm, HEAD_DIM), lambda j, i, tab: (i, 0))
    return pl.pallas_call(
        functools.partial(_mm_cols_norm_rope_kernel, rope=rope),
        grid_spec=pltpu.PrefetchScalarGridSpec(
            num_scalar_prefetch=1, grid=(len(col_tiles), m // tm),
            in_specs=[pl.BlockSpec((tm, kk), lambda j, i, tab: (i, 0)),
                      pl.BlockSpec((kk, tn), lambda j, i, tab: (0, tab[j])), vec, vec, table, table],
            out_specs=pl.BlockSpec((tm, tn), lambda j, i, tab: (i, j)),
        ),
        out_shape=jax.ShapeDtypeStruct((m, n), _BF16),
        compiler_params=_cparams("parallel", "parallel"), name="matmul_cols_norm_rope",
    )(jnp.asarray(col_tiles, jnp.int32), a, w, gains[None, :], post_scale[None, :], cos, sin_signed)


def _rope_tables(n):
    axis_dim = HEAD_DIM // 2
    t = jnp.arange(n)
    row = (t // GRID_W).astype(_F32)
    col = (t % GRID_W).astype(_F32)
    inv = ROPE_THETA ** (-jnp.arange(0, axis_dim, 2, dtype=_F32) / axis_dim)
    ang_r = row[:, None] * inv
    ang_c = col[:, None] * inv
    cos = jnp.concatenate([jnp.cos(ang_r)] * 2 + [jnp.cos(ang_c)] * 2, axis=-1)
    sin = jnp.concatenate([-jnp.sin(ang_r), jnp.sin(ang_r), -jnp.sin(ang_c), jnp.sin(ang_c)], axis=-1)
    return cos, sin


def _dot_nt(a, b):
    return lax.dot_general(a, b, (((1,), (1,)), ((), ())), preferred_element_type=_F32)


_GQA_Q_BLOCK = 256
_GQA_KEY_BLOCK = 8320
_MXU_WIDTH = 256


def _gqa_kernel(q_ref, k_ref, v_ref, o_ref, m_ref, acc_ref, *, tk):
    group = q_ref.shape[1] // HEAD_DIM
    m_ref[...] = jnp.full(m_ref.shape, -jnp.inf, _F32)
    acc_ref[...] = jnp.zeros(acc_ref.shape, _F32)

    def body(j, carry):
        start = pl.multiple_of(j * tk, tk)
        for g in range(group):
            s = _dot_nt(q_ref[:, g * HEAD_DIM:(g + 1) * HEAD_DIM], k_ref[pl.ds(start, tk), :])
            m_prev = m_ref[g]
            m_new = jnp.maximum(m_prev, jnp.max(s, axis=-1, keepdims=True))
            p = jnp.exp2(s - m_new)
            acc_ref[g] = jnp.exp2(m_prev - m_new) * acc_ref[g] + jnp.dot(
                p.astype(_BF16), v_ref[pl.ds(start, tk), :], preferred_element_type=_F32)
            m_ref[g] = m_new
        return carry

    lax.fori_loop(0, k_ref.shape[0] // tk, body, 0)
    for g in range(group):
        acc = acc_ref[g]
        o_ref[:, g * HEAD_DIM:(g + 1) * HEAD_DIM] = (acc[:, :HEAD_DIM] / acc[:, HEAD_DIM:HEAD_DIM + 1]).astype(
            o_ref.dtype)


def _gqa_attention(q, k, v, n_kv_heads, n_q_heads):
    l = q.shape[0]
    qw = n_q_heads * HEAD_DIM
    s = k.shape[0]
    gw = qw // n_kv_heads
    tq = _tile(l, _GQA_Q_BLOCK)
    tk = _tile(s, _GQA_KEY_BLOCK, 128)
    group = gw // HEAD_DIM
    ones_col = jnp.zeros((s, n_kv_heads, HEAD_DIM), v.dtype).at[:, :, 0].set(1)
    v_aug = jnp.concatenate([v.reshape(s, n_kv_heads, HEAD_DIM), ones_col], axis=-1).reshape(s, -1)
    return pl.pallas_call(
        functools.partial(_gqa_kernel, tk=tk),
        grid=(n_kv_heads, l // tq),
        in_specs=[
            pl.BlockSpec((tq, gw), lambda h, i: (i, h)),
            pl.BlockSpec((s, HEAD_DIM), lambda h, i: (0, h), pipeline_mode=pl.Buffered(1)),
            pl.BlockSpec((s, 2 * HEAD_DIM), lambda h, i: (0, h), pipeline_mode=pl.Buffered(1)),
        ],
        out_specs=pl.BlockSpec((tq, gw), lambda h, i: (i, h)),
        out_shape=jax.ShapeDtypeStruct((l, qw), _BF16),
        scratch_shapes=[pltpu.VMEM((group, tq, 1), _F32), pltpu.VMEM((group, tq, 2 * HEAD_DIM), _F32)],
        compiler_params=_cparams("arbitrary", "arbitrary"), name="gqa_attention",
    )(q, k, v_aug)


_NA_QROWS = 8
_NA_KROWS = 16
_NA_CHAINS = 4


def _na_patterns():
    first = dict(delta=0, a=[max(i - NA_WIN_H // 2, 0) for i in range(_NA_QROWS)])
    inner = dict(delta=-(NA_WIN_H // 2), a=list(range(_NA_QROWS)))
    last = dict(delta=-(_NA_KROWS - _NA_QROWS),
                a=[min(i + NA_WIN_H // 2, _NA_KROWS - NA_WIN_H) for i in range(_NA_QROWS)])
    return first, inner, last


def _na_kernel(q_ref, k_ref, v_ref, kc_ref, vc_ref, tb_ref, o_ref, bias_head, bias_inner, bias_tail, *, n_blocks,
               scale):
    step = pl.program_id(1)
    n_steps = n_blocks // _NA_CHAINS
    w = GRID_W
    tq = _NA_QROWS * w
    lane_lo = lax.broadcasted_iota(jnp.int32, (w, 2 * w), 1) < w

    def build_bias(bias_ref, pat):
        for i in range(_NA_QROWS):
            for jj in range(_NA_KROWS // 2):
                j0 = 2 * jj
                in0 = 0 <= j0 - pat["a"][i] < NA_WIN_H
                in1 = 0 <= j0 + 1 - pat["a"][i] < NA_WIN_H
                rel0 = pat["delta"] + j0 - i + NA_WIN_H - 1
                if not (in0 or in1):
                    tile = jnp.full((w, 2 * w), _NEG, _F32)
                else:
                    tile = tb_ref[rel0 + 1]
                    if not in1:
                        tile = jnp.where(lane_lo, tile, _NEG)
                    elif not in0:
                        tile = jnp.where(lane_lo, _NEG, tile)
                bias_ref[i * w:(i + 1) * w, j0 * w:(j0 + 2) * w] = tile

    first, inner, last = _na_patterns()
    pl.when(step == 0)(functools.partial(build_bias, bias_head, first))
    pl.when(step == 0)(functools.partial(build_bias, bias_inner, inner))
    pl.when(step == 0)(functools.partial(build_bias, bias_tail, inner))
    pl.when(step == 1)(functools.partial(build_bias, bias_head, inner))
    pl.when(step == n_steps - 1)(functools.partial(build_bias, bias_tail, last))

    kc = kc_ref[...]
    vc = vc_ref[...]
    chain_bias = [bias_head] + [bias_inner] * (_NA_CHAINS - 2) + [bias_tail]
    for half, bias_ref in enumerate(chain_bias):
        b = _NA_CHAINS * step + half
        slab_row = jnp.where(b == 0, 0, jnp.where(b == n_blocks - 1, (n_blocks - 2) * _NA_QROWS,
                                                  b * _NA_QROWS - NA_WIN_H // 2))
        start = pl.multiple_of(slab_row * w, w)
        q = q_ref[half * tq:(half + 1) * tq, :]
        s_loc = _dot_nt(q, k_ref[pl.ds(start, _NA_KROWS * w), :]) * scale + bias_ref[...]
        s_ctx = _dot_nt(q, kc) * scale
        m = jnp.maximum(jnp.max(s_loc, axis=-1, keepdims=True), jnp.max(s_ctx, axis=-1, keepdims=True))
        p_loc = jnp.exp(s_loc - m)
        p_ctx = jnp.exp(s_ctx - m)
        denom = jnp.sum(p_loc, axis=-1, keepdims=True) + jnp.sum(p_ctx, axis=-1, keepdims=True)
        o = (jnp.dot(p_loc.astype(_BF16), v_ref[pl.ds(start, _NA_KROWS * w), :], preferred_element_type=_F32)
             + jnp.dot(p_ctx.astype(_BF16), vc, preferred_element_type=_F32))
        o_ref[half * tq:(half + 1) * tq, :] = (o / denom).astype(o_ref.dtype)


def _na_bias_table(rpb):
    qc = np.arange(GRID_W)
    win_start = np.clip(qc - NA_WIN_W // 2, 0, GRID_W - NA_WIN_W)
    rel = qc[None, :] - qc[:, None]
    in_win = (qc[None, :] >= win_start[:, None]) & (qc[None, :] < win_start[:, None] + NA_WIN_W)
    rel_idx = np.clip(rel + NA_WIN_W - 1, 0, 2 * NA_WIN_W - 2)
    toe = jnp.where(in_win[None, None], rpb[:, :, rel_idx], _NEG).astype(_F32)
    neg = jnp.full_like(toe[:, :1], _NEG)
    lo = jnp.concatenate([neg, toe], axis=1)
    hi = jnp.concatenate([toe, neg], axis=1)
    return jnp.concatenate([lo, hi], axis=-1)


def _na_attention(px, pc, rpb, n_heads, q_head0, k_head0, v_head0, ck_head0, cv_head0):
    l = px.shape[0]
    lc = pc.shape[0]
    rows = l // GRID_W
    n_blocks = rows // _NA_QROWS
    assert rows % (_NA_CHAINS * _NA_QROWS) == 0 and n_blocks >= 3
    tq = _NA_CHAINS * _NA_QROWS * GRID_W
    tb = _na_bias_table(rpb)
    head_blk = lambda n, h0: pl.BlockSpec((n, HEAD_DIM), lambda h, b: (0, h0 + h))
    bias_scratch = pltpu.VMEM((_NA_QROWS * GRID_W, _NA_KROWS * GRID_W), _F32)
    return pl.pallas_call(
        functools.partial(_na_kernel, n_blocks=n_blocks, scale=HEAD_DIM ** -0.5),
        grid=(n_heads, n_blocks // _NA_CHAINS),
        in_specs=[
            pl.BlockSpec((tq, HEAD_DIM), lambda h, b: (b, q_head0 + h)),
            head_blk(l, k_head0), head_blk(l, v_head0), head_blk(lc, ck_head0), head_blk(lc, cv_head0),
            pl.BlockSpec((None, 2 * NA_WIN_H, GRID_W, 2 * GRID_W), lambda h, b: (h, 0, 0, 0)),
        ],
        out_specs=pl.BlockSpec((tq, HEAD_DIM), lambda h, b: (b, h)),
        out_shape=jax.ShapeDtypeStruct((l, n_heads * HEAD_DIM), _BF16),
        scratch_shapes=[bias_scratch, bias_scratch, bias_scratch],
        compiler_params=_cparams("arbitrary", "arbitrary"), name="na_attention",
    )(px, px, px, pc, pc, tb)


_POOL_HALO = 16


def _pool_kernel(prev_ref, cur_ref, next_ref, g_ref, sh_ref, sc_ref, o_ref, *, n_tokens):
    i = pl.program_id(0)
    tm, d = cur_ref.shape
    dg = d // len(POOL_WINDOWS)
    x = jnp.concatenate([prev_ref[...], cur_ref[...], next_ref[...]], axis=0)
    a = _rms(x, g_ref[...]) * (1.0 + sc_ref[...]) + sh_ref[...]
    pos = i * tm - _POOL_HALO + lax.broadcasted_iota(jnp.int32, (tm + 2 * _POOL_HALO, 1), 0)
    a = jnp.where((pos >= 0) & (pos < n_tokens), a, 0.0)
    t = i * tm + lax.broadcasted_iota(jnp.int32, (tm, 1), 0)
    n_ext = tm + 2 * _POOL_HALO
    for g, w in enumerate(POOL_WINDOWS):
        ag = a[:, g * dg:(g + 1) * dg]
        run, span = ag, 1
        while span < w:
            run = run[:n_ext - 2 * span + 1, :] + run[span:n_ext - span + 1, :]
            span *= 2
        tot = run[_POOL_HALO - w // 2:_POOL_HALO - w // 2 + tm, :]
        cnt = jnp.minimum(t + (w - w // 2), n_tokens) - jnp.maximum(t - w // 2, 0)
        diff = tot / cnt.astype(_F32) - ag[_POOL_HALO:_POOL_HALO + tm, :]
        o_ref[:, g * dg:(g + 1) * dg] = diff.astype(o_ref.dtype)


def _pool_diffs(x, gain, shift, scale):
    t, d = x.shape
    assert all(w & (w - 1) == 0 and w // 2 <= _POOL_HALO for w in POOL_WINDOWS)
    tm = _tile(t, 256, _POOL_HALO)
    hb = tm // _POOL_HALO
    n_halo_blocks = t // _POOL_HALO
    row = pl.BlockSpec((1, d), lambda i: (0, 0))
    return pl.pallas_call(
        functools.partial(_pool_kernel, n_tokens=t), grid=(t // tm,),
        in_specs=[
            pl.BlockSpec((_POOL_HALO, d), lambda i: (jnp.maximum(i * hb - 1, 0), 0)),
            pl.BlockSpec((tm, d), lambda i: (i, 0)),
            pl.BlockSpec((_POOL_HALO, d), lambda i: (jnp.minimum((i + 1) * hb, n_halo_blocks - 1), 0)),
            row, row, row,
        ],
        out_specs=pl.BlockSpec((tm, d), lambda i: (i, 0)),
        out_shape=jax.ShapeDtypeStruct((t, d), _BF16),
        compiler_params=_cparams("parallel"), name="pool_diffs",
    )(x, x, x, gain[None, :], shift[None, :], scale[None, :])


def _router_kernel(t_ref, w_ref, b_ref, idx_ref, gate_ref, rank_ref, cnt_ref):
    n_exp = w_ref.shape[0]
    tm = t_ref.shape[0]
    per_group = n_exp // N_EXPERT_GROUPS

    @pl.when(pl.program_id(0) == 0)
    def _():
        cnt_ref[...] = jnp.zeros(cnt_ref.shape, _F32)

    logits = _dot_nt(w_ref[...], t_ref[...])
    scores = 1.0 / (1.0 + jnp.exp(-logits))
    biased = scores + b_ref[...]

    def first_max(vals):
        idx = lax.broadcasted_iota(jnp.int32, vals.shape, 0).astype(_F32)
        m = jnp.max(vals, axis=0, keepdims=True)
        first = jnp.min(jnp.where(vals == m, idx, float(vals.shape[0])), axis=0, keepdims=True)
        return m, idx == first

    def take_top(vals, k):
        hits = []
        for _ in range(k):
            _, hit = first_max(vals)
            hits.append(hit)
            vals = jnp.where(hit, -jnp.inf, vals)
        return hits

    def union(hits):
        sel = jnp.zeros(hits[0].shape, _F32)
        for hit in hits:
            sel = jnp.where(hit, 1.0, sel)
        return sel

    group_scores = []
    for g in range(N_EXPERT_GROUPS):
        v = biased[g * per_group:(g + 1) * per_group, :]
        m1, hit = first_max(v)
        m2 = jnp.max(jnp.where(hit, -jnp.inf, v), axis=0, keepdims=True)
        group_scores.append(m1 + m2)
    gsel = union(take_top(jnp.concatenate(group_scores, axis=0), TOPK_GROUPS))
    keep = jnp.concatenate([jnp.broadcast_to(gsel[g:g + 1, :], (per_group, tm)) for g in range(N_EXPERT_GROUPS)],
                           axis=0)
    hits = take_top(jnp.where(keep > 0.0, biased, -jnp.inf), TOP_K)
    esel = union(hits)
    wsel = esel * scores
    gates = wsel / jnp.sum(wsel, axis=0, keepdims=True) * ROUTED_SCALE

    earlier = (lax.broadcasted_iota(jnp.int32, (tm, tm), 0) < lax.broadcasted_iota(jnp.int32, (tm, tm), 1))
    rank = cnt_ref[...] + jnp.dot(esel.astype(_BF16), jnp.where(earlier, 1.0, 0.0).astype(_BF16),
                                  preferred_element_type=_F32)
    cnt_ref[...] += jnp.sum(esel, axis=1, keepdims=True)

    eidx = lax.broadcasted_iota(jnp.int32, (n_exp, tm), 0).astype(_F32)
    pick = lambda hit, v: jnp.sum(jnp.where(hit, v, 0.0), axis=0, keepdims=True)
    idx_ref[...] = jnp.concatenate([pick(h, eidx) for h in hits], axis=0).astype(jnp.int32)
    gate_ref[...] = jnp.concatenate([pick(h, gates) for h in hits], axis=0)
    rank_ref[...] = jnp.concatenate([pick(h, rank) for h in hits], axis=0).astype(jnp.int32)


def _router(t, w_router, router_bias):
    n_tok, d = t.shape
    n_exp = w_router.shape[1]
    tm = _tile(n_tok, 512, 128)
    per_tok = pl.BlockSpec((TOP_K, tm), lambda i: (0, i))
    return pl.pallas_call(
        _router_kernel, grid=(n_tok // tm,),
        in_specs=[
            pl.BlockSpec((tm, d), lambda i: (i, 0)),
            pl.BlockSpec((n_exp, d), lambda i: (0, 0)),
            pl.BlockSpec((n_exp, 1), lambda i: (0, 0)),
        ],
        out_specs=[per_tok, per_tok, per_tok, pl.BlockSpec((n_exp, 1), lambda i: (0, 0))],
        out_shape=[jax.ShapeDtypeStruct((TOP_K, n_tok), jnp.int32), jax.ShapeDtypeStruct((TOP_K, n_tok), _F32),
                   jax.ShapeDtypeStruct((TOP_K, n_tok), jnp.int32), jax.ShapeDtypeStruct((n_exp, 1), _F32)],
        compiler_params=_cparams("arbitrary"), name="moe_router",
    )(t, w_router.T.astype(_BF16), router_bias[:, None])


_MOE_TILE_ROWS = 256


def _packed_cols(d):
    return d // 2


def _pack_pairs(y):
    half = y.shape[1] // 2
    bits = lambda v: lax.bitcast_convert_type(v.astype(_BF16).astype(_F32), jnp.uint32)
    return (bits(y[:, :half]) & jnp.uint32(0xFFFF0000)) | (bits(y[:, half:]) >> jnp.uint32(16))


def _unpack_pairs(w):
    hi = lax.bitcast_convert_type(w & jnp.uint32(0xFFFF0000), _F32)
    lo = lax.bitcast_convert_type(w << jnp.uint32(16), _F32)
    return hi, lo


def _norm_mod_pack_kernel(x_ref, g_ref, sh_ref, sc_ref, o_ref, p_ref):
    y = _rms(x_ref[...], g_ref[...]) * (1.0 + sc_ref[...]) + sh_ref[...]
    o_ref[...] = y.astype(o_ref.dtype)
    p_ref[...] = _pack_pairs(y)


def _norm_mod_pack(x, gain, shift, scale):
    t, d = x.shape
    tm = _tile(t, 256)
    row = pl.BlockSpec((1, d), lambda i: (0, 0))
    blk = pl.BlockSpec((tm, d), lambda i: (i, 0))
    return pl.pallas_call(
        _norm_mod_pack_kernel, grid=(t // tm,), in_specs=[blk, row, row, row],
        out_specs=[blk, pl.BlockSpec((tm, _packed_cols(d)), lambda i: (i, 0))],
        out_shape=[jax.ShapeDtypeStruct((t, d), _BF16), jax.ShapeDtypeStruct((t, _packed_cols(d)), jnp.uint32)],
        compiler_params=_cparams("parallel"), name="rmsnorm_modulate_pack",
    )(x, gain[None, :], shift[None, :], scale[None, :])


def _moe_plan(idx8, rank8, counts, tile_rows):
    n_exp = counts.shape[0]
    n_tok = idx8.shape[1]
    cnt = counts[:, 0].astype(jnp.int32)
    padded = (cnt + tile_rows - 1) // tile_rows * tile_rows
    ends = jnp.cumsum(padded)
    starts = ends - padded
    onehot = idx8[:, :, None] == jnp.arange(n_exp, dtype=jnp.int32)
    pos8 = rank8 + jnp.sum(jnp.where(onehot, starts, 0), axis=-1)
    n_tiles = TOP_K * n_tok // tile_rows + n_exp
    tile_row0 = jnp.arange(n_tiles, dtype=jnp.int32) * tile_rows
    tile_expert = jnp.minimum(jnp.sum(tile_row0[:, None] >= ends[None, :], axis=1), n_exp - 1).astype(jnp.int32)
    n_active = (ends[-1:] // tile_rows).astype(jnp.int32)
    last_tile_row = jnp.where(padded > 0, ends - tile_rows, -1).astype(jnp.int32)
    return pos8.T.reshape(-1), tile_expert, n_active, last_tile_row, n_tiles


def _dispatch_kernel(pos_ref, last_ref, na_ref, tp_ref, xs_ref, zero_ref, sem, zsem, *, tile_rows):
    i = pl.program_id(0)
    tb = tp_ref.shape[0]
    n_exp = last_ref.shape[0]
    n_tiles = xs_ref.shape[0] // tile_rows

    @pl.when(i == 0)
    def _():
        zero_ref[...] = jnp.zeros(zero_ref.shape, zero_ref.dtype)

        def fill(row0):
            return pltpu.make_async_copy(zero_ref, xs_ref.at[pl.ds(pl.multiple_of(row0, tile_rows), tile_rows)],
                                         zsem)

        def each(e, start):
            tail = na_ref[0] + e

            def go(row0):
                cp = fill(row0)
                cp.start() if start else cp.wait()

            pl.when(last_ref[e] >= 0)(lambda: go(jnp.maximum(last_ref[e], 0)))
            pl.when(tail < n_tiles)(lambda: go(jnp.minimum(tail, n_tiles - 1) * tile_rows))

        lax.fori_loop(0, n_exp, lambda e, c: (each(e, True), c)[1], 0)
        lax.fori_loop(0, n_exp, lambda e, c: (each(e, False), c)[1], 0)

    def row(r, c):
        base = (i * tb + r) * TOP_K
        for k in range(TOP_K):
            pltpu.make_async_copy(tp_ref.at[pl.ds(r, 1)], xs_ref.at[pl.ds(pos_ref[base + k], 1)], sem).start()
        return c

    lax.fori_loop(0, tb, row, 0, unroll=2)
    for _ in range(TOP_K):
        pltpu.make_async_copy(tp_ref, xs_ref.at[pl.ds(0, tb)], sem).wait()


def _dispatch(tp, pos_flat, last_tile_row, n_active, n_rows, tile_rows):
    n_tok, c = tp.shape
    tb = _tile(n_tok, 256)
    return pl.pallas_call(
        functools.partial(_dispatch_kernel, tile_rows=tile_rows),
        grid_spec=pltpu.PrefetchScalarGridSpec(
            num_scalar_prefetch=3, grid=(n_tok // tb,),
            in_specs=[pl.BlockSpec((tb, c), lambda i, pos, last, na: (i, 0))],
            out_specs=pl.BlockSpec(memory_space=pl.ANY),
            scratch_shapes=[pltpu.VMEM((tile_rows, c), jnp.uint32), pltpu.SemaphoreType.DMA,
                            pltpu.SemaphoreType.DMA],
        ),
        out_shape=jax.ShapeDtypeStruct((n_rows, c), jnp.uint32),
        compiler_params=_cparams("arbitrary"), name="moe_dispatch",
    )(pos_flat, last_tile_row, n_active, tp)


def _grouped_ffn_kernel(te_ref, na_ref, xs_ref, wg_ref, wu_ref, wd_ref, ys_ref, wg_bf, wu_bf, wd_bf):
    i = pl.program_id(0)
    active = i < na_ref[0]
    new_expert = jnp.logical_or(i == 0, te_ref[i] != te_ref[jnp.maximum(i - 1, 0)])

    @pl.when(jnp.logical_and(active, new_expert))
    def _():
        wg_bf[...] = wg_ref[...].astype(_BF16)
        wu_bf[...] = wu_ref[...].astype(_BF16)
        wd_bf[...] = wd_ref[...].astype(_BF16)

    @pl.when(active)
    def _():
        half = wg_bf.shape[0] // 2
        hi, lo = _unpack_pairs(xs_ref[...])
        hi = hi.astype(_BF16)
        lo = lo.astype(_BF16)
        dot = lambda a, b: jnp.dot(a, b, preferred_element_type=_F32)
        hg = dot(hi, wg_bf[:half, :]) + dot(lo, wg_bf[half:, :])
        hu = dot(hi, wu_bf[:half, :]) + dot(lo, wu_bf[half:, :])
        h = (_silu(hg) * hu).astype(_BF16)
        ys_ref[...] = _pack_pairs(dot(h, wd_bf[...]))

    @pl.when(jnp.logical_not(active))
    def _():
        ys_ref[...] = jnp.zeros(ys_ref.shape, ys_ref.dtype)


def _grouped_ffn(xs, tile_expert, n_active, w_gate, w_up, w_down, layer, tile_rows):
    n_rows, c = xs.shape
    _, n_exp, d, ff = w_gate.shape
    n_tiles = n_rows // tile_rows
    row_blk = pl.BlockSpec((tile_rows, c), lambda i, te, na: (jnp.minimum(i, jnp.maximum(na[0] - 1, 0)), 0))
    return pl.pallas_call(
        _grouped_ffn_kernel,
        grid_spec=pltpu.PrefetchScalarGridSpec(
            num_scalar_prefetch=2, grid=(n_tiles,),
            in_specs=[row_blk,
                      pl.BlockSpec((None, None, d, ff), lambda i, te, na: (layer, te[i], 0, 0)),
                      pl.BlockSpec((None, None, d, ff), lambda i, te, na: (layer, te[i], 0, 0)),
                      pl.BlockSpec((None, None, ff, d), lambda i, te, na: (layer, te[i], 0, 0))],
            out_specs=pl.BlockSpec((tile_rows, c), lambda i, te, na: (i, 0)),
            scratch_shapes=[pltpu.VMEM((d, ff), _BF16), pltpu.VMEM((d, ff), _BF16), pltpu.VMEM((ff, d), _BF16)],
        ),
        out_shape=jax.ShapeDtypeStruct((n_rows, c), jnp.uint32),
        compiler_params=_cparams("arbitrary"), name="moe_grouped_ffn",
    )(tile_expert, n_active, xs, w_gate, w_up, w_down)


def _combine_kernel(pos_ref, ys_ref, g8_ref, x_ref, sh_ref, g2_ref, fg_ref, o_ref, buf, sems, *, final_norm):
    i = pl.program_id(0)
    tb, d = x_ref.shape
    half = d // 2

    def gather(block, slot):
        def row(r, c):
            base = (block * tb + r) * TOP_K
            for k in range(TOP_K):
                pltpu.make_async_copy(ys_ref.at[pl.ds(pos_ref[base + k], 1)], buf.at[slot, k, pl.ds(r, 1)],
                                      sems.at[slot]).start()
            return c

        lax.fori_loop(0, tb, row, 0, unroll=2)

    @pl.when(i == 0)
    def _():
        gather(0, 0)

    @pl.when(i + 1 < pl.num_programs(0))
    def _():
        gather(i + 1, lax.rem(i + 1, 2))

    slot = lax.rem(i, 2)
    for k in range(TOP_K):
        pltpu.make_async_copy(ys_ref.at[pl.ds(0, tb)], buf.at[slot, k], sems.at[slot]).wait()

    g8 = g8_ref[...]
    acc_hi = jnp.zeros((tb, half), _F32)
    acc_lo = jnp.zeros((tb, half), _F32)
    for k in range(TOP_K):
        hi, lo = _unpack_pairs(buf[slot, k])
        acc_hi = acc_hi + g8[:, k:k + 1] * hi
        acc_lo = acc_lo + g8[:, k:k + 1] * lo
    out_hi = x_ref[:, :half] + g2_ref[:, :half] * (sh_ref[:, :half] + acc_hi)
    out_lo = x_ref[:, half:] + g2_ref[:, half:] * (sh_ref[:, half:] + acc_lo)
    if final_norm:
        ssq = jnp.sum(out_hi * out_hi, axis=-1, keepdims=True) + jnp.sum(out_lo * out_lo, axis=-1, keepdims=True)
        inv = lax.rsqrt(ssq / d + RMS_EPS)
        out_hi = out_hi * inv * fg_ref[:, :half]
        out_lo = out_lo * inv * fg_ref[:, half:]
    o_ref[:, :half] = out_hi
    o_ref[:, half:] = out_lo


def _combine(ys, pos_flat, gate8, x, shared, gate2, final_gain):
    n_tok, d = x.shape
    c = ys.shape[1]
    tb = _tile(n_tok, 128)
    blk = pl.BlockSpec((tb, d), lambda i, pos: (i, 0))
    vec = pl.BlockSpec((1, d), lambda i, pos: (0, 0))
    final_norm = final_gain is not None
    if not final_norm:
        final_gain = jnp.ones((d,), _F32)
    return pl.pallas_call(
        functools.partial(_combine_kernel, final_norm=final_norm),
        grid_spec=pltpu.PrefetchScalarGridSpec(
            num_scalar_prefetch=1, grid=(n_tok // tb,),
            in_specs=[pl.BlockSpec(memory_space=pl.ANY),
                      pl.BlockSpec((tb, TOP_K), lambda i, pos: (i, 0)),
                      blk, blk, vec, vec],
            out_specs=blk,
            scratch_shapes=[pltpu.VMEM((2, TOP_K, tb, c), jnp.uint32), pltpu.SemaphoreType.DMA((2,))],
        ),
        out_shape=jax.ShapeDtypeStruct((n_tok, d), _F32),
        compiler_params=_cparams("arbitrary"), name="moe_combine",
    )(pos_flat, ys, gate8, x, shared, gate2[None, :], final_gain[None, :])


def _ffn_kernel(t_ref, wg_ref, wu_ref, wd_ref, o_ref):
    e = pl.program_id(1)
    t = t_ref[...]
    h = _silu(jnp.dot(t, wg_ref[...], preferred_element_type=_F32)) * jnp.dot(t, wu_ref[...],
                                                                               preferred_element_type=_F32)
    y = jnp.dot(h.astype(_BF16), wd_ref[...], preferred_element_type=_F32)

    @pl.when(e == 0)
    def _():
        o_ref[...] = y

    @pl.when(e > 0)
    def _():
        o_ref[...] += y


def _ffn(t, w_gate, w_up, w_down):
    n_tok, d = t.shape
    n_chunks, _, ff = w_gate.shape
    tm = _tile(n_tok, 512)
    blk = pl.BlockSpec((tm, d), lambda i, e: (i, 0))
    return pl.pallas_call(
        _ffn_kernel, grid=(n_tok // tm, n_chunks),
        in_specs=[blk,
                  pl.BlockSpec((None, d, ff), lambda i, e: (e, 0, 0)),
                  pl.BlockSpec((None, d, ff), lambda i, e: (e, 0, 0)),
                  pl.BlockSpec((None, ff, d), lambda i, e: (e, 0, 0))],
        out_specs=blk,
        out_shape=jax.ShapeDtypeStruct((n_tok, d), _F32),
        compiler_params=_cparams("parallel", "arbitrary"), name="shared_ffn",
    )(t, w_gate, w_up, w_down)


def _moe_residual(x, gain, shift, scale, gate2, w_router, router_bias, w_gate, w_up, w_down, layer, ws_gate, ws_up,
                  ws_down, final_gain):
    d = x.shape[1]
    t, tp = _norm_mod_pack(x, gain, shift, scale)
    idx8, gate8, rank8, counts = _router(t, w_router, router_bias)
    pos_flat, tile_expert, n_active, last_tile_row, n_tiles = _moe_plan(idx8, rank8, counts, _MOE_TILE_ROWS)
    xs = _dispatch(tp, pos_flat, last_tile_row, n_active, n_tiles * _MOE_TILE_ROWS, _MOE_TILE_ROWS)
    ys = _grouped_ffn(xs, tile_expert, n_active, w_gate, w_up, w_down, layer, _MOE_TILE_ROWS)
    sff = ws_gate.shape[1]
    ff = w_gate.shape[3]
    n_sh = sff // ff
    to_chunks = lambda w: w.reshape(d, n_sh, ff).transpose(1, 0, 2).astype(_BF16)
    shared = _ffn(t, to_chunks(ws_gate), to_chunks(ws_up), ws_down.reshape(n_sh, ff, d).astype(_BF16))
    return _combine(ys, pos_flat, gate8.T, x, shared, gate2, final_gain)


def _attention_layer(x, h_c, mods, mods_c, g_mix, w_in, w_out, rpb, q_gain, k_gain):
    shift1, scale1, gate1 = mods[:3]
    cshift1, cscale1 = mods_c[:2]
    n_tok = x.shape[0]
    na_w = NA_HEADS * HEAD_DIM
    gq_w = GQA_Q_HEADS * HEAD_DIM
    gkv_w = GQA_KV_HEADS * HEAD_DIM
    q_cols = na_w + gq_w
    o_nk, o_gk, o_gv, o_end = q_cols, q_cols + 2 * na_w, q_cols + 2 * na_w + gkv_w, w_in.shape[1]
    tn = int(np.gcd.reduce([na_w, gq_w, gkv_w, 512]))
    tiles = lambda lo, hi: list(range(lo // tn, hi // tn))
    w_bf = w_in.astype(_BF16)

    a_x = _norm_mod(x, g_mix, shift1, scale1, _BF16)
    a_c = _norm_mod(h_c, g_mix, cshift1, cscale1, _BF16)

    kv_plain = tiles(o_nk, o_gk) + tiles(o_gv, o_end)
    p_plain = _matmul_cols(a_x, w_bf, tiles(0, na_w) + kv_plain, tn, _BF16)
    c_plain = _matmul_cols(a_c, w_bf, kv_plain, tn, _BF16)

    cos, sin = _rope_tables(n_tok)
    gains = jnp.concatenate([jnp.tile(q_gain, GQA_Q_HEADS), jnp.tile(k_gain, GQA_KV_HEADS)])
    post = jnp.concatenate([jnp.full((gq_w,), HEAD_DIM ** -0.5 * np.log2(np.e), _F32), jnp.ones((gkv_w,), _F32)])
    qk = _matmul_cols_norm_rope(a_x, w_bf, tiles(na_w, q_cols) + tiles(o_gk, o_gv), tn, gains, post, cos,
                                sin)
    ck = _matmul_cols_norm_rope(a_c, w_bf, tiles(o_gk, o_gv), tn, gains[gq_w:], post[gq_w:], None, None)

    o_a = _na_attention(p_plain, c_plain, rpb, NA_HEADS, 0, NA_HEADS, 2 * NA_HEADS, 0, NA_HEADS)
    k_all = jnp.concatenate([qk[:, gq_w:], ck], axis=0)
    v_all = jnp.concatenate([p_plain[:, 3 * na_w:], c_plain[:, 2 * na_w:]], axis=0)
    o_b = _gqa_attention(qk, k_all, v_all, GQA_KV_HEADS, GQA_Q_HEADS)
    return _matmul2_res(o_a, o_b, w_out.astype(_BF16), x, gate1)


def _pool_layer(x, mods, g_mix, w_pool, pool_scale):
    shift1, scale1, gate1 = mods[:3]
    diffs = _pool_diffs(x, g_mix, shift1, scale1)
    return _matmul(diffs, w_pool.astype(_BF16), grouped=True, res=x, gate=gate1, colscale=pool_scale)


def kernel(x, c, ctx, c_ctx, w_mod, b_mod, g_mix, g_ffn, attn_w_in, attn_w_out, na_rpb, q_gain, k_gain, pool_w,
           pool_scale, moe_w_router, moe_router_bias, moe_w_gate, moe_w_up, moe_w_down, moe_ws_gate, moe_ws_up,
           moe_ws_down, g_final):
    batch, seq, d = x.shape
    depth = w_mod.shape[0]
    outs = []
    for bi in range(batch):
        xb = x[bi]
        h_c = ctx[bi]
        conds = jnp.stack([c[bi], c_ctx])
        for i in range(depth):
            j = i // 2
            mod_rows = _adaln(conds, w_mod, i, b_mod[i])
            mods = jnp.split(mod_rows[0], N_MOD)
            mods_c = jnp.split(mod_rows[1], N_MOD)
            if i % 2 == 0:
                xb = _attention_layer(xb, h_c, mods, mods_c, g_mix[i], attn_w_in[j], attn_w_out[j], na_rpb[j],
                                      q_gain[j], k_gain[j])
            else:
                xb = _pool_layer(xb, mods, g_mix[i], pool_w[j], pool_scale[j])
            assert not any(l % 2 == 0 for l in range(i + 1, depth)), "context-stream update not implemented"
            xb = _moe_residual(xb, g_ffn[i], mods[3], mods[4], mods[5], moe_w_router[i], moe_router_bias[i],
                               moe_w_gate, moe_w_up, moe_w_down, i, moe_ws_gate[i], moe_ws_up[i], moe_ws_down[i],
                               g_final if i == depth - 1 else None)
        outs.append(xb)
    return jnp.stack(outs)
```

```python
import functools

import numpy as np
import jax
import jax.numpy as jnp
from jax import lax
from jax.experimental import pallas as pl
from jax.experimental.pallas import tpu as pltpu

GRID_W = 64
HEAD_DIM = 128
NA_HEADS = 16
GQA_Q_HEADS = 16
GQA_KV_HEADS = 4
NA_WIN_H = 8
NA_WIN_W = 16
ROPE_THETA = 10000.0
POOL_WINDOWS = (2, 4, 8, 16)
N_EXPERT_GROUPS = 8
TOPK_GROUPS = 4
TOP_K = 8
ROUTED_SCALE = 2.5
N_MOD = 6
RMS_EPS = 1e-6

_F32 = jnp.float32
_BF16 = jnp.bfloat16
_NEG = -1e30
_VMEM_LIMIT_BYTES = 56 * 1024 * 1024


def _cparams(*sem):
    return pltpu.CompilerParams(dimension_semantics=sem, vmem_limit_bytes=_VMEM_LIMIT_BYTES)


def _tile(n, target, mult=8):
    t = min(n, target) // mult * mult
    while t > mult and n % t:
        t -= mult
    assert t > 0 and n % t == 0, (n, target, mult)
    return t


def _silu(x):
    return x / (1.0 + jnp.exp(-x))


def _adaln_kernel(c_ref, w_ref, b_ref, o_ref, *, kc):
    n_rows, d, _ = c_ref.shape
    bn = w_ref.shape[1]
    accs = [jnp.zeros((1, bn), _F32) for _ in range(n_rows)]
    for k0 in range(0, d, kc):
        w = w_ref[k0:k0 + kc, :]
        for r in range(n_rows):
            cc = _silu(c_ref[r, k0:k0 + kc, :])
            accs[r] = accs[r] + jnp.sum(cc * w, axis=0, keepdims=True)
    o_ref[...] = jnp.concatenate(accs, axis=0) + b_ref[...]


def _adaln(conds, w_mod, layer, b_mod):
    n_rows, d = conds.shape
    n = w_mod.shape[2]
    bn = _tile(n, 512, 128)
    return pl.pallas_call(
        functools.partial(_adaln_kernel, kc=_tile(d, 256)),
        grid=(n // bn,),
        in_specs=[
            pl.BlockSpec((n_rows, d, 1), lambda j: (0, 0, 0)),
            pl.BlockSpec((None, d, bn), lambda j: (layer, 0, j)),
            pl.BlockSpec((1, bn), lambda j: (0, j)),
        ],
        out_specs=pl.BlockSpec((n_rows, bn), lambda j: (0, j)),
        out_shape=jax.ShapeDtypeStruct((n_rows, n), _F32),
        compiler_params=_cparams("arbitrary"),
        name="adaln",
    )(conds[:, :, None], w_mod, b_mod[None, :])


def _rms(x, gain):
    return x * lax.rsqrt(jnp.mean(x * x, axis=-1, keepdims=True) + RMS_EPS) * gain


def _norm_mod_kernel(x_ref, g_ref, sh_ref, sc_ref, o_ref):
    y = _rms(x_ref[...], g_ref[...])
    o_ref[...] = (y * (1.0 + sc_ref[...]) + sh_ref[...]).astype(o_ref.dtype)


def _norm_mod(x, gain, shift, scale, out_dtype):
    t, d = x.shape
    tm = _tile(t, 256)
    row = pl.BlockSpec((1, d), lambda i: (0, 0))
    blk = pl.BlockSpec((tm, d), lambda i: (i, 0))
    return pl.pallas_call(
        _norm_mod_kernel, grid=(t // tm,), in_specs=[blk, row, row, row], out_specs=blk,
        out_shape=jax.ShapeDtypeStruct((t, d), out_dtype),
        compiler_params=_cparams("parallel"), name="rmsnorm_modulate",
    )(x, gain[None, :], shift[None, :], scale[None, :])


def _mm_kernel(a_ref, w_ref, o_ref):
    o_ref[...] = jnp.dot(a_ref[...], w_ref[...], preferred_element_type=_F32).astype(o_ref.dtype)


def _mm_res_kernel(a_ref, w_ref, res_ref, gate_ref, cs_ref, o_ref):
    y = jnp.dot(a_ref[...], w_ref[...], preferred_element_type=_F32) * cs_ref[...]
    o_ref[...] = res_ref[...] + gate_ref[...] * y


def _matmul(a, w, out_dtype=_F32, *, grouped=False, res=None, gate=None, colscale=None, tm_target=512,
            tn_target=1024):
    m = a.shape[0]
    if grouped:
        g, kk, tn = w.shape
        n = g * tn
        a_spec = lambda tm: pl.BlockSpec((tm, kk), lambda j, i: (i, j))
        w_spec = pl.BlockSpec((None, kk, tn), lambda j, i: (j, 0, 0))
    else:
        kk, n = w.shape
        tn = _tile(n, tn_target, 128)
        a_spec = lambda tm: pl.BlockSpec((tm, kk), lambda j, i: (i, 0))
        w_spec = pl.BlockSpec((kk, tn), lambda j, i: (0, j))
    tm = _tile(m, tm_target)
    o_spec = pl.BlockSpec((tm, tn), lambda j, i: (i, j))
    row = pl.BlockSpec((1, tn), lambda j, i: (0, j))
    grid = (n // tn, m // tm)
    if res is None:
        return pl.pallas_call(
            _mm_kernel, grid=grid, in_specs=[a_spec(tm), w_spec], out_specs=o_spec,
            out_shape=jax.ShapeDtypeStruct((m, n), out_dtype),
            compiler_params=_cparams("parallel", "parallel"), name="matmul",
        )(a, w)
    if colscale is None:
        colscale = jnp.ones((n,), _F32)
    return pl.pallas_call(
        _mm_res_kernel, grid=grid, in_specs=[a_spec(tm), w_spec, o_spec, row, row], out_specs=o_spec,
        out_shape=jax.ShapeDtypeStruct((m, n), _F32),
        compiler_params=_cparams("parallel", "parallel"), name="matmul_gated_residual",
    )(a, w, res, gate[None, :], colscale[None, :])


def _mm2_res_kernel(a1_ref, a2_ref, w1_ref, w2_ref, res_ref, gate_ref, o_ref):
    y = (jnp.dot(a1_ref[...], w1_ref[...], preferred_element_type=_F32)
         + jnp.dot(a2_ref[...], w2_ref[...], preferred_element_type=_F32))
    o_ref[...] = res_ref[...] + gate_ref[...] * y


def _matmul2_res(a1, a2, w, res, gate):
    m, k1 = a1.shape
    assert a2.shape == (m, k1) and w.shape[0] == 2 * k1
    n = w.shape[1]
    tm = _tile(m, 512)
    tn = _tile(n, 1024, 128)
    a_spec = pl.BlockSpec((tm, k1), lambda j, i: (i, 0))
    o_spec = pl.BlockSpec((tm, tn), lambda j, i: (i, j))
    return pl.pallas_call(
        _mm2_res_kernel, grid=(n // tn, m // tm),
        in_specs=[a_spec, a_spec, pl.BlockSpec((k1, tn), lambda j, i: (0, j)),
                  pl.BlockSpec((k1, tn), lambda j, i: (1, j)), o_spec, pl.BlockSpec((1, tn), lambda j, i: (0, j))],
        out_specs=o_spec, out_shape=jax.ShapeDtypeStruct((m, n), _F32),
        compiler_params=_cparams("parallel", "parallel"), name="matmul2_gated_residual",
    )(a1, a2, w, w, res, gate[None, :])


def _mm_cols_kernel(tab_ref, a_ref, w_ref, o_ref):
    del tab_ref
    _mm_kernel(a_ref, w_ref, o_ref)


def _matmul_cols(a, w, col_tiles, tn, out_dtype):
    m, kk = a.shape
    tm = _tile(m, 512)
    n = len(col_tiles) * tn
    return pl.pallas_call(
        _mm_cols_kernel,
        grid_spec=pltpu.PrefetchScalarGridSpec(
            num_scalar_prefetch=1, grid=(len(col_tiles), m // tm),
            in_specs=[pl.BlockSpec((tm, kk), lambda j, i, tab: (i, 0)),
                      pl.BlockSpec((kk, tn), lambda j, i, tab: (0, tab[j]))],
            out_specs=pl.BlockSpec((tm, tn), lambda j, i, tab: (i, j)),
        ),
        out_shape=jax.ShapeDtypeStruct((m, n), out_dtype),
        compiler_params=_cparams("parallel", "parallel"), name="matmul_cols",
    )(jnp.asarray(col_tiles, jnp.int32), a, w)


def _mm_cols_norm_rope_kernel(tab_ref, a_ref, w_ref, gain_ref, post_ref, cos_ref, sin_ref, o_ref, *, rope):
    del tab_ref
    x = jnp.dot(a_ref[...], w_ref[...], preferred_element_type=_F32)
    if rope:
        cos = cos_ref[...]
        sin = sin_ref[...]
        lane = lax.broadcasted_iota(jnp.int32, cos.shape, 1)
        first_half = (lane % (HEAD_DIM // 2)) < (HEAD_DIM // 4)
    for h in range(x.shape[1] // HEAD_DIM):
        sl = slice(h * HEAD_DIM, (h + 1) * HEAD_DIM)
        y = _rms(x[:, sl], gain_ref[:, sl])
        if rope:
            partner = jnp.where(first_half, pltpu.roll(y, HEAD_DIM - HEAD_DIM // 4, 1),
                                pltpu.roll(y, HEAD_DIM // 4, 1))
            y = y * cos + partner * sin
        o_ref[:, sl] = (y * post_ref[:, sl]).astype(o_ref.dtype)


def _matmul_cols_norm_rope(a, w, col_tiles, tn, gains, post_scale, cos, sin_signed):
    m, kk = a.shape
    tm = _tile(m, 512)
    n = len(col_tiles) * tn
    rope = cos is not None
    if not rope:
        cos = jnp.zeros((m, HEAD_DIM), _F32)
        sin_signed = cos
    vec = pl.BlockSpec((1, tn), lambda j, i, tab: (0, j))
    table = pl.BlockSpec((tm, HEAD_DIM), lambda j, i, tab: (i, 0))
    return pl.pallas_call(
        functools.partial(_mm_cols_norm_rope_kernel, rope=rope),
        grid_spec=pltpu.PrefetchScalarGridSpec(
            num_scalar_prefetch=1, grid=(len(col_tiles), m // tm),
            in_specs=[pl.BlockSpec((tm, kk), lambda j, i, tab: (i, 0)),
                      pl.BlockSpec((kk, tn), lambda j, i, tab: (0, tab[j])), vec, vec, table, table],
            out_specs=pl.BlockSpec((tm, tn), lambda j, i, tab: (i, j)),
        ),
        out_shape=jax.ShapeDtypeStruct((m, n), _BF16),
        compiler_params=_cparams("parallel", "parallel"), name="matmul_cols_norm_rope",
    )(jnp.asarray(col_tiles, jnp.int32), a, w, gains[None, :], post_scale[None, :], cos, sin_signed)


def _rope_tables(n):
    axis_dim = HEAD_DIM // 2
    t = jnp.arange(n)
    row = (t // GRID_W).astype(_F32)
    col = (t % GRID_W).astype(_F32)
    inv = ROPE_THETA ** (-jnp.arange(0, axis_dim, 2, dtype=_F32) / axis_dim)
    ang_r = row[:, None] * inv
    ang_c = col[:, None] * inv
    cos = jnp.concatenate([jnp.cos(ang_r)] * 2 + [jnp.cos(ang_c)] * 2, axis=-1)
    sin = jnp.concatenate([-jnp.sin(ang_r), jnp.sin(ang_r), -jnp.sin(ang_c), jnp.sin(ang_c)], axis=-1)
    return cos, sin


def _dot_nt(a, b):
    return lax.dot_general(a, b, (((1,), (1,)), ((), ())), preferred_element_type=_F32)


_GQA_Q_BLOCK = 256
_GQA_KEY_BLOCK = 8320


def _gqa_kernel(q_ref, k_ref, v_ref, o_ref, m_ref, acc_ref, *, tk):
    group = q_ref.shape[1] // HEAD_DIM
    m_ref[...] = jnp.full(m_ref.shape, -jnp.inf, _F32)
    acc_ref[...] = jnp.zeros(acc_ref.shape, _F32)

    def body(j, carry):
        start = pl.multiple_of(j * tk, tk)
        for g in range(group):
            s = _dot_nt(q_ref[:, g * HEAD_DIM:(g + 1) * HEAD_DIM], k_ref[pl.ds(start, tk), :])
            m_prev = m_ref[g]
            m_new = jnp.maximum(m_prev, jnp.max(s, axis=-1, keepdims=True))
            p = jnp.exp2(s - m_new)
            acc_ref[g] = jnp.exp2(m_prev - m_new) * acc_ref[g] + jnp.dot(
                p.astype(_BF16), v_ref[pl.ds(start, tk), :], preferred_element_type=_F32)
            m_ref[g] = m_new
        return carry

    lax.fori_loop(0, k_ref.shape[0] // tk, body, 0)
    for g in range(group):
        acc = acc_ref[g]
        o_ref[:, g * HEAD_DIM:(g + 1) * HEAD_DIM] = (acc[:, :HEAD_DIM] / acc[:, HEAD_DIM:HEAD_DIM + 1]).astype(
            o_ref.dtype)


def _gqa_attention(q, k, v, n_kv_heads, n_q_heads):
    l = q.shape[0]
    qw = n_q_heads * HEAD_DIM
    s = k.shape[0]
    gw = qw // n_kv_heads
    tq = _tile(l, _GQA_Q_BLOCK)
    tk = _tile(s, _GQA_KEY_BLOCK, 128)
    group = gw // HEAD_DIM
    ones_col = jnp.zeros((s, n_kv_heads, HEAD_DIM), v.dtype).at[:, :, 0].set(1)
    v_aug = jnp.concatenate([v.reshape(s, n_kv_heads, HEAD_DIM), ones_col], axis=-1).reshape(s, -1)
    return pl.pallas_call(
        functools.partial(_gqa_kernel, tk=tk),
        grid=(n_kv_heads, l // tq),
        in_specs=[
            pl.BlockSpec((tq, gw), lambda h, i: (i, h)),
            pl.BlockSpec((s, HEAD_DIM), lambda h, i: (0, h), pipeline_mode=pl.Buffered(1)),
            pl.BlockSpec((s, 2 * HEAD_DIM), lambda h, i: (0, h), pipeline_mode=pl.Buffered(1)),
        ],
        out_specs=pl.BlockSpec((tq, gw), lambda h, i: (i, h)),
        out_shape=jax.ShapeDtypeStruct((l, qw), _BF16),
        scratch_shapes=[pltpu.VMEM((group, tq, 1), _F32), pltpu.VMEM((group, tq, 2 * HEAD_DIM), _F32)],
        compiler_params=_cparams("arbitrary", "arbitrary"), name="gqa_attention",
    )(q, k, v_aug)


_NA_QROWS = 8
_NA_KROWS = 16
_NA_CHAINS = 4


def _na_patterns():
    first = dict(delta=0, a=[max(i - NA_WIN_H // 2, 0) for i in range(_NA_QROWS)])
    inner = dict(delta=-(NA_WIN_H // 2), a=list(range(_NA_QROWS)))
    last = dict(delta=-(_NA_KROWS - _NA_QROWS),
                a=[min(i + NA_WIN_H // 2, _NA_KROWS - NA_WIN_H) for i in range(_NA_QROWS)])
    return first, inner, last


def _na_kernel(q_ref, k_ref, v_ref, kc_ref, vc_ref, tb_ref, o_ref, bias_head, bias_inner, bias_tail, *, n_blocks,
               scale):
    step = pl.program_id(1)
    n_steps = n_blocks // _NA_CHAINS
    w = GRID_W
    tq = _NA_QROWS * w
    lane_lo = lax.broadcasted_iota(jnp.int32, (w, 2 * w), 1) < w

    def build_bias(bias_ref, pat):
        for i in range(_NA_QROWS):
            for jj in range(_NA_KROWS // 2):
                j0 = 2 * jj
                in0 = 0 <= j0 - pat["a"][i] < NA_WIN_H
                in1 = 0 <= j0 + 1 - pat["a"][i] < NA_WIN_H
                rel0 = pat["delta"] + j0 - i + NA_WIN_H - 1
                if not (in0 or in1):
                    tile = jnp.full((w, 2 * w), _NEG, _F32)
                else:
                    tile = tb_ref[rel0 + 1]
                    if not in1:
                        tile = jnp.where(lane_lo, tile, _NEG)
                    elif not in0:
                        tile = jnp.where(lane_lo, _NEG, tile)
                bias_ref[i * w:(i + 1) * w, j0 * w:(j0 + 2) * w] = tile

    first, inner, last = _na_patterns()
    pl.when(step == 0)(functools.partial(build_bias, bias_head, first))
    pl.when(step == 0)(functools.partial(build_bias, bias_inner, inner))
    pl.when(step == 0)(functools.partial(build_bias, bias_tail, inner))
    pl.when(step == 1)(functools.partial(build_bias, bias_head, inner))
    pl.when(step == n_steps - 1)(functools.partial(build_bias, bias_tail, last))

    kc = kc_ref[...]
    vc = vc_ref[...]
    chain_bias = [bias_head] + [bias_inner] * (_NA_CHAINS - 2) + [bias_tail]
    for half, bias_ref in enumerate(chain_bias):
        b = _NA_CHAINS * step + half
        slab_row = jnp.where(b == 0, 0, jnp.where(b == n_blocks - 1, (n_blocks - 2) * _NA_QROWS,
                                                  b * _NA_QROWS - NA_WIN_H // 2))
        start = pl.multiple_of(slab_row * w, w)
        q = q_ref[half * tq:(half + 1) * tq, :]
        s_loc = _dot_nt(q, k_ref[pl.ds(start, _NA_KROWS * w), :]) * scale + bias_ref[...]
        s_ctx = _dot_nt(q, kc) * scale
        m = jnp.maximum(jnp.max(s_loc, axis=-1, keepdims=True), jnp.max(s_ctx, axis=-1, keepdims=True))
        p_loc = jnp.exp(s_loc - m)
        p_ctx = jnp.exp(s_ctx - m)
        denom = jnp.sum(p_loc, axis=-1, keepdims=True) + jnp.sum(p_ctx, axis=-1, keepdims=True)
        o = (jnp.dot(p_loc.astype(_BF16), v_ref[pl.ds(start, _NA_KROWS * w), :], preferred_element_type=_F32)
             + jnp.dot(p_ctx.astype(_BF16), vc, preferred_element_type=_F32))
        o_ref[half * tq:(half + 1) * tq, :] = (o / denom).astype(o_ref.dtype)


def _na_bias_table(rpb):
    qc = np.arange(GRID_W)
    win_start = np.clip(qc - NA_WIN_W // 2, 0, GRID_W - NA_WIN_W)
    rel = qc[None, :] - qc[:, None]
    in_win = (qc[None, :] >= win_start[:, None]) & (qc[None, :] < win_start[:, None] + NA_WIN_W)
    rel_idx = np.clip(rel + NA_WIN_W - 1, 0, 2 * NA_WIN_W - 2)
    toe = jnp.where(in_win[None, None], rpb[:, :, rel_idx], _NEG).astype(_F32)
    neg = jnp.full_like(toe[:, :1], _NEG)
    lo = jnp.concatenate([neg, toe], axis=1)
    hi = jnp.concatenate([toe, neg], axis=1)
    return jnp.concatenate([lo, hi], axis=-1)


def _na_attention(px, pc, rpb, n_heads, q_head0, k_head0, v_head0, ck_head0, cv_head0):
    l = px.shape[0]
    lc = pc.shape[0]
    rows = l // GRID_W
    n_blocks = rows // _NA_QROWS
    assert rows % (_NA_CHAINS * _NA_QROWS) == 0 and n_blocks >= 3
    tq = _NA_CHAINS * _NA_QROWS * GRID_W
    tb = _na_bias_table(rpb)
    head_blk = lambda n, h0: pl.BlockSpec((n, HEAD_DIM), lambda h, b: (0, h0 + h))
    bias_scratch = pltpu.VMEM((_NA_QROWS * GRID_W, _NA_KROWS * GRID_W), _F32)
    return pl.pallas_call(
        functools.partial(_na_kernel, n_blocks=n_blocks, scale=HEAD_DIM ** -0.5),
        grid=(n_heads, n_blocks // _NA_CHAINS),
        in_specs=[
            pl.BlockSpec((tq, HEAD_DIM), lambda h, b: (b, q_head0 + h)),
            head_blk(l, k_head0), head_blk(l, v_head0), head_blk(lc, ck_head0), head_blk(lc, cv_head0),
            pl.BlockSpec((None, 2 * NA_WIN_H, GRID_W, 2 * GRID_W), lambda h, b: (h, 0, 0, 0)),
        ],
        out_specs=pl.BlockSpec((tq, HEAD_DIM), lambda h, b: (b, h)),
        out_shape=jax.ShapeDtypeStruct((l, n_heads * HEAD_DIM), _BF16),
        scratch_shapes=[bias_scratch, bias_scratch, bias_scratch],
        compiler_params=_cparams("arbitrary", "arbitrary"), name="na_attention",
    )(px, px, px, pc, pc, tb)


_POOL_HALO = 16


def _pool_kernel(prev_ref, cur_ref, next_ref, g_ref, sh_ref, sc_ref, o_ref, *, n_tokens):
    i = pl.program_id(0)
    tm, d = cur_ref.shape
    dg = d // len(POOL_WINDOWS)
    x = jnp.concatenate([prev_ref[...], cur_ref[...], next_ref[...]], axis=0)
    a = _rms(x, g_ref[...]) * (1.0 + sc_ref[...]) + sh_ref[...]
    pos = i * tm - _POOL_HALO + lax.broadcasted_iota(jnp.int32, (tm + 2 * _POOL_HALO, 1), 0)
    a = jnp.where((pos >= 0) & (pos < n_tokens), a, 0.0)
    t = i * tm + lax.broadcasted_iota(jnp.int32, (tm, 1), 0)
    n_ext = tm + 2 * _POOL_HALO
    for g, w in enumerate(POOL_WINDOWS):
        ag = a[:, g * dg:(g + 1) * dg]
        run, span = ag, 1
        while span < w:
            run = run[:n_ext - 2 * span + 1, :] + run[span:n_ext - span + 1, :]
            span *= 2
        tot = run[_POOL_HALO - w // 2:_POOL_HALO - w // 2 + tm, :]
        cnt = jnp.minimum(t + (w - w // 2), n_tokens) - jnp.maximum(t - w // 2, 0)
        diff = tot / cnt.astype(_F32) - ag[_POOL_HALO:_POOL_HALO + tm, :]
        o_ref[:, g * dg:(g + 1) * dg] = diff.astype(o_ref.dtype)


def _pool_diffs(x, gain, shift, scale):
    t, d = x.shape
    assert all(w & (w - 1) == 0 and w // 2 <= _POOL_HALO for w in POOL_WINDOWS)
    tm = _tile(t, 256, _POOL_HALO)
    hb = tm // _POOL_HALO
    n_halo_blocks = t // _POOL_HALO
    row = pl.BlockSpec((1, d), lambda i: (0, 0))
    return pl.pallas_call(
        functools.partial(_pool_kernel, n_tokens=t), grid=(t // tm,),
        in_specs=[
            pl.BlockSpec((_POOL_HALO, d), lambda i: (jnp.maximum(i * hb - 1, 0), 0)),
            pl.BlockSpec((tm, d), lambda i: (i, 0)),
            pl.BlockSpec((_POOL_HALO, d), lambda i: (jnp.minimum((i + 1) * hb, n_halo_blocks - 1), 0)),
            row, row, row,
        ],
        out_specs=pl.BlockSpec((tm, d), lambda i: (i, 0)),
        out_shape=jax.ShapeDtypeStruct((t, d), _BF16),
        compiler_params=_cparams("parallel"), name="pool_diffs",
    )(x, x, x, gain[None, :], shift[None, :], scale[None, :])


def _router_kernel(t_ref, w_ref, b_ref, idx_ref, gate_ref, rank_ref, cnt_ref):
    n_exp = w_ref.shape[0]
    tm = t_ref.shape[0]
    per_group = n_exp // N_EXPERT_GROUPS

    @pl.when(pl.program_id(0) == 0)
    def _():
        cnt_ref[...] = jnp.zeros(cnt_ref.shape, _F32)

    logits = _dot_nt(w_ref[...], t_ref[...])
    scores = 1.0 / (1.0 + jnp.exp(-logits))
    biased = scores + b_ref[...]

    def first_max(vals):
        idx = lax.broadcasted_iota(jnp.int32, vals.shape, 0).astype(_F32)
        m = jnp.max(vals, axis=0, keepdims=True)
        first = jnp.min(jnp.where(vals == m, idx, float(vals.shape[0])), axis=0, keepdims=True)
        return m, idx == first

    def take_top(vals, k):
        hits = []
        for _ in range(k):
            _, hit = first_max(vals)
            hits.append(hit)
            vals = jnp.where(hit, -jnp.inf, vals)
        return hits

    def union(hits):
        sel = jnp.zeros(hits[0].shape, _F32)
        for hit in hits:
            sel = jnp.where(hit, 1.0, sel)
        return sel

    group_scores = []
    for g in range(N_EXPERT_GROUPS):
        v = biased[g * per_group:(g + 1) * per_group, :]
        m1, hit = first_max(v)
        m2 = jnp.max(jnp.where(hit, -jnp.inf, v), axis=0, keepdims=True)
        group_scores.append(m1 + m2)
    gsel = union(take_top(jnp.concatenate(group_scores, axis=0), TOPK_GROUPS))
    keep = jnp.concatenate([jnp.broadcast_to(gsel[g:g + 1, :], (per_group, tm)) for g in range(N_EXPERT_GROUPS)],
                           axis=0)
    hits = take_top(jnp.where(keep > 0.0, biased, -jnp.inf), TOP_K)
    esel = union(hits)
    wsel = esel * scores
    gates = wsel / jnp.sum(wsel, axis=0, keepdims=True) * ROUTED_SCALE

    earlier = (lax.broadcasted_iota(jnp.int32, (tm, tm), 0) < lax.broadcasted_iota(jnp.int32, (tm, tm), 1))
    rank = cnt_ref[...] + jnp.dot(esel.astype(_BF16), jnp.where(earlier, 1.0, 0.0).astype(_BF16),
                                  preferred_element_type=_F32)
    cnt_ref[...] += jnp.sum(esel, axis=1, keepdims=True)

    eidx = lax.broadcasted_iota(jnp.int32, (n_exp, tm), 0).astype(_F32)
    pick = lambda hit, v: jnp.sum(jnp.where(hit, v, 0.0), axis=0, keepdims=True)
    idx_ref[...] = jnp.concatenate([pick(h, eidx) for h in hits], axis=0).astype(jnp.int32)
    gate_ref[...] = jnp.concatenate([pick(h, gates) for h in hits], axis=0)
    rank_ref[...] = jnp.concatenate([pick(h, rank) for h in hits], axis=0).astype(jnp.int32)


def _router(t, w_router, router_bias):
    n_tok, d = t.shape
    n_exp = w_router.shape[1]
    tm = _tile(n_tok, 512, 128)
    per_tok = pl.BlockSpec((TOP_K, tm), lambda i: (0, i))
    return pl.pallas_call(
        _router_kernel, grid=(n_tok // tm,),
        in_specs=[
            pl.BlockSpec((tm, d), lambda i: (i, 0)),
            pl.BlockSpec((n_exp, d), lambda i: (0, 0)),
            pl.BlockSpec((n_exp, 1), lambda i: (0, 0)),
        ],
        out_specs=[per_tok, per_tok, per_tok, pl.BlockSpec((n_exp, 1), lambda i: (0, 0))],
        out_shape=[jax.ShapeDtypeStruct((TOP_K, n_tok), jnp.int32), jax.ShapeDtypeStruct((TOP_K, n_tok), _F32),
                   jax.ShapeDtypeStruct((TOP_K, n_tok), jnp.int32), jax.ShapeDtypeStruct((n_exp, 1), _F32)],
        compiler_params=_cparams("arbitrary"), name="moe_router",
    )(t, w_router.T.astype(_BF16), router_bias[:, None])


_MOE_TILE_ROWS = 256


def _packed_cols(d):
    return d // 2


def _pack_pairs(y):
    half = y.shape[1] // 2
    bits = lambda v: lax.bitcast_convert_type(v.astype(_BF16).astype(_F32), jnp.uint32)
    return (bits(y[:, :half]) & jnp.uint32(0xFFFF0000)) | (bits(y[:, half:]) >> jnp.uint32(16))


def _unpack_pairs(w):
    hi = lax.bitcast_convert_type(w & jnp.uint32(0xFFFF0000), _F32)
    lo = lax.bitcast_convert_type(w << jnp.uint32(16), _F32)
    return hi, lo


def _norm_mod_pack_kernel(x_ref, g_ref, sh_ref, sc_ref, o_ref, p_ref):
    y = _rms(x_ref[...], g_ref[...]) * (1.0 + sc_ref[...]) + sh_ref[...]
    o_ref[...] = y.astype(o_ref.dtype)
    p_ref[...] = _pack_pairs(y)


def _norm_mod_pack(x, gain, shift, scale):
    t, d = x.shape
    tm = _tile(t, 256)
    row = pl.BlockSpec((1, d), lambda i: (0, 0))
    blk = pl.BlockSpec((tm, d), lambda i: (i, 0))
    return pl.pallas_call(
        _norm_mod_pack_kernel, grid=(t // tm,), in_specs=[blk, row, row, row],
        out_specs=[blk, pl.BlockSpec((tm, _packed_cols(d)), lambda i: (i, 0))],
        out_shape=[jax.ShapeDtypeStruct((t, d), _BF16), jax.ShapeDtypeStruct((t, _packed_cols(d)), jnp.uint32)],
        compiler_params=_cparams("parallel"), name="rmsnorm_modulate_pack",
    )(x, gain[None, :], shift[None, :], scale[None, :])


def _moe_plan(idx8, rank8, counts, tile_rows):
    n_exp = counts.shape[0]
    n_tok = idx8.shape[1]
    cnt = counts[:, 0].astype(jnp.int32)
    padded = (cnt + tile_rows - 1) // tile_rows * tile_rows
    ends = jnp.cumsum(padded)
    starts = ends - padded
    onehot = idx8[:, :, None] == jnp.arange(n_exp, dtype=jnp.int32)
    pos8 = rank8 + jnp.sum(jnp.where(onehot, starts, 0), axis=-1)
    n_tiles = TOP_K * n_tok // tile_rows + n_exp
    tile_row0 = jnp.arange(n_tiles, dtype=jnp.int32) * tile_rows
    tile_expert = jnp.minimum(jnp.sum(tile_row0[:, None] >= ends[None, :], axis=1), n_exp - 1).astype(jnp.int32)
    n_active = (ends[-1:] // tile_rows).astype(jnp.int32)
    last_tile_row = jnp.where(padded > 0, ends - tile_rows, -1).astype(jnp.int32)
    return pos8.T.reshape(-1), tile_expert, n_active, last_tile_row, n_tiles


def _dispatch_kernel(pos_ref, last_ref, na_ref, tp_ref, xs_ref, zero_ref, sem, zsem, *, tile_rows):
    i = pl.program_id(0)
    tb = tp_ref.shape[0]
    n_exp = last_ref.shape[0]
    n_tiles = xs_ref.shape[0] // tile_rows

    @pl.when(i == 0)
    def _():
        zero_ref[...] = jnp.zeros(zero_ref.shape, zero_ref.dtype)

        def fill(row0):
            return pltpu.make_async_copy(zero_ref, xs_ref.at[pl.ds(pl.multiple_of(row0, tile_rows), tile_rows)],
                                         zsem)

        def each(e, start):
            tail = na_ref[0] + e

            def go(row0):
                cp = fill(row0)
                cp.start() if start else cp.wait()

            pl.when(last_ref[e] >= 0)(lambda: go(jnp.maximum(last_ref[e], 0)))
            pl.when(tail < n_tiles)(lambda: go(jnp.minimum(tail, n_tiles - 1) * tile_rows))

        lax.fori_loop(0, n_exp, lambda e, c: (each(e, True), c)[1], 0)
        lax.fori_loop(0, n_exp, lambda e, c: (each(e, False), c)[1], 0)

    def row(r, c):
        base = (i * tb + r) * TOP_K
        for k in range(TOP_K):
            pltpu.make_async_copy(tp_ref.at[pl.ds(r, 1)], xs_ref.at[pl.ds(pos_ref[base + k], 1)], sem).start()
        return c

    lax.fori_loop(0, tb, row, 0, unroll=2)
    for _ in range(TOP_K):
        pltpu.make_async_copy(tp_ref, xs_ref.at[pl.ds(0, tb)], sem).wait()


def _dispatch(tp, pos_flat, last_tile_row, n_active, n_rows, tile_rows):
    n_tok, c = tp.shape
    tb = _tile(n_tok, 512)
    return pl.pallas_call(
        functools.partial(_dispatch_kernel, tile_rows=tile_rows),
        grid_spec=pltpu.PrefetchScalarGridSpec(
            num_scalar_prefetch=3, grid=(n_tok // tb,),
            in_specs=[pl.BlockSpec((tb, c), lambda i, pos, last, na: (i, 0))],
            out_specs=pl.BlockSpec(memory_space=pl.ANY),
            scratch_shapes=[pltpu.VMEM((tile_rows, c), jnp.uint32), pltpu.SemaphoreType.DMA,
                            pltpu.SemaphoreType.DMA],
        ),
        out_shape=jax.ShapeDtypeStruct((n_rows, c), jnp.uint32),
        compiler_params=_cparams("arbitrary"), name="moe_dispatch",
    )(pos_flat, last_tile_row, n_active, tp)


def _grouped_ffn_kernel(te_ref, na_ref, xs_ref, wg_ref, wu_ref, wd_ref, ys_ref, wg_bf, wu_bf, wd_bf):
    i = pl.program_id(0)
    active = i < na_ref[0]
    new_expert = jnp.logical_or(i == 0, te_ref[i] != te_ref[jnp.maximum(i - 1, 0)])

    @pl.when(jnp.logical_and(active, new_expert))
    def _():
        wg_bf[...] = wg_ref[...].astype(_BF16)
        wu_bf[...] = wu_ref[...].astype(_BF16)
        wd_bf[...] = wd_ref[...].astype(_BF16)

    @pl.when(active)
    def _():
        half = wg_bf.shape[0] // 2
        hi, lo = _unpack_pairs(xs_ref[...])
        hi = hi.astype(_BF16)
        lo = lo.astype(_BF16)
        dot = lambda a, b: jnp.dot(a, b, preferred_element_type=_F32)
        hg = dot(hi, wg_bf[:half, :]) + dot(lo, wg_bf[half:, :])
        hu = dot(hi, wu_bf[:half, :]) + dot(lo, wu_bf[half:, :])
        h = (_silu(hg) * hu).astype(_BF16)
        ys_ref[...] = _pack_pairs(dot(h, wd_bf[...]))

    @pl.when(jnp.logical_not(active))
    def _():
        ys_ref[...] = jnp.zeros(ys_ref.shape, ys_ref.dtype)


def _grouped_ffn(xs, tile_expert, n_active, w_gate, w_up, w_down, layer, tile_rows):
    n_rows, c = xs.shape
    _, n_exp, d, ff = w_gate.shape
    n_tiles = n_rows // tile_rows
    row_blk = pl.BlockSpec((tile_rows, c), lambda i, te, na: (jnp.minimum(i, jnp.maximum(na[0] - 1, 0)), 0))
    return pl.pallas_call(
        _grouped_ffn_kernel,
        grid_spec=pltpu.PrefetchScalarGridSpec(
            num_scalar_prefetch=2, grid=(n_tiles,),
            in_specs=[row_blk,
                      pl.BlockSpec((None, None, d, ff), lambda i, te, na: (layer, te[i], 0, 0)),
                      pl.BlockSpec((None, None, d, ff), lambda i, te, na: (layer, te[i], 0, 0)),
                      pl.BlockSpec((None, None, ff, d), lambda i, te, na: (layer, te[i], 0, 0))],
            out_specs=pl.BlockSpec((tile_rows, c), lambda i, te, na: (i, 0)),
            scratch_shapes=[pltpu.VMEM((d, ff), _BF16), pltpu.VMEM((d, ff), _BF16), pltpu.VMEM((ff, d), _BF16)],
        ),
        out_shape=jax.ShapeDtypeStruct((n_rows, c), jnp.uint32),
        compiler_params=_cparams("arbitrary"), name="moe_grouped_ffn",
    )(tile_expert, n_active, xs, w_gate, w_up, w_down)


def _combine_kernel(pos_ref, ys_ref, g8_ref, x_ref, sh_ref, g2_ref, fg_ref, o_ref, buf, sems, *, final_norm):
    i = pl.program_id(0)
    tb, d = x_ref.shape
    half = d // 2

    def gather(block, slot):
        def row(r, c):
            base = (block * tb + r) * TOP_K
            for k in range(TOP_K):
                pltpu.make_async_copy(ys_ref.at[pl.ds(pos_ref[base + k], 1)], buf.at[slot, k, pl.ds(r, 1)],
                                      sems.at[slot]).start()
            return c

        lax.fori_loop(0, tb, row, 0, unroll=2)

    @pl.when(i == 0)
    def _():
        gather(0, 0)

    @pl.when(i + 1 < pl.num_programs(0))
    def _():
        gather(i + 1, lax.rem(i + 1, 2))

    slot = lax.rem(i, 2)
    for k in range(TOP_K):
        pltpu.make_async_copy(ys_ref.at[pl.ds(0, tb)], buf.at[slot, k], sems.at[slot]).wait()

    g8 = g8_ref[...]
    acc_hi = jnp.zeros((tb, half), _F32)
    acc_lo = jnp.zeros((tb, half), _F32)
    for k in range(TOP_K):
        hi, lo = _unpack_pairs(buf[slot, k])
        acc_hi = acc_hi + g8[:, k:k + 1] * hi
        acc_lo = acc_lo + g8[:, k:k + 1] * lo
    out_hi = x_ref[:, :half] + g2_ref[:, :half] * (sh_ref[:, :half] + acc_hi)
    out_lo = x_ref[:, half:] + g2_ref[:, half:] * (sh_ref[:, half:] + acc_lo)
    if final_norm:
        ssq = jnp.sum(out_hi * out_hi, axis=-1, keepdims=True) + jnp.sum(out_lo * out_lo, axis=-1, keepdims=True)
        inv = lax.rsqrt(ssq / d + RMS_EPS)
        out_hi = out_hi * inv * fg_ref[:, :half]
        out_lo = out_lo * inv * fg_ref[:, half:]
    o_ref[:, :half] = out_hi
    o_ref[:, half:] = out_lo


def _combine(ys, pos_flat, gate8, x, shared, gate2, final_gain):
    n_tok, d = x.shape
    c = ys.shape[1]
    tb = _tile(n_tok, 128)
    blk = pl.BlockSpec((tb, d), lambda i, pos: (i, 0))
    vec = pl.BlockSpec((1, d), lambda i, pos: (0, 0))
    final_norm = final_gain is not None
    if not final_norm:
        final_gain = jnp.ones((d,), _F32)
    return pl.pallas_call(
        functools.partial(_combine_kernel, final_norm=final_norm),
        grid_spec=pltpu.PrefetchScalarGridSpec(
            num_scalar_prefetch=1, grid=(n_tok // tb,),
            in_specs=[pl.BlockSpec(memory_space=pl.ANY),
                      pl.BlockSpec((tb, TOP_K), lambda i, pos: (i, 0)),
                      blk, blk, vec, vec],
            out_specs=blk,
            scratch_shapes=[pltpu.VMEM((2, TOP_K, tb, c), jnp.uint32), pltpu.SemaphoreType.DMA((2,))],
        ),
        out_shape=jax.ShapeDtypeStruct((n_tok, d), _F32),
        compiler_params=_cparams("arbitrary"), name="moe_combine",
    )(pos_flat, ys, gate8, x, shared, gate2[None, :], final_gain[None, :])


def _ffn_kernel(t_ref, wg_ref, wu_ref, wd_ref, o_ref):
    t = t_ref[...]
    h = _silu(jnp.dot(t, wg_ref[...], preferred_element_type=_F32)) * jnp.dot(t, wu_ref[...],
                                                                               preferred_element_type=_F32)
    o_ref[...] = jnp.dot(h.astype(_BF16), wd_ref[...], preferred_element_type=_F32)


def _ffn(t, w_gate, w_up, w_down):
    n_tok, d = t.shape
    ff = w_gate.shape[1]
    tm = _tile(n_tok, 256)
    blk = pl.BlockSpec((tm, d), lambda i: (i, 0))
    resident = lambda r, c: pl.BlockSpec((r, c), lambda i: (0, 0), pipeline_mode=pl.Buffered(1))
    return pl.pallas_call(
        _ffn_kernel, grid=(n_tok // tm,),
        in_specs=[blk, resident(d, ff), resident(d, ff), resident(ff, d)],
        out_specs=blk,
        out_shape=jax.ShapeDtypeStruct((n_tok, d), _F32),
        compiler_params=_cparams("parallel"), name="shared_ffn",
    )(t, w_gate, w_up, w_down)


def _moe_residual(x, gain, shift, scale, gate2, w_router, router_bias, w_gate, w_up, w_down, layer, ws_gate, ws_up,
                  ws_down, final_gain):
    t, tp = _norm_mod_pack(x, gain, shift, scale)
    idx8, gate8, rank8, counts = _router(t, w_router, router_bias)
    pos_flat, tile_expert, n_active, last_tile_row, n_tiles = _moe_plan(idx8, rank8, counts, _MOE_TILE_ROWS)
    xs = _dispatch(tp, pos_flat, last_tile_row, n_active, n_tiles * _MOE_TILE_ROWS, _MOE_TILE_ROWS)
    ys = _grouped_ffn(xs, tile_expert, n_active, w_gate, w_up, w_down, layer, _MOE_TILE_ROWS)
    shared = _ffn(t, ws_gate.astype(_BF16), ws_up.astype(_BF16), ws_down.astype(_BF16))
    return _combine(ys, pos_flat, gate8.T, x, shared, gate2, final_gain)


def _attention_layer(x, h_c, mods, mods_c, g_mix, w_in, w_out, rpb, q_gain, k_gain):
    shift1, scale1, gate1 = mods[:3]
    cshift1, cscale1 = mods_c[:2]
    n_tok = x.shape[0]
    na_w = NA_HEADS * HEAD_DIM
    gq_w = GQA_Q_HEADS * HEAD_DIM
    gkv_w = GQA_KV_HEADS * HEAD_DIM
    q_cols = na_w + gq_w
    o_nk, o_gk, o_gv, o_end = q_cols, q_cols + 2 * na_w, q_cols + 2 * na_w + gkv_w, w_in.shape[1]
    tn = int(np.gcd.reduce([na_w, gq_w, gkv_w, 512]))
    tiles = lambda lo, hi: list(range(lo // tn, hi // tn))
    w_bf = w_in.astype(_BF16)

    a_x = _norm_mod(x, g_mix, shift1, scale1, _BF16)
    a_c = _norm_mod(h_c, g_mix, cshift1, cscale1, _BF16)

    kv_plain = tiles(o_nk, o_gk) + tiles(o_gv, o_end)
    p_plain = _matmul_cols(a_x, w_bf, tiles(0, na_w) + kv_plain, tn, _BF16)
    c_plain = _matmul_cols(a_c, w_bf, kv_plain, tn, _BF16)

    cos, sin = _rope_tables(n_tok)
    gains = jnp.concatenate([jnp.tile(q_gain, GQA_Q_HEADS), jnp.tile(k_gain, GQA_KV_HEADS)])
    post = jnp.concatenate([jnp.full((gq_w,), HEAD_DIM ** -0.5 * np.log2(np.e), _F32), jnp.ones((gkv_w,), _F32)])
    qk = _matmul_cols_norm_rope(a_x, w_bf, tiles(na_w, q_cols) + tiles(o_gk, o_gv), tn, gains, post, cos,
                                sin)
    ck = _matmul_cols_norm_rope(a_c, w_bf, tiles(o_gk, o_gv), tn, gains[gq_w:], post[gq_w:], None, None)

    o_a = _na_attention(p_plain, c_plain, rpb, NA_HEADS, 0, NA_HEADS, 2 * NA_HEADS, 0, NA_HEADS)
    k_all = jnp.concatenate([qk[:, gq_w:], ck], axis=0)
    v_all = jnp.concatenate([p_plain[:, 3 * na_w:], c_plain[:, 2 * na_w:]], axis=0)
    o_b = _gqa_attention(qk, k_all, v_all, GQA_KV_HEADS, GQA_Q_HEADS)
    return _matmul2_res(o_a, o_b, w_out.astype(_BF16), x, gate1)


def _pool_layer(x, mods, g_mix, w_pool, pool_scale):
    shift1, scale1, gate1 = mods[:3]
    diffs = _pool_diffs(x, g_mix, shift1, scale1)
    return _matmul(diffs, w_pool.astype(_BF16), grouped=True, res=x, gate=gate1, colscale=pool_scale)


def kernel(x, c, ctx, c_ctx, w_mod, b_mod, g_mix, g_ffn, attn_w_in, attn_w_out, na_rpb, q_gain, k_gain, pool_w,
           pool_scale, moe_w_router, moe_router_bias, moe_w_gate, moe_w_up, moe_w_down, moe_ws_gate, moe_ws_up,
           moe_ws_down, g_final):
    batch, seq, d = x.shape
    depth = w_mod.shape[0]
    outs = []
    for bi in range(batch):
        xb = x[bi]
        h_c = ctx[bi]
        conds = jnp.stack([c[bi], c_ctx])
        for i in range(depth):
            j = i // 2
            mod_rows = _adaln(conds, w_mod, i, b_mod[i])
            mods = jnp.split(mod_rows[0], N_MOD)
            mods_c = jnp.split(mod_rows[1], N_MOD)
            if i % 2 == 0:
                xb = _attention_layer(xb, h_c, mods, mods_c, g_mix[i], attn_w_in[j], attn_w_out[j], na_rpb[j],
                                      q_gain[j], k_gain[j])
            else:
                xb = _pool_layer(xb, mods, g_mix[i], pool_w[j], pool_scale[j])
            assert not any(l % 2 == 0 for l in range(i + 1, depth)), "context-stream update not implemented"
            xb = _moe_residual(xb, g_ffn[i], mods[3], mods[4], mods[5], moe_w_router[i], moe_router_bias[i],
                               moe_w_gate, moe_w_up, moe_w_down, i, moe_ws_gate[i], moe_ws_up[i], moe_ws_down[i],
                               g_final if i == depth - 1 else None)
        outs.append(xb)
    return jnp.stack(outs)
```

```python
import functools

import numpy as np
import jax
import jax.numpy as jnp
from jax import lax
from jax.experimental import pallas as pl
from jax.experimental.pallas import tpu as pltpu

GRID_W = 64
HEAD_DIM = 128
NA_HEADS = 16
GQA_Q_HEADS = 16
GQA_KV_HEADS = 4
NA_WIN_H = 8
NA_WIN_W = 16
ROPE_THETA = 10000.0
POOL_WINDOWS = (2, 4, 8, 16)
N_EXPERT_GROUPS = 8
TOPK_GROUPS = 4
TOP_K = 8
ROUTED_SCALE = 2.5
N_MOD = 6
RMS_EPS = 1e-6

_F32 = jnp.float32
_BF16 = jnp.bfloat16
_NEG = -1e30
_VMEM_LIMIT_BYTES = 56 * 1024 * 1024


def _cparams(*sem):
    return pltpu.CompilerParams(dimension_semantics=sem, vmem_limit_bytes=_VMEM_LIMIT_BYTES)


def _tile(n, target, mult=8):
    t = min(n, target) // mult * mult
    while t > mult and n % t:
        t -= mult
    assert t > 0 and n % t == 0, (n, target, mult)
    return t


def _silu(x):
    return x / (1.0 + jnp.exp(-x))


def _adaln_kernel(c_ref, w_ref, b_ref, o_ref, *, kc):
    n_rows, d, _ = c_ref.shape
    bn = w_ref.shape[1]
    accs = [jnp.zeros((1, bn), _F32) for _ in range(n_rows)]
    for k0 in range(0, d, kc):
        w = w_ref[k0:k0 + kc, :]
        for r in range(n_rows):
            cc = _silu(c_ref[r, k0:k0 + kc, :])
            accs[r] = accs[r] + jnp.sum(cc * w, axis=0, keepdims=True)
    o_ref[...] = jnp.concatenate(accs, axis=0) + b_ref[...]


def _adaln(conds, w_mod, layer, b_mod):
    n_rows, d = conds.shape
    n = w_mod.shape[2]
    bn = _tile(n, 512, 128)
    return pl.pallas_call(
        functools.partial(_adaln_kernel, kc=_tile(d, 256)),
        grid=(n // bn,),
        in_specs=[
            pl.BlockSpec((n_rows, d, 1), lambda j: (0, 0, 0)),
            pl.BlockSpec((None, d, bn), lambda j: (layer, 0, j)),
            pl.BlockSpec((1, bn), lambda j: (0, j)),
        ],
        out_specs=pl.BlockSpec((n_rows, bn), lambda j: (0, j)),
        out_shape=jax.ShapeDtypeStruct((n_rows, n), _F32),
        compiler_params=_cparams("arbitrary"),
        name="adaln",
    )(conds[:, :, None], w_mod, b_mod[None, :])


def _rms(x, gain):
    return x * lax.rsqrt(jnp.mean(x * x, axis=-1, keepdims=True) + RMS_EPS) * gain


def _norm_mod_kernel(x_ref, g_ref, sh_ref, sc_ref, o_ref):
    y = _rms(x_ref[...], g_ref[...])
    o_ref[...] = (y * (1.0 + sc_ref[...]) + sh_ref[...]).astype(o_ref.dtype)


def _norm_mod(x, gain, shift, scale, out_dtype):
    t, d = x.shape
    tm = _tile(t, 256)
    row = pl.BlockSpec((1, d), lambda i: (0, 0))
    blk = pl.BlockSpec((tm, d), lambda i: (i, 0))
    return pl.pallas_call(
        _norm_mod_kernel, grid=(t // tm,), in_specs=[blk, row, row, row], out_specs=blk,
        out_shape=jax.ShapeDtypeStruct((t, d), out_dtype),
        compiler_params=_cparams("parallel"), name="rmsnorm_modulate",
    )(x, gain[None, :], shift[None, :], scale[None, :])


def _mm_kernel(a_ref, w_ref, o_ref):
    o_ref[...] = jnp.dot(a_ref[...], w_ref[...], preferred_element_type=_F32).astype(o_ref.dtype)


def _mm_res_kernel(a_ref, w_ref, res_ref, gate_ref, cs_ref, o_ref):
    y = jnp.dot(a_ref[...], w_ref[...], preferred_element_type=_F32) * cs_ref[...]
    o_ref[...] = res_ref[...] + gate_ref[...] * y


def _matmul(a, w, out_dtype=_F32, *, grouped=False, res=None, gate=None, colscale=None, tm_target=512,
            tn_target=1024):
    m = a.shape[0]
    if grouped:
        g, kk, tn = w.shape
        n = g * tn
        a_spec = lambda tm: pl.BlockSpec((tm, kk), lambda j, i: (i, j))
        w_spec = pl.BlockSpec((None, kk, tn), lambda j, i: (j, 0, 0))
    else:
        kk, n = w.shape
        tn = _tile(n, tn_target, 128)
        a_spec = lambda tm: pl.BlockSpec((tm, kk), lambda j, i: (i, 0))
        w_spec = pl.BlockSpec((kk, tn), lambda j, i: (0, j))
    tm = _tile(m, tm_target)
    o_spec = pl.BlockSpec((tm, tn), lambda j, i: (i, j))
    row = pl.BlockSpec((1, tn), lambda j, i: (0, j))
    grid = (n // tn, m // tm)
    if res is None:
        return pl.pallas_call(
            _mm_kernel, grid=grid, in_specs=[a_spec(tm), w_spec], out_specs=o_spec,
            out_shape=jax.ShapeDtypeStruct((m, n), out_dtype),
            compiler_params=_cparams("parallel", "parallel"), name="matmul",
        )(a, w)
    if colscale is None:
        colscale = jnp.ones((n,), _F32)
    return pl.pallas_call(
        _mm_res_kernel, grid=grid, in_specs=[a_spec(tm), w_spec, o_spec, row, row], out_specs=o_spec,
        out_shape=jax.ShapeDtypeStruct((m, n), _F32),
        compiler_params=_cparams("parallel", "parallel"), name="matmul_gated_residual",
    )(a, w, res, gate[None, :], colscale[None, :])


def _mm2_res_kernel(a1_ref, a2_ref, w1_ref, w2_ref, res_ref, gate_ref, o_ref):
    y = (jnp.dot(a1_ref[...], w1_ref[...], preferred_element_type=_F32)
         + jnp.dot(a2_ref[...], w2_ref[...], preferred_element_type=_F32))
    o_ref[...] = res_ref[...] + gate_ref[...] * y


def _matmul2_res(a1, a2, w, res, gate):
    m, k1 = a1.shape
    assert a2.shape == (m, k1) and w.shape[0] == 2 * k1
    n = w.shape[1]
    tm = _tile(m, 512)
    tn = _tile(n, 1024, 128)
    a_spec = pl.BlockSpec((tm, k1), lambda j, i: (i, 0))
    o_spec = pl.BlockSpec((tm, tn), lambda j, i: (i, j))
    return pl.pallas_call(
        _mm2_res_kernel, grid=(n // tn, m // tm),
        in_specs=[a_spec, a_spec, pl.BlockSpec((k1, tn), lambda j, i: (0, j)),
                  pl.BlockSpec((k1, tn), lambda j, i: (1, j)), o_spec, pl.BlockSpec((1, tn), lambda j, i: (0, j))],
        out_specs=o_spec, out_shape=jax.ShapeDtypeStruct((m, n), _F32),
        compiler_params=_cparams("parallel", "parallel"), name="matmul2_gated_residual",
    )(a1, a2, w, w, res, gate[None, :])


def _mm_cols_kernel(tab_ref, a_ref, w_ref, o_ref):
    del tab_ref
    _mm_kernel(a_ref, w_ref, o_ref)


def _matmul_cols(a, w, col_tiles, tn, out_dtype):
    m, kk = a.shape
    tm = _tile(m, 512)
    n = len(col_tiles) * tn
    return pl.pallas_call(
        _mm_cols_kernel,
        grid_spec=pltpu.PrefetchScalarGridSpec(
            num_scalar_prefetch=1, grid=(len(col_tiles), m // tm),
            in_specs=[pl.BlockSpec((tm, kk), lambda j, i, tab: (i, 0)),
                      pl.BlockSpec((kk, tn), lambda j, i, tab: (0, tab[j]))],
            out_specs=pl.BlockSpec((tm, tn), lambda j, i, tab: (i, j)),
        ),
        out_shape=jax.ShapeDtypeStruct((m, n), out_dtype),
        compiler_params=_cparams("parallel", "parallel"), name="matmul_cols",
    )(jnp.asarray(col_tiles, jnp.int32), a, w)


def _mm_cols_norm_rope_kernel(tab_ref, a_ref, w_ref, gain_ref, post_ref, cos_ref, sin_ref, o_ref, *, rope):
    del tab_ref
    x = jnp.dot(a_ref[...], w_ref[...], preferred_element_type=_F32)
    if rope:
        cos = cos_ref[...]
        sin = sin_ref[...]
        lane = lax.broadcasted_iota(jnp.int32, cos.shape, 1)
        first_half = (lane % (HEAD_DIM // 2)) < (HEAD_DIM // 4)
    for h in range(x.shape[1] // HEAD_DIM):
        sl = slice(h * HEAD_DIM, (h + 1) * HEAD_DIM)
        y = _rms(x[:, sl], gain_ref[:, sl])
        if rope:
            partner = jnp.where(first_half, pltpu.roll(y, HEAD_DIM - HEAD_DIM // 4, 1),
                                pltpu.roll(y, HEAD_DIM // 4, 1))
            y = y * cos + partner * sin
        o_ref[:, sl] = (y * post_ref[:, sl]).astype(o_ref.dtype)


def _matmul_cols_norm_rope(a, w, col_tiles, tn, gains, post_scale, cos, sin_signed):
    m, kk = a.shape
    tm = _tile(m, 512)
    n = len(col_tiles) * tn
    rope = cos is not None
    if not rope:
        cos = jnp.zeros((m, HEAD_DIM), _F32)
        sin_signed = cos
    vec = pl.BlockSpec((1, tn), lambda j, i, tab: (0, j))
    table = pl.BlockSpec((tm, HEAD_DIM), lambda j, i, tab: (i, 0))
    return pl.pallas_call(
        functools.partial(_mm_cols_norm_rope_kernel, rope=rope),
        grid_spec=pltpu.PrefetchScalarGridSpec(
            num_scalar_prefetch=1, grid=(len(col_tiles), m // tm),
            in_specs=[pl.BlockSpec((tm, kk), lambda j, i, tab: (i, 0)),
                      pl.BlockSpec((kk, tn), lambda j, i, tab: (0, tab[j])), vec, vec, table, table],
            out_specs=pl.BlockSpec((tm, tn), lambda j, i, tab: (i, j)),
        ),
        out_shape=jax.ShapeDtypeStruct((m, n), _BF16),
        compiler_params=_cparams("parallel", "parallel"), name="matmul_cols_norm_rope",
    )(jnp.asarray(col_tiles, jnp.int32), a, w, gains[None, :], post_scale[None, :], cos, sin_signed)


def _rope_tables(n):
    axis_dim = HEAD_DIM // 2
    t = jnp.arange(n)
    row = (t // GRID_W).astype(_F32)
    col = (t % GRID_W).astype(_F32)
    inv = ROPE_THETA ** (-jnp.arange(0, axis_dim, 2, dtype=_F32) / axis_dim)
    ang_r = row[:, None] * inv
    ang_c = col[:, None] * inv
    cos = jnp.concatenate([jnp.cos(ang_r)] * 2 + [jnp.cos(ang_c)] * 2, axis=-1)
    sin = jnp.concatenate([-jnp.sin(ang_r), jnp.sin(ang_r), -jnp.sin(ang_c), jnp.sin(ang_c)], axis=-1)
    return cos, sin


def _dot_nt(a, b):
    return lax.dot_general(a, b, (((1,), (1,)), ((), ())), preferred_element_type=_F32)


_GQA_Q_BLOCK = 256
_GQA_KEY_BLOCK = 8320


def _gqa_kernel(q_ref, k_ref, v_ref, o_ref, m_ref, acc_ref, *, tk):
    group = q_ref.shape[1] // HEAD_DIM
    m_ref[...] = jnp.full(m_ref.shape, -jnp.inf, _F32)
    acc_ref[...] = jnp.zeros(acc_ref.shape, _F32)

    def body(j, carry):
        start = pl.multiple_of(j * tk, tk)
        for g in range(group):
            s = _dot_nt(q_ref[:, g * HEAD_DIM:(g + 1) * HEAD_DIM], k_ref[pl.ds(start, tk), :])
            m_prev = m_ref[g]
            m_new = jnp.maximum(m_prev, jnp.max(s, axis=-1, keepdims=True))
            p = jnp.exp2(s - m_new)
            acc_ref[g] = jnp.exp2(m_prev - m_new) * acc_ref[g] + jnp.dot(
                p.astype(_BF16), v_ref[pl.ds(start, tk), :], preferred_element_type=_F32)
            m_ref[g] = m_new
        return carry

    lax.fori_loop(0, k_ref.shape[0] // tk, body, 0)
    for g in range(group):
        acc = acc_ref[g]
        o_ref[:, g * HEAD_DIM:(g + 1) * HEAD_DIM] = (acc[:, :HEAD_DIM] / acc[:, HEAD_DIM:HEAD_DIM + 1]).astype(
            o_ref.dtype)


def _gqa_attention(q, k, v, n_kv_heads, n_q_heads):
    l = q.shape[0]
    qw = n_q_heads * HEAD_DIM
    s = k.shape[0]
    gw = qw // n_kv_heads
    tq = _tile(l, _GQA_Q_BLOCK)
    tk = _tile(s, _GQA_KEY_BLOCK, 128)
    group = gw // HEAD_DIM
    ones_col = jnp.zeros((s, n_kv_heads, HEAD_DIM), v.dtype).at[:, :, 0].set(1)
    v_aug = jnp.concatenate([v.reshape(s, n_kv_heads, HEAD_DIM), ones_col], axis=-1).reshape(s, -1)
    return pl.pallas_call(
        functools.partial(_gqa_kernel, tk=tk),
        grid=(n_kv_heads, l // tq),
        in_specs=[
            pl.BlockSpec((tq, gw), lambda h, i: (i, h)),
            pl.BlockSpec((s, HEAD_DIM), lambda h, i: (0, h), pipeline_mode=pl.Buffered(1)),
            pl.BlockSpec((s, 2 * HEAD_DIM), lambda h, i: (0, h), pipeline_mode=pl.Buffered(1)),
        ],
        out_specs=pl.BlockSpec((tq, gw), lambda h, i: (i, h)),
        out_shape=jax.ShapeDtypeStruct((l, qw), _BF16),
        scratch_shapes=[pltpu.VMEM((group, tq, 1), _F32), pltpu.VMEM((group, tq, 2 * HEAD_DIM), _F32)],
        compiler_params=_cparams("arbitrary", "arbitrary"), name="gqa_attention",
    )(q, k, v_aug)


_NA_QROWS = 8
_NA_KROWS = 16
_NA_CHAINS = 4


def _na_patterns():
    first = dict(delta=0, a=[max(i - NA_WIN_H // 2, 0) for i in range(_NA_QROWS)])
    inner = dict(delta=-(NA_WIN_H // 2), a=list(range(_NA_QROWS)))
    last = dict(delta=-(_NA_KROWS - _NA_QROWS),
                a=[min(i + NA_WIN_H // 2, _NA_KROWS - NA_WIN_H) for i in range(_NA_QROWS)])
    return first, inner, last


def _na_kernel(q_ref, k_ref, v_ref, kc_ref, vc_ref, tb_ref, o_ref, bias_head, bias_inner, bias_tail, *, n_blocks,
               scale):
    step = pl.program_id(1)
    n_steps = n_blocks // _NA_CHAINS
    w = GRID_W
    tq = _NA_QROWS * w
    lane_lo = lax.broadcasted_iota(jnp.int32, (w, 2 * w), 1) < w

    def build_bias(bias_ref, pat):
        for i in range(_NA_QROWS):
            for jj in range(_NA_KROWS // 2):
                j0 = 2 * jj
                in0 = 0 <= j0 - pat["a"][i] < NA_WIN_H
                in1 = 0 <= j0 + 1 - pat["a"][i] < NA_WIN_H
                rel0 = pat["delta"] + j0 - i + NA_WIN_H - 1
                if not (in0 or in1):
                    tile = jnp.full((w, 2 * w), _NEG, _F32)
                else:
                    tile = tb_ref[rel0 + 1]
                    if not in1:
                        tile = jnp.where(lane_lo, tile, _NEG)
                    elif not in0:
                        tile = jnp.where(lane_lo, _NEG, tile)
                bias_ref[i * w:(i + 1) * w, j0 * w:(j0 + 2) * w] = tile

    first, inner, last = _na_patterns()
    pl.when(step == 0)(functools.partial(build_bias, bias_head, first))
    pl.when(step == 0)(functools.partial(build_bias, bias_inner, inner))
    pl.when(step == 0)(functools.partial(build_bias, bias_tail, inner))
    pl.when(step == 1)(functools.partial(build_bias, bias_head, inner))
    pl.when(step == n_steps - 1)(functools.partial(build_bias, bias_tail, last))

    kc = kc_ref[...]
    vc = vc_ref[...]
    chain_bias = [bias_head] + [bias_inner] * (_NA_CHAINS - 2) + [bias_tail]
    for half, bias_ref in enumerate(chain_bias):
        b = _NA_CHAINS * step + half
        slab_row = jnp.where(b == 0, 0, jnp.where(b == n_blocks - 1, (n_blocks - 2) * _NA_QROWS,
                                                  b * _NA_QROWS - NA_WIN_H // 2))
        start = pl.multiple_of(slab_row * w, w)
        q = q_ref[half * tq:(half + 1) * tq, :]
        s_loc = _dot_nt(q, k_ref[pl.ds(start, _NA_KROWS * w), :]) * scale + bias_ref[...]
        s_ctx = _dot_nt(q, kc) * scale
        m = jnp.maximum(jnp.max(s_loc, axis=-1, keepdims=True), jnp.max(s_ctx, axis=-1, keepdims=True))
        p_loc = jnp.exp(s_loc - m)
        p_ctx = jnp.exp(s_ctx - m)
        denom = jnp.sum(p_loc, axis=-1, keepdims=True) + jnp.sum(p_ctx, axis=-1, keepdims=True)
        o = (jnp.dot(p_loc.astype(_BF16), v_ref[pl.ds(start, _NA_KROWS * w), :], preferred_element_type=_F32)
             + jnp.dot(p_ctx.astype(_BF16), vc, preferred_element_type=_F32))
        o_ref[half * tq:(half + 1) * tq, :] = (o / denom).astype(o_ref.dtype)


def _na_bias_table(rpb):
    qc = np.arange(GRID_W)
    win_start = np.clip(qc - NA_WIN_W // 2, 0, GRID_W - NA_WIN_W)
    rel = qc[None, :] - qc[:, None]
    in_win = (qc[None, :] >= win_start[:, None]) & (qc[None, :] < win_start[:, None] + NA_WIN_W)
    rel_idx = np.clip(rel + NA_WIN_W - 1, 0, 2 * NA_WIN_W - 2)
    toe = jnp.where(in_win[None, None], rpb[:, :, rel_idx], _NEG).astype(_F32)
    neg = jnp.full_like(toe[:, :1], _NEG)
    lo = jnp.concatenate([neg, toe], axis=1)
    hi = jnp.concatenate([toe, neg], axis=1)
    return jnp.concatenate([lo, hi], axis=-1)


def _na_attention(px, pc, rpb, n_heads, q_head0, k_head0, v_head0, ck_head0, cv_head0):
    l = px.shape[0]
    lc = pc.shape[0]
    rows = l // GRID_W
    n_blocks = rows // _NA_QROWS
    assert rows % (_NA_CHAINS * _NA_QROWS) == 0 and n_blocks >= 3
    tq = _NA_CHAINS * _NA_QROWS * GRID_W
    tb = _na_bias_table(rpb)
    head_blk = lambda n, h0: pl.BlockSpec((n, HEAD_DIM), lambda h, b: (0, h0 + h))
    bias_scratch = pltpu.VMEM((_NA_QROWS * GRID_W, _NA_KROWS * GRID_W), _F32)
    return pl.pallas_call(
        functools.partial(_na_kernel, n_blocks=n_blocks, scale=HEAD_DIM ** -0.5),
        grid=(n_heads, n_blocks // _NA_CHAINS),
        in_specs=[
            pl.BlockSpec((tq, HEAD_DIM), lambda h, b: (b, q_head0 + h)),
            head_blk(l, k_head0), head_blk(l, v_head0), head_blk(lc, ck_head0), head_blk(lc, cv_head0),
            pl.BlockSpec((None, 2 * NA_WIN_H, GRID_W, 2 * GRID_W), lambda h, b: (h, 0, 0, 0)),
        ],
        out_specs=pl.BlockSpec((tq, HEAD_DIM), lambda h, b: (b, h)),
        out_shape=jax.ShapeDtypeStruct((l, n_heads * HEAD_DIM), _BF16),
        scratch_shapes=[bias_scratch, bias_scratch, bias_scratch],
        compiler_params=_cparams("arbitrary", "arbitrary"), name="na_attention",
    )(px, px, px, pc, pc, tb)


_POOL_HALO = 16


def _pool_kernel(prev_ref, cur_ref, next_ref, g_ref, sh_ref, sc_ref, o_ref, *, n_tokens):
    i = pl.program_id(0)
    tm, d = cur_ref.shape
    dg = d // len(POOL_WINDOWS)
    x = jnp.concatenate([prev_ref[...], cur_ref[...], next_ref[...]], axis=0)
    a = _rms(x, g_ref[...]) * (1.0 + sc_ref[...]) + sh_ref[...]
    pos = i * tm - _POOL_HALO + lax.broadcasted_iota(jnp.int32, (tm + 2 * _POOL_HALO, 1), 0)
    a = jnp.where((pos >= 0) & (pos < n_tokens), a, 0.0)
    t = i * tm + lax.broadcasted_iota(jnp.int32, (tm, 1), 0)
    n_ext = tm + 2 * _POOL_HALO
    for g, w in enumerate(POOL_WINDOWS):
        ag = a[:, g * dg:(g + 1) * dg]
        run, span = ag, 1
        while span < w:
            run = run[:n_ext - 2 * span + 1, :] + run[span:n_ext - span + 1, :]
            span *= 2
        tot = run[_POOL_HALO - w // 2:_POOL_HALO - w // 2 + tm, :]
        cnt = jnp.minimum(t + (w - w // 2), n_tokens) - jnp.maximum(t - w // 2, 0)
        diff = tot / cnt.astype(_F32) - ag[_POOL_HALO:_POOL_HALO + tm, :]
        o_ref[:, g * dg:(g + 1) * dg] = diff.astype(o_ref.dtype)


def _pool_diffs(x, gain, shift, scale):
    t, d = x.shape
    assert all(w & (w - 1) == 0 and w // 2 <= _POOL_HALO for w in POOL_WINDOWS)
    tm = _tile(t, 256, _POOL_HALO)
    hb = tm // _POOL_HALO
    n_halo_blocks = t // _POOL_HALO
    row = pl.BlockSpec((1, d), lambda i: (0, 0))
    return pl.pallas_call(
        functools.partial(_pool_kernel, n_tokens=t), grid=(t // tm,),
        in_specs=[
            pl.BlockSpec((_POOL_HALO, d), lambda i: (jnp.maximum(i * hb - 1, 0), 0)),
            pl.BlockSpec((tm, d), lambda i: (i, 0)),
            pl.BlockSpec((_POOL_HALO, d), lambda i: (jnp.minimum((i + 1) * hb, n_halo_blocks - 1), 0)),
            row, row, row,
        ],
        out_specs=pl.BlockSpec((tm, d), lambda i: (i, 0)),
        out_shape=jax.ShapeDtypeStruct((t, d), _BF16),
        compiler_params=_cparams("parallel"), name="pool_diffs",
    )(x, x, x, gain[None, :], shift[None, :], scale[None, :])


def _router_kernel(t_ref, w_ref, b_ref, idx_ref, gate_ref, rank_ref, cnt_ref):
    n_exp = w_ref.shape[0]
    tm = t_ref.shape[0]
    per_group = n_exp // N_EXPERT_GROUPS

    @pl.when(pl.program_id(0) == 0)
    def _():
        cnt_ref[...] = jnp.zeros(cnt_ref.shape, _F32)

    logits = _dot_nt(w_ref[...], t_ref[...])
    scores = 1.0 / (1.0 + jnp.exp(-logits))
    biased = scores + b_ref[...]

    def first_max(vals):
        idx = lax.broadcasted_iota(jnp.int32, vals.shape, 0).astype(_F32)
        m = jnp.max(vals, axis=0, keepdims=True)
        first = jnp.min(jnp.where(vals == m, idx, float(vals.shape[0])), axis=0, keepdims=True)
        return m, idx == first

    def take_top(vals, k):
        hits = []
        for _ in range(k):
            _, hit = first_max(vals)
            hits.append(hit)
            vals = jnp.where(hit, -jnp.inf, vals)
        return hits

    def union(hits):
        sel = jnp.zeros(hits[0].shape, _F32)
        for hit in hits:
            sel = jnp.where(hit, 1.0, sel)
        return sel

    group_scores = []
    for g in range(N_EXPERT_GROUPS):
        v = biased[g * per_group:(g + 1) * per_group, :]
        m1, hit = first_max(v)
        m2 = jnp.max(jnp.where(hit, -jnp.inf, v), axis=0, keepdims=True)
        group_scores.append(m1 + m2)
    gsel = union(take_top(jnp.concatenate(group_scores, axis=0), TOPK_GROUPS))
    keep = jnp.concatenate([jnp.broadcast_to(gsel[g:g + 1, :], (per_group, tm)) for g in range(N_EXPERT_GROUPS)],
                           axis=0)
    hits = take_top(jnp.where(keep > 0.0, biased, -jnp.inf), TOP_K)
    esel = union(hits)
    wsel = esel * scores
    gates = wsel / jnp.sum(wsel, axis=0, keepdims=True) * ROUTED_SCALE

    earlier = (lax.broadcasted_iota(jnp.int32, (tm, tm), 0) < lax.broadcasted_iota(jnp.int32, (tm, tm), 1))
    rank = cnt_ref[...] + jnp.dot(esel.astype(_BF16), jnp.where(earlier, 1.0, 0.0).astype(_BF16),
                                  preferred_element_type=_F32)
    cnt_ref[...] += jnp.sum(esel, axis=1, keepdims=True)

    eidx = lax.broadcasted_iota(jnp.int32, (n_exp, tm), 0).astype(_F32)
    pick = lambda hit, v: jnp.sum(jnp.where(hit, v, 0.0), axis=0, keepdims=True)
    idx_ref[...] = jnp.concatenate([pick(h, eidx) for h in hits], axis=0).astype(jnp.int32)
    gate_ref[...] = jnp.concatenate([pick(h, gates) for h in hits], axis=0)
    rank_ref[...] = jnp.concatenate([pick(h, rank) for h in hits], axis=0).astype(jnp.int32)


def _router(t, w_router, router_bias):
    n_tok, d = t.shape
    n_exp = w_router.shape[1]
    tm = _tile(n_tok, 512, 128)
    per_tok = pl.BlockSpec((TOP_K, tm), lambda i: (0, i))
    return pl.pallas_call(
        _router_kernel, grid=(n_tok // tm,),
        in_specs=[
            pl.BlockSpec((tm, d), lambda i: (i, 0)),
            pl.BlockSpec((n_exp, d), lambda i: (0, 0)),
            pl.BlockSpec((n_exp, 1), lambda i: (0, 0)),
        ],
        out_specs=[per_tok, per_tok, per_tok, pl.BlockSpec((n_exp, 1), lambda i: (0, 0))],
        out_shape=[jax.ShapeDtypeStruct((TOP_K, n_tok), jnp.int32), jax.ShapeDtypeStruct((TOP_K, n_tok), _F32),
                   jax.ShapeDtypeStruct((TOP_K, n_tok), jnp.int32), jax.ShapeDtypeStruct((n_exp, 1), _F32)],
        compiler_params=_cparams("arbitrary"), name="moe_router",
    )(t, w_router.T.astype(_BF16), router_bias[:, None])


_MOE_TILE_ROWS = 256


def _packed_cols(d):
    return d // 2


def _pack_pairs(y):
    half = y.shape[1] // 2
    bits = lambda v: lax.bitcast_convert_type(v.astype(_BF16).astype(_F32), jnp.uint32)
    return (bits(y[:, :half]) & jnp.uint32(0xFFFF0000)) | (bits(y[:, half:]) >> jnp.uint32(16))


def _unpack_pairs(w):
    hi = lax.bitcast_convert_type(w & jnp.uint32(0xFFFF0000), _F32)
    lo = lax.bitcast_convert_type(w << jnp.uint32(16), _F32)
    return hi, lo


def _norm_mod_pack_kernel(x_ref, g_ref, sh_ref, sc_ref, o_ref, p_ref):
    y = _rms(x_ref[...], g_ref[...]) * (1.0 + sc_ref[...]) + sh_ref[...]
    o_ref[...] = y.astype(o_ref.dtype)
    p_ref[...] = _pack_pairs(y)


def _norm_mod_pack(x, gain, shift, scale):
    t, d = x.shape
    tm = _tile(t, 256)
    row = pl.BlockSpec((1, d), lambda i: (0, 0))
    blk = pl.BlockSpec((tm, d), lambda i: (i, 0))
    return pl.pallas_call(
        _norm_mod_pack_kernel, grid=(t // tm,), in_specs=[blk, row, row, row],
        out_specs=[blk, pl.BlockSpec((tm, _packed_cols(d)), lambda i: (i, 0))],
        out_shape=[jax.ShapeDtypeStruct((t, d), _BF16), jax.ShapeDtypeStruct((t, _packed_cols(d)), jnp.uint32)],
        compiler_params=_cparams("parallel"), name="rmsnorm_modulate_pack",
    )(x, gain[None, :], shift[None, :], scale[None, :])


def _moe_plan(idx8, rank8, counts, tile_rows):
    n_exp = counts.shape[0]
    n_tok = idx8.shape[1]
    cnt = counts[:, 0].astype(jnp.int32)
    padded = (cnt + tile_rows - 1) // tile_rows * tile_rows
    ends = jnp.cumsum(padded)
    starts = ends - padded
    onehot = idx8[:, :, None] == jnp.arange(n_exp, dtype=jnp.int32)
    pos8 = rank8 + jnp.sum(jnp.where(onehot, starts, 0), axis=-1)
    n_tiles = TOP_K * n_tok // tile_rows + n_exp
    tile_row0 = jnp.arange(n_tiles, dtype=jnp.int32) * tile_rows
    tile_expert = jnp.minimum(jnp.sum(tile_row0[:, None] >= ends[None, :], axis=1), n_exp - 1).astype(jnp.int32)
    n_active = (ends[-1:] // tile_rows).astype(jnp.int32)
    last_tile_row = jnp.where(padded > 0, ends - tile_rows, -1).astype(jnp.int32)
    return pos8.T.reshape(-1), tile_expert, n_active, last_tile_row, n_tiles


def _dispatch_kernel(pos_ref, last_ref, na_ref, tp_ref, xs_ref, zero_ref, sem, zsem, *, tile_rows):
    i = pl.program_id(0)
    tb = tp_ref.shape[0]
    n_exp = last_ref.shape[0]
    n_tiles = xs_ref.shape[0] // tile_rows

    @pl.when(i == 0)
    def _():
        zero_ref[...] = jnp.zeros(zero_ref.shape, zero_ref.dtype)

        def fill(row0):
            return pltpu.make_async_copy(zero_ref, xs_ref.at[pl.ds(pl.multiple_of(row0, tile_rows), tile_rows)],
                                         zsem)

        def each(e, start):
            tail = na_ref[0] + e

            def go(row0):
                cp = fill(row0)
                cp.start() if start else cp.wait()

            pl.when(last_ref[e] >= 0)(lambda: go(jnp.maximum(last_ref[e], 0)))
            pl.when(tail < n_tiles)(lambda: go(jnp.minimum(tail, n_tiles - 1) * tile_rows))

        lax.fori_loop(0, n_exp, lambda e, c: (each(e, True), c)[1], 0)
        lax.fori_loop(0, n_exp, lambda e, c: (each(e, False), c)[1], 0)

    def row(r, c):
        base = (i * tb + r) * TOP_K
        for k in range(TOP_K):
            pltpu.make_async_copy(tp_ref.at[pl.ds(r, 1)], xs_ref.at[pl.ds(pos_ref[base + k], 1)], sem).start()
        return c

    lax.fori_loop(0, tb, row, 0, unroll=2)
    for _ in range(TOP_K):
        pltpu.make_async_copy(tp_ref, xs_ref.at[pl.ds(0, tb)], sem).wait()


def _dispatch(tp, pos_flat, last_tile_row, n_active, n_rows, tile_rows):
    n_tok, c = tp.shape
    tb = _tile(n_tok, 512)
    return pl.pallas_call(
        functools.partial(_dispatch_kernel, tile_rows=tile_rows),
        grid_spec=pltpu.PrefetchScalarGridSpec(
            num_scalar_prefetch=3, grid=(n_tok // tb,),
            in_specs=[pl.BlockSpec((tb, c), lambda i, pos, last, na: (i, 0))],
            out_specs=pl.BlockSpec(memory_space=pl.ANY),
            scratch_shapes=[pltpu.VMEM((tile_rows, c), jnp.uint32), pltpu.SemaphoreType.DMA,
                            pltpu.SemaphoreType.DMA],
        ),
        out_shape=jax.ShapeDtypeStruct((n_rows, c), jnp.uint32),
        compiler_params=_cparams("arbitrary"), name="moe_dispatch",
    )(pos_flat, last_tile_row, n_active, tp)


def _grouped_ffn_kernel(te_ref, na_ref, xs_ref, wg_ref, wu_ref, wd_ref, ys_ref, wg_bf, wu_bf, wd_bf):
    i = pl.program_id(0)
    active = i < na_ref[0]
    new_expert = jnp.logical_or(i == 0, te_ref[i] != te_ref[jnp.maximum(i - 1, 0)])

    @pl.when(jnp.logical_and(active, new_expert))
    def _():
        wg_bf[...] = wg_ref[...].astype(_BF16)
        wu_bf[...] = wu_ref[...].astype(_BF16)
        wd_bf[...] = wd_ref[...].astype(_BF16)

    @pl.when(active)
    def _():
        half = wg_bf.shape[0] // 2
        hi, lo = _unpack_pairs(xs_ref[...])
        hi = hi.astype(_BF16)
        lo = lo.astype(_BF16)
        dot = lambda a, b: jnp.dot(a, b, preferred_element_type=_F32)
        hg = dot(hi, wg_bf[:half, :]) + dot(lo, wg_bf[half:, :])
        hu = dot(hi, wu_bf[:half, :]) + dot(lo, wu_bf[half:, :])
        h = (_silu(hg) * hu).astype(_BF16)
        ys_ref[...] = _pack_pairs(dot(h, wd_bf[...]))

    @pl.when(jnp.logical_not(active))
    def _():
        ys_ref[...] = jnp.zeros(ys_ref.shape, ys_ref.dtype)


def _grouped_ffn(xs, tile_expert, n_active, w_gate, w_up, w_down, layer, tile_rows):
    n_rows, c = xs.shape
    _, n_exp, d, ff = w_gate.shape
    n_tiles = n_rows // tile_rows
    row_blk = pl.BlockSpec((tile_rows, c), lambda i, te, na: (jnp.minimum(i, jnp.maximum(na[0] - 1, 0)), 0))
    return pl.pallas_call(
        _grouped_ffn_kernel,
        grid_spec=pltpu.PrefetchScalarGridSpec(
            num_scalar_prefetch=2, grid=(n_tiles,),
            in_specs=[row_blk,
                      pl.BlockSpec((None, None, d, ff), lambda i, te, na: (layer, te[i], 0, 0)),
                      pl.BlockSpec((None, None, d, ff), lambda i, te, na: (layer, te[i], 0, 0)),
                      pl.BlockSpec((None, None, ff, d), lambda i, te, na: (layer, te[i], 0, 0))],
            out_specs=pl.BlockSpec((tile_rows, c), lambda i, te, na: (i, 0)),
            scratch_shapes=[pltpu.VMEM((d, ff), _BF16), pltpu.VMEM((d, ff), _BF16), pltpu.VMEM((ff, d), _BF16)],
        ),
        out_shape=jax.ShapeDtypeStruct((n_rows, c), jnp.uint32),
        compiler_params=_cparams("arbitrary"), name="moe_grouped_ffn",
    )(tile_expert, n_active, xs, w_gate, w_up, w_down)


def _combine_kernel(pos_ref, ys_ref, g8_ref, x_ref, sh_ref, g2_ref, fg_ref, o_ref, buf, sems, *, final_norm):
    i = pl.program_id(0)
    tb, d = x_ref.shape
    half = d // 2

    def gather(block, slot):
        def row(r, c):
            base = (block * tb + r) * TOP_K
            for k in range(TOP_K):
                pltpu.make_async_copy(ys_ref.at[pl.ds(pos_ref[base + k], 1)], buf.at[slot, k, pl.ds(r, 1)],
                                      sems.at[slot]).start()
            return c

        lax.fori_loop(0, tb, row, 0, unroll=2)

    @pl.when(i == 0)
    def _():
        gather(0, 0)

    for parity in range(2):
        @pl.when(jnp.logical_and(i + 1 < pl.num_programs(0), lax.rem(i + 1, 2) == parity))
        def _():
            gather(i + 1, parity)

    slot = lax.rem(i, 2)
    for k in range(TOP_K):
        pltpu.make_async_copy(ys_ref.at[pl.ds(0, tb)], buf.at[slot, k], sems.at[slot]).wait()

    g8 = g8_ref[...]
    acc_hi = jnp.zeros((tb, half), _F32)
    acc_lo = jnp.zeros((tb, half), _F32)
    for k in range(TOP_K):
        hi, lo = _unpack_pairs(buf[slot, k])
        acc_hi = acc_hi + g8[:, k:k + 1] * hi
        acc_lo = acc_lo + g8[:, k:k + 1] * lo
    out_hi = x_ref[:, :half] + g2_ref[:, :half] * (sh_ref[:, :half] + acc_hi)
    out_lo = x_ref[:, half:] + g2_ref[:, half:] * (sh_ref[:, half:] + acc_lo)
    if final_norm:
        ssq = jnp.sum(out_hi * out_hi, axis=-1, keepdims=True) + jnp.sum(out_lo * out_lo, axis=-1, keepdims=True)
        inv = lax.rsqrt(ssq / d + RMS_EPS)
        out_hi = out_hi * inv * fg_ref[:, :half]
        out_lo = out_lo * inv * fg_ref[:, half:]
    o_ref[:, :half] = out_hi
    o_ref[:, half:] = out_lo


def _combine(ys, pos_flat, gate8, x, shared, gate2, final_gain):
    n_tok, d = x.shape
    c = ys.shape[1]
    tb = _tile(n_tok, 128)
    blk = pl.BlockSpec((tb, d), lambda i, pos: (i, 0))
    vec = pl.BlockSpec((1, d), lambda i, pos: (0, 0))
    final_norm = final_gain is not None
    if not final_norm:
        final_gain = jnp.ones((d,), _F32)
    return pl.pallas_call(
        functools.partial(_combine_kernel, final_norm=final_norm),
        grid_spec=pltpu.PrefetchScalarGridSpec(
            num_scalar_prefetch=1, grid=(n_tok // tb,),
            in_specs=[pl.BlockSpec(memory_space=pl.ANY),
                      pl.BlockSpec((tb, TOP_K), lambda i, pos: (i, 0)),
                      blk, blk, vec, vec],
            out_specs=blk,
            scratch_shapes=[pltpu.VMEM((2, TOP_K, tb, c), jnp.uint32), pltpu.SemaphoreType.DMA((2,))],
        ),
        out_shape=jax.ShapeDtypeStruct((n_tok, d), _F32),
        compiler_params=_cparams("arbitrary"), name="moe_combine",
    )(pos_flat, ys, gate8, x, shared, gate2[None, :], final_gain[None, :])


def _ffn_kernel(t_ref, wg_ref, wu_ref, wd_ref, o_ref):
    t = t_ref[...]
    h = _silu(jnp.dot(t, wg_ref[...], preferred_element_type=_F32)) * jnp.dot(t, wu_ref[...],
                                                                               preferred_element_type=_F32)
    o_ref[...] = jnp.dot(h.astype(_BF16), wd_ref[...], preferred_element_type=_F32)


def _ffn(t, w_gate, w_up, w_down):
    n_tok, d = t.shape
    ff = w_gate.shape[1]
    tm = _tile(n_tok, 256)
    blk = pl.BlockSpec((tm, d), lambda i: (i, 0))
    resident = lambda r, c: pl.BlockSpec((r, c), lambda i: (0, 0), pipeline_mode=pl.Buffered(1))
    return pl.pallas_call(
        _ffn_kernel, grid=(n_tok // tm,),
        in_specs=[blk, resident(d, ff), resident(d, ff), resident(ff, d)],
        out_specs=blk,
        out_shape=jax.ShapeDtypeStruct((n_tok, d), _F32),
        compiler_params=_cparams("parallel"), name="shared_ffn",
    )(t, w_gate, w_up, w_down)


def _moe_residual(x, gain, shift, scale, gate2, w_router, router_bias, w_gate, w_up, w_down, layer, ws_gate, ws_up,
                  ws_down, final_gain):
    t, tp = _norm_mod_pack(x, gain, shift, scale)
    idx8, gate8, rank8, counts = _router(t, w_router, router_bias)
    pos_flat, tile_expert, n_active, last_tile_row, n_tiles = _moe_plan(idx8, rank8, counts, _MOE_TILE_ROWS)
    xs = _dispatch(tp, pos_flat, last_tile_row, n_active, n_tiles * _MOE_TILE_ROWS, _MOE_TILE_ROWS)
    ys = _grouped_ffn(xs, tile_expert, n_active, w_gate, w_up, w_down, layer, _MOE_TILE_ROWS)
    shared = _ffn(t, ws_gate.astype(_BF16), ws_up.astype(_BF16), ws_down.astype(_BF16))
    return _combine(ys, pos_flat, gate8.T, x, shared, gate2, final_gain)


def _attention_layer(x, h_c, mods, mods_c, g_mix, w_in, w_out, rpb, q_gain, k_gain):
    shift1, scale1, gate1 = mods[:3]
    cshift1, cscale1 = mods_c[:2]
    n_tok = x.shape[0]
    na_w = NA_HEADS * HEAD_DIM
    gq_w = GQA_Q_HEADS * HEAD_DIM
    gkv_w = GQA_KV_HEADS * HEAD_DIM
    q_cols = na_w + gq_w
    o_nk, o_gk, o_gv, o_end = q_cols, q_cols + 2 * na_w, q_cols + 2 * na_w + gkv_w, w_in.shape[1]
    tn = int(np.gcd.reduce([na_w, gq_w, gkv_w, 512]))
    tiles = lambda lo, hi: list(range(lo // tn, hi // tn))
    w_bf = w_in.astype(_BF16)

    a_x = _norm_mod(x, g_mix, shift1, scale1, _BF16)
    a_c = _norm_mod(h_c, g_mix, cshift1, cscale1, _BF16)

    kv_plain = tiles(o_nk, o_gk) + tiles(o_gv, o_end)
    p_plain = _matmul_cols(a_x, w_bf, tiles(0, na_w) + kv_plain, tn, _BF16)
    c_plain = _matmul_cols(a_c, w_bf, kv_plain, tn, _BF16)

    cos, sin = _rope_tables(n_tok)
    gains = jnp.concatenate([jnp.tile(q_gain, GQA_Q_HEADS), jnp.tile(k_gain, GQA_KV_HEADS)])
    post = jnp.concatenate([jnp.full((gq_w,), HEAD_DIM ** -0.5 * np.log2(np.e), _F32), jnp.ones((gkv_w,), _F32)])
    qk = _matmul_cols_norm_rope(a_x, w_bf, tiles(na_w, q_cols) + tiles(o_gk, o_gv), tn, gains, post, cos,
                                sin)
    ck = _matmul_cols_norm_rope(a_c, w_bf, tiles(o_gk, o_gv), tn, gains[gq_w:], post[gq_w:], None, None)

    o_a = _na_attention(p_plain, c_plain, rpb, NA_HEADS, 0, NA_HEADS, 2 * NA_HEADS, 0, NA_HEADS)
    k_all = jnp.concatenate([qk[:, gq_w:], ck], axis=0)
    v_all = jnp.concatenate([p_plain[:, 3 * na_w:], c_plain[:, 2 * na_w:]], axis=0)
    o_b = _gqa_attention(qk, k_all, v_all, GQA_KV_HEADS, GQA_Q_HEADS)
    return _matmul2_res(o_a, o_b, w_out.astype(_BF16), x, gate1)


def _pool_layer(x, mods, g_mix, w_pool, pool_scale):
    shift1, scale1, gate1 = mods[:3]
    diffs = _pool_diffs(x, g_mix, shift1, scale1)
    return _matmul(diffs, w_pool.astype(_BF16), grouped=True, res=x, gate=gate1, colscale=pool_scale)


def kernel(x, c, ctx, c_ctx, w_mod, b_mod, g_mix, g_ffn, attn_w_in, attn_w_out, na_rpb, q_gain, k_gain, pool_w,
           pool_scale, moe_w_router, moe_router_bias, moe_w_gate, moe_w_up, moe_w_down, moe_ws_gate, moe_ws_up,
           moe_ws_down, g_final):
    batch, seq, d = x.shape
    depth = w_mod.shape[0]
    outs = []
    for bi in range(batch):
        xb = x[bi]
        h_c = ctx[bi]
        conds = jnp.stack([c[bi], c_ctx])
        for i in range(depth):
            j = i // 2
            mod_rows = _adaln(conds, w_mod, i, b_mod[i])
            mods = jnp.split(mod_rows[0], N_MOD)
            mods_c = jnp.split(mod_rows[1], N_MOD)
            if i % 2 == 0:
                xb = _attention_layer(xb, h_c, mods, mods_c, g_mix[i], attn_w_in[j], attn_w_out[j], na_rpb[j],
                                      q_gain[j], k_gain[j])
            else:
                xb = _pool_layer(xb, mods, g_mix[i], pool_w[j], pool_scale[j])
            assert not any(l % 2 == 0 for l in range(i + 1, depth)), "context-stream update not implemented"
            xb = _moe_residual(xb, g_ffn[i], mods[3], mods[4], mods[5], moe_w_router[i], moe_router_bias[i],
                               moe_w_gate, moe_w_up, moe_w_down, i, moe_ws_gate[i], moe_ws_up[i], moe_ws_down[i],
                               g_final if i == depth - 1 else None)
        outs.append(xb)
    return jnp.stack(outs)
```

```python
import functools

import numpy as np
import jax
import jax.numpy as jnp
from jax import lax
from jax.experimental import pallas as pl
from jax.experimental.pallas import tpu as pltpu

GRID_W = 64
HEAD_DIM = 128
NA_HEADS = 16
GQA_Q_HEADS = 16
GQA_KV_HEADS = 4
NA_WIN_H = 8
NA_WIN_W = 16
ROPE_THETA = 10000.0
POOL_WINDOWS = (2, 4, 8, 16)
N_EXPERT_GROUPS = 8
TOPK_GROUPS = 4
TOP_K = 8
ROUTED_SCALE = 2.5
N_MOD = 6
RMS_EPS = 1e-6

_F32 = jnp.float32
_BF16 = jnp.bfloat16
_NEG = -1e30
_VMEM_LIMIT_BYTES = 56 * 1024 * 1024


def _cparams(*sem):
    return pltpu.CompilerParams(dimension_semantics=sem, vmem_limit_bytes=_VMEM_LIMIT_BYTES)


def _tile(n, target, mult=8):
    t = min(n, target) // mult * mult
    while t > mult and n % t:
        t -= mult
    assert t > 0 and n % t == 0, (n, target, mult)
    return t


def _silu(x):
    return x / (1.0 + jnp.exp(-x))


def _adaln_kernel(c_ref, w_ref, b_ref, o_ref, *, kc):
    n_rows, d, _ = c_ref.shape
    bn = w_ref.shape[1]
    accs = [jnp.zeros((1, bn), _F32) for _ in range(n_rows)]
    for k0 in range(0, d, kc):
        w = w_ref[k0:k0 + kc, :]
        for r in range(n_rows):
            cc = _silu(c_ref[r, k0:k0 + kc, :])
            accs[r] = accs[r] + jnp.sum(cc * w, axis=0, keepdims=True)
    o_ref[...] = jnp.concatenate(accs, axis=0) + b_ref[...]


def _adaln(conds, w_mod, layer, b_mod):
    n_rows, d = conds.shape
    n = w_mod.shape[2]
    bn = _tile(n, 512, 128)
    return pl.pallas_call(
        functools.partial(_adaln_kernel, kc=_tile(d, 256)),
        grid=(n // bn,),
        in_specs=[
            pl.BlockSpec((n_rows, d, 1), lambda j: (0, 0, 0)),
            pl.BlockSpec((None, d, bn), lambda j: (layer, 0, j)),
            pl.BlockSpec((1, bn), lambda j: (0, j)),
        ],
        out_specs=pl.BlockSpec((n_rows, bn), lambda j: (0, j)),
        out_shape=jax.ShapeDtypeStruct((n_rows, n), _F32),
        compiler_params=_cparams("arbitrary"),
        name="adaln",
    )(conds[:, :, None], w_mod, b_mod[None, :])


def _rms(x, gain):
    return x * lax.rsqrt(jnp.mean(x * x, axis=-1, keepdims=True) + RMS_EPS) * gain


def _norm_mod_kernel(x_ref, g_ref, sh_ref, sc_ref, o_ref):
    y = _rms(x_ref[...], g_ref[...])
    o_ref[...] = (y * (1.0 + sc_ref[...]) + sh_ref[...]).astype(o_ref.dtype)


def _norm_mod(x, gain, shift, scale, out_dtype):
    t, d = x.shape
    tm = _tile(t, 256)
    row = pl.BlockSpec((1, d), lambda i: (0, 0))
    blk = pl.BlockSpec((tm, d), lambda i: (i, 0))
    return pl.pallas_call(
        _norm_mod_kernel, grid=(t // tm,), in_specs=[blk, row, row, row], out_specs=blk,
        out_shape=jax.ShapeDtypeStruct((t, d), out_dtype),
        compiler_params=_cparams("parallel"), name="rmsnorm_modulate",
    )(x, gain[None, :], shift[None, :], scale[None, :])


def _mm_kernel(a_ref, w_ref, o_ref):
    o_ref[...] = jnp.dot(a_ref[...], w_ref[...], preferred_element_type=_F32).astype(o_ref.dtype)


def _mm_res_kernel(a_ref, w_ref, res_ref, gate_ref, cs_ref, o_ref):
    y = jnp.dot(a_ref[...], w_ref[...], preferred_element_type=_F32) * cs_ref[...]
    o_ref[...] = res_ref[...] + gate_ref[...] * y


def _matmul(a, w, out_dtype=_F32, *, grouped=False, res=None, gate=None, colscale=None, tm_target=512,
            tn_target=1024):
    m = a.shape[0]
    if grouped:
        g, kk, tn = w.shape
        n = g * tn
        a_spec = lambda tm: pl.BlockSpec((tm, kk), lambda j, i: (i, j))
        w_spec = pl.BlockSpec((None, kk, tn), lambda j, i: (j, 0, 0))
    else:
        kk, n = w.shape
        tn = _tile(n, tn_target, 128)
        a_spec = lambda tm: pl.BlockSpec((tm, kk), lambda j, i: (i, 0))
        w_spec = pl.BlockSpec((kk, tn), lambda j, i: (0, j))
    tm = _tile(m, tm_target)
    o_spec = pl.BlockSpec((tm, tn), lambda j, i: (i, j))
    row = pl.BlockSpec((1, tn), lambda j, i: (0, j))
    grid = (n // tn, m // tm)
    if res is None:
        return pl.pallas_call(
            _mm_kernel, grid=grid, in_specs=[a_spec(tm), w_spec], out_specs=o_spec,
            out_shape=jax.ShapeDtypeStruct((m, n), out_dtype),
            compiler_params=_cparams("parallel", "parallel"), name="matmul",
        )(a, w)
    if colscale is None:
        colscale = jnp.ones((n,), _F32)
    return pl.pallas_call(
        _mm_res_kernel, grid=grid, in_specs=[a_spec(tm), w_spec, o_spec, row, row], out_specs=o_spec,
        out_shape=jax.ShapeDtypeStruct((m, n), _F32),
        compiler_params=_cparams("parallel", "parallel"), name="matmul_gated_residual",
    )(a, w, res, gate[None, :], colscale[None, :])


def _mm2_res_kernel(a1_ref, a2_ref, w1_ref, w2_ref, res_ref, gate_ref, o_ref):
    y = (jnp.dot(a1_ref[...], w1_ref[...], preferred_element_type=_F32)
         + jnp.dot(a2_ref[...], w2_ref[...], preferred_element_type=_F32))
    o_ref[...] = res_ref[...] + gate_ref[...] * y


def _matmul2_res(a1, a2, w, res, gate):
    m, k1 = a1.shape
    assert a2.shape == (m, k1) and w.shape[0] == 2 * k1
    n = w.shape[1]
    tm = _tile(m, 512)
    tn = _tile(n, 1024, 128)
    a_spec = pl.BlockSpec((tm, k1), lambda j, i: (i, 0))
    o_spec = pl.BlockSpec((tm, tn), lambda j, i: (i, j))
    return pl.pallas_call(
        _mm2_res_kernel, grid=(n // tn, m // tm),
        in_specs=[a_spec, a_spec, pl.BlockSpec((k1, tn), lambda j, i: (0, j)),
                  pl.BlockSpec((k1, tn), lambda j, i: (1, j)), o_spec, pl.BlockSpec((1, tn), lambda j, i: (0, j))],
        out_specs=o_spec, out_shape=jax.ShapeDtypeStruct((m, n), _F32),
        compiler_params=_cparams("parallel", "parallel"), name="matmul2_gated_residual",
    )(a1, a2, w, w, res, gate[None, :])


def _mm_cols_kernel(tab_ref, a_ref, w_ref, o_ref):
    del tab_ref
    _mm_kernel(a_ref, w_ref, o_ref)


def _matmul_cols(a, w, col_tiles, tn, out_dtype):
    m, kk = a.shape
    tm = _tile(m, 512)
    n = len(col_tiles) * tn
    return pl.pallas_call(
        _mm_cols_kernel,
        grid_spec=pltpu.PrefetchScalarGridSpec(
            num_scalar_prefetch=1, grid=(len(col_tiles), m // tm),
            in_specs=[pl.BlockSpec((tm, kk), lambda j, i, tab: (i, 0)),
                      pl.BlockSpec((kk, tn), lambda j, i, tab: (0, tab[j]))],
            out_specs=pl.BlockSpec((tm, tn), lambda j, i, tab: (i, j)),
        ),
        out_shape=jax.ShapeDtypeStruct((m, n), out_dtype),
        compiler_params=_cparams("parallel", "parallel"), name="matmul_cols",
    )(jnp.asarray(col_tiles, jnp.int32), a, w)


def _mm_cols_norm_rope_kernel(tab_ref, a_ref, w_ref, gain_ref, post_ref, cos_ref, sin_ref, o_ref, *, rope):
    del tab_ref
    x = jnp.dot(a_ref[...], w_ref[...], preferred_element_type=_F32)
    if rope:
        cos = cos_ref[...]
        sin = sin_ref[...]
        lane = lax.broadcasted_iota(jnp.int32, cos.shape, 1)
        first_half = (lane % (HEAD_DIM // 2)) < (HEAD_DIM // 4)
    for h in range(x.shape[1] // HEAD_DIM):
        sl = slice(h * HEAD_DIM, (h + 1) * HEAD_DIM)
        y = _rms(x[:, sl], gain_ref[:, sl])
        if rope:
            partner = jnp.where(first_half, pltpu.roll(y, HEAD_DIM - HEAD_DIM // 4, 1),
                                pltpu.roll(y, HEAD_DIM // 4, 1))
            y = y * cos + partner * sin
        o_ref[:, sl] = (y * post_ref[:, sl]).astype(o_ref.dtype)


def _matmul_cols_norm_rope(a, w, col_tiles, tn, gains, post_scale, cos, sin_signed):
    m, kk = a.shape
    tm = _tile(m, 512)
    n = len(col_tiles) * tn
    rope = cos is not None
    if not rope:
        cos = jnp.zeros((m, HEAD_DIM), _F32)
        sin_signed = cos
    vec = pl.BlockSpec((1, tn), lambda j, i, tab: (0, j))
    table = pl.BlockSpec((tm, HEAD_DIM), lambda j, i, tab: (i, 0))
    return pl.pallas_call(
        functools.partial(_mm_cols_norm_rope_kernel, rope=rope),
        grid_spec=pltpu.PrefetchScalarGridSpec(
            num_scalar_prefetch=1, grid=(len(col_tiles), m // tm),
            in_specs=[pl.BlockSpec((tm, kk), lambda j, i, tab: (i, 0)),
                      pl.BlockSpec((kk, tn), lambda j, i, tab: (0, tab[j])), vec, vec, table, table],
            out_specs=pl.BlockSpec((tm, tn), lambda j, i, tab: (i, j)),
        ),
        out_shape=jax.ShapeDtypeStruct((m, n), _BF16),
        compiler_params=_cparams("parallel", "parallel"), name="matmul_cols_norm_rope",
    )(jnp.asarray(col_tiles, jnp.int32), a, w, gains[None, :], post_scale[None, :], cos, sin_signed)


def _rope_tables(n):
    axis_dim = HEAD_DIM // 2
    t = jnp.arange(n)
    row = (t // GRID_W).astype(_F32)
    col = (t % GRID_W).astype(_F32)
    inv = ROPE_THETA ** (-jnp.arange(0, axis_dim, 2, dtype=_F32) / axis_dim)
    ang_r = row[:, None] * inv
    ang_c = col[:, None] * inv
    cos = jnp.concatenate([jnp.cos(ang_r)] * 2 + [jnp.cos(ang_c)] * 2, axis=-1)
    sin = jnp.concatenate([-jnp.sin(ang_r), jnp.sin(ang_r), -jnp.sin(ang_c), jnp.sin(ang_c)], axis=-1)
    return cos, sin


def _dot_nt(a, b):
    return lax.dot_general(a, b, (((1,), (1,)), ((), ())), preferred_element_type=_F32)


_GQA_Q_BLOCK = 256
_GQA_KEY_BLOCK = 8320


def _gqa_kernel(q_ref, k_ref, v_ref, o_ref, m_ref, acc_ref, *, tk):
    group = q_ref.shape[1] // HEAD_DIM
    m_ref[...] = jnp.full(m_ref.shape, -jnp.inf, _F32)
    acc_ref[...] = jnp.zeros(acc_ref.shape, _F32)

    def body(j, carry):
        start = pl.multiple_of(j * tk, tk)
        for g in range(group):
            s = _dot_nt(q_ref[:, g * HEAD_DIM:(g + 1) * HEAD_DIM], k_ref[pl.ds(start, tk), :])
            m_prev = m_ref[g]
            m_new = jnp.maximum(m_prev, jnp.max(s, axis=-1, keepdims=True))
            p = jnp.exp2(s - m_new)
            acc_ref[g] = jnp.exp2(m_prev - m_new) * acc_ref[g] + jnp.dot(
                p.astype(_BF16), v_ref[pl.ds(start, tk), :], preferred_element_type=_F32)
            m_ref[g] = m_new
        return carry

    lax.fori_loop(0, k_ref.shape[0] // tk, body, 0)
    for g in range(group):
        acc = acc_ref[g]
        o_ref[:, g * HEAD_DIM:(g + 1) * HEAD_DIM] = (acc[:, :HEAD_DIM] / acc[:, HEAD_DIM:HEAD_DIM + 1]).astype(
            o_ref.dtype)


def _gqa_attention(q, k, v, n_kv_heads, n_q_heads):
    l = q.shape[0]
    qw = n_q_heads * HEAD_DIM
    s = k.shape[0]
    gw = qw // n_kv_heads
    tq = _tile(l, _GQA_Q_BLOCK)
    tk = _tile(s, _GQA_KEY_BLOCK, 128)
    group = gw // HEAD_DIM
    ones_col = jnp.zeros((s, n_kv_heads, HEAD_DIM), v.dtype).at[:, :, 0].set(1)
    v_aug = jnp.concatenate([v.reshape(s, n_kv_heads, HEAD_DIM), ones_col], axis=-1).reshape(s, -1)
    return pl.pallas_call(
        functools.partial(_gqa_kernel, tk=tk),
        grid=(n_kv_heads, l // tq),
        in_specs=[
            pl.BlockSpec((tq, gw), lambda h, i: (i, h)),
            pl.BlockSpec((s, HEAD_DIM), lambda h, i: (0, h), pipeline_mode=pl.Buffered(1)),
            pl.BlockSpec((s, 2 * HEAD_DIM), lambda h, i: (0, h), pipeline_mode=pl.Buffered(1)),
        ],
        out_specs=pl.BlockSpec((tq, gw), lambda h, i: (i, h)),
        out_shape=jax.ShapeDtypeStruct((l, qw), _BF16),
        scratch_shapes=[pltpu.VMEM((group, tq, 1), _F32), pltpu.VMEM((group, tq, 2 * HEAD_DIM), _F32)],
        compiler_params=_cparams("arbitrary", "arbitrary"), name="gqa_attention",
    )(q, k, v_aug)


_NA_QROWS = 8
_NA_KROWS = 16
_NA_CHAINS = 4


def _na_patterns():
    first = dict(delta=0, a=[max(i - NA_WIN_H // 2, 0) for i in range(_NA_QROWS)])
    inner = dict(delta=-(NA_WIN_H // 2), a=list(range(_NA_QROWS)))
    last = dict(delta=-(_NA_KROWS - _NA_QROWS),
                a=[min(i + NA_WIN_H // 2, _NA_KROWS - NA_WIN_H) for i in range(_NA_QROWS)])
    return first, inner, last


def _na_kernel(q_ref, k_ref, v_ref, kc_ref, vc_ref, tb_ref, o_ref, bias_head, bias_inner, bias_tail, *, n_blocks,
               scale):
    step = pl.program_id(1)
    n_steps = n_blocks // _NA_CHAINS
    w = GRID_W
    tq = _NA_QROWS * w
    lane_lo = lax.broadcasted_iota(jnp.int32, (w, 2 * w), 1) < w

    def build_bias(bias_ref, pat):
        for i in range(_NA_QROWS):
            for jj in range(_NA_KROWS // 2):
                j0 = 2 * jj
                in0 = 0 <= j0 - pat["a"][i] < NA_WIN_H
                in1 = 0 <= j0 + 1 - pat["a"][i] < NA_WIN_H
                rel0 = pat["delta"] + j0 - i + NA_WIN_H - 1
                if not (in0 or in1):
                    tile = jnp.full((w, 2 * w), _NEG, _F32)
                else:
                    tile = tb_ref[rel0 + 1]
                    if not in1:
                        tile = jnp.where(lane_lo, tile, _NEG)
                    elif not in0:
                        tile = jnp.where(lane_lo, _NEG, tile)
                bias_ref[i * w:(i + 1) * w, j0 * w:(j0 + 2) * w] = tile

    first, inner, last = _na_patterns()
    pl.when(step == 0)(functools.partial(build_bias, bias_head, first))
    pl.when(step == 0)(functools.partial(build_bias, bias_inner, inner))
    pl.when(step == 0)(functools.partial(build_bias, bias_tail, inner))
    pl.when(step == 1)(functools.partial(build_bias, bias_head, inner))
    pl.when(step == n_steps - 1)(functools.partial(build_bias, bias_tail, last))

    kc = kc_ref[...]
    vc = vc_ref[...]
    chain_bias = [bias_head] + [bias_inner] * (_NA_CHAINS - 2) + [bias_tail]
    for half, bias_ref in enumerate(chain_bias):
        b = _NA_CHAINS * step + half
        slab_row = jnp.where(b == 0, 0, jnp.where(b == n_blocks - 1, (n_blocks - 2) * _NA_QROWS,
                                                  b * _NA_QROWS - NA_WIN_H // 2))
        start = pl.multiple_of(slab_row * w, w)
        q = q_ref[half * tq:(half + 1) * tq, :]
        s_loc = _dot_nt(q, k_ref[pl.ds(start, _NA_KROWS * w), :]) * scale + bias_ref[...]
        s_ctx = _dot_nt(q, kc) * scale
        m = jnp.maximum(jnp.max(s_loc, axis=-1, keepdims=True), jnp.max(s_ctx, axis=-1, keepdims=True))
        p_loc = jnp.exp(s_loc - m)
        p_ctx = jnp.exp(s_ctx - m)
        denom = jnp.sum(p_loc, axis=-1, keepdims=True) + jnp.sum(p_ctx, axis=-1, keepdims=True)
        o = (jnp.dot(p_loc.astype(_BF16), v_ref[pl.ds(start, _NA_KROWS * w), :], preferred_element_type=_F32)
             + jnp.dot(p_ctx.astype(_BF16), vc, preferred_element_type=_F32))
        o_ref[half * tq:(half + 1) * tq, :] = (o / denom).astype(o_ref.dtype)


def _na_bias_table(rpb):
    qc = np.arange(GRID_W)
    win_start = np.clip(qc - NA_WIN_W // 2, 0, GRID_W - NA_WIN_W)
    rel = qc[None, :] - qc[:, None]
    in_win = (qc[None, :] >= win_start[:, None]) & (qc[None, :] < win_start[:, None] + NA_WIN_W)
    rel_idx = np.clip(rel + NA_WIN_W - 1, 0, 2 * NA_WIN_W - 2)
    toe = jnp.where(in_win[None, None], rpb[:, :, rel_idx], _NEG).astype(_F32)
    neg = jnp.full_like(toe[:, :1], _NEG)
    lo = jnp.concatenate([neg, toe], axis=1)
    hi = jnp.concatenate([toe, neg], axis=1)
    return jnp.concatenate([lo, hi], axis=-1)


def _na_attention(px, pc, rpb, n_heads, q_head0, k_head0, v_head0, ck_head0, cv_head0):
    l = px.shape[0]
    lc = pc.shape[0]
    rows = l // GRID_W
    n_blocks = rows // _NA_QROWS
    assert rows % (_NA_CHAINS * _NA_QROWS) == 0 and n_blocks >= 3
    tq = _NA_CHAINS * _NA_QROWS * GRID_W
    tb = _na_bias_table(rpb)
    head_blk = lambda n, h0: pl.BlockSpec((n, HEAD_DIM), lambda h, b: (0, h0 + h))
    bias_scratch = pltpu.VMEM((_NA_QROWS * GRID_W, _NA_KROWS * GRID_W), _F32)
    return pl.pallas_call(
        functools.partial(_na_kernel, n_blocks=n_blocks, scale=HEAD_DIM ** -0.5),
        grid=(n_heads, n_blocks // _NA_CHAINS),
        in_specs=[
            pl.BlockSpec((tq, HEAD_DIM), lambda h, b: (b, q_head0 + h)),
            head_blk(l, k_head0), head_blk(l, v_head0), head_blk(lc, ck_head0), head_blk(lc, cv_head0),
            pl.BlockSpec((None, 2 * NA_WIN_H, GRID_W, 2 * GRID_W), lambda h, b: (h, 0, 0, 0)),
        ],
        out_specs=pl.BlockSpec((tq, HEAD_DIM), lambda h, b: (b, h)),
        out_shape=jax.ShapeDtypeStruct((l, n_heads * HEAD_DIM), _BF16),
        scratch_shapes=[bias_scratch, bias_scratch, bias_scratch],
        compiler_params=_cparams("arbitrary", "arbitrary"), name="na_attention",
    )(px, px, px, pc, pc, tb)


_POOL_HALO = 16


def _pool_kernel(prev_ref, cur_ref, next_ref, g_ref, sh_ref, sc_ref, o_ref, *, n_tokens):
    i = pl.program_id(0)
    tm, d = cur_ref.shape
    dg = d // len(POOL_WINDOWS)
    x = jnp.concatenate([prev_ref[...], cur_ref[...], next_ref[...]], axis=0)
    a = _rms(x, g_ref[...]) * (1.0 + sc_ref[...]) + sh_ref[...]
    pos = i * tm - _POOL_HALO + lax.broadcasted_iota(jnp.int32, (tm + 2 * _POOL_HALO, 1), 0)
    a = jnp.where((pos >= 0) & (pos < n_tokens), a, 0.0)
    t = i * tm + lax.broadcasted_iota(jnp.int32, (tm, 1), 0)
    n_ext = tm + 2 * _POOL_HALO
    for g, w in enumerate(POOL_WINDOWS):
        ag = a[:, g * dg:(g + 1) * dg]
        run, span = ag, 1
        while span < w:
            run = run[:n_ext - 2 * span + 1, :] + run[span:n_ext - span + 1, :]
            span *= 2
        tot = run[_POOL_HALO - w // 2:_POOL_HALO - w // 2 + tm, :]
        cnt = jnp.minimum(t + (w - w // 2), n_tokens) - jnp.maximum(t - w // 2, 0)
        diff = tot / cnt.astype(_F32) - ag[_POOL_HALO:_POOL_HALO + tm, :]
        o_ref[:, g * dg:(g + 1) * dg] = diff.astype(o_ref.dtype)


def _pool_diffs(x, gain, shift, scale):
    t, d = x.shape
    assert all(w & (w - 1) == 0 and w // 2 <= _POOL_HALO for w in POOL_WINDOWS)
    tm = _tile(t, 256, _POOL_HALO)
    hb = tm // _POOL_HALO
    n_halo_blocks = t // _POOL_HALO
    row = pl.BlockSpec((1, d), lambda i: (0, 0))
    return pl.pallas_call(
        functools.partial(_pool_kernel, n_tokens=t), grid=(t // tm,),
        in_specs=[
            pl.BlockSpec((_POOL_HALO, d), lambda i: (jnp.maximum(i * hb - 1, 0), 0)),
            pl.BlockSpec((tm, d), lambda i: (i, 0)),
            pl.BlockSpec((_POOL_HALO, d), lambda i: (jnp.minimum((i + 1) * hb, n_halo_blocks - 1), 0)),
            row, row, row,
        ],
        out_specs=pl.BlockSpec((tm, d), lambda i: (i, 0)),
        out_shape=jax.ShapeDtypeStruct((t, d), _BF16),
        compiler_params=_cparams("parallel"), name="pool_diffs",
    )(x, x, x, gain[None, :], shift[None, :], scale[None, :])


def _router_kernel(t_ref, w_ref, b_ref, idx_ref, gate_ref, rank_ref, cnt_ref):
    n_exp = w_ref.shape[0]
    tm = t_ref.shape[0]
    per_group = n_exp // N_EXPERT_GROUPS

    @pl.when(pl.program_id(0) == 0)
    def _():
        cnt_ref[...] = jnp.zeros(cnt_ref.shape, _F32)

    logits = _dot_nt(w_ref[...], t_ref[...])
    scores = 1.0 / (1.0 + jnp.exp(-logits))
    biased = scores + b_ref[...]

    def first_max(vals):
        idx = lax.broadcasted_iota(jnp.int32, vals.shape, 0).astype(_F32)
        m = jnp.max(vals, axis=0, keepdims=True)
        first = jnp.min(jnp.where(vals == m, idx, float(vals.shape[0])), axis=0, keepdims=True)
        return m, idx == first

    def take_top(vals, k):
        hits = []
        for _ in range(k):
            _, hit = first_max(vals)
            hits.append(hit)
            vals = jnp.where(hit, -jnp.inf, vals)
        return hits

    def union(hits):
        sel = jnp.zeros(hits[0].shape, _F32)
        for hit in hits:
            sel = jnp.where(hit, 1.0, sel)
        return sel

    group_scores = []
    for g in range(N_EXPERT_GROUPS):
        v = biased[g * per_group:(g + 1) * per_group, :]
        m1, hit = first_max(v)
        m2 = jnp.max(jnp.where(hit, -jnp.inf, v), axis=0, keepdims=True)
        group_scores.append(m1 + m2)
    gsel = union(take_top(jnp.concatenate(group_scores, axis=0), TOPK_GROUPS))
    keep = jnp.concatenate([jnp.broadcast_to(gsel[g:g + 1, :], (per_group, tm)) for g in range(N_EXPERT_GROUPS)],
                           axis=0)
    hits = take_top(jnp.where(keep > 0.0, biased, -jnp.inf), TOP_K)
    esel = union(hits)
    wsel = esel * scores
    gates = wsel / jnp.sum(wsel, axis=0, keepdims=True) * ROUTED_SCALE

    earlier = (lax.broadcasted_iota(jnp.int32, (tm, tm), 0) < lax.broadcasted_iota(jnp.int32, (tm, tm), 1))
    rank = cnt_ref[...] + jnp.dot(esel.astype(_BF16), jnp.where(earlier, 1.0, 0.0).astype(_BF16),
                                  preferred_element_type=_F32)
    cnt_ref[...] += jnp.sum(esel, axis=1, keepdims=True)

    eidx = lax.broadcasted_iota(jnp.int32, (n_exp, tm), 0).astype(_F32)
    pick = lambda hit, v: jnp.sum(jnp.where(hit, v, 0.0), axis=0, keepdims=True)
    idx_ref[...] = jnp.concatenate([pick(h, eidx) for h in hits], axis=0).astype(jnp.int32)
    gate_ref[...] = jnp.concatenate([pick(h, gates) for h in hits], axis=0)
    rank_ref[...] = jnp.concatenate([pick(h, rank) for h in hits], axis=0).astype(jnp.int32)


def _router(t, w_router, router_bias):
    n_tok, d = t.shape
    n_exp = w_router.shape[1]
    tm = _tile(n_tok, 512, 128)
    per_tok = pl.BlockSpec((TOP_K, tm), lambda i: (0, i))
    return pl.pallas_call(
        _router_kernel, grid=(n_tok // tm,),
        in_specs=[
            pl.BlockSpec((tm, d), lambda i: (i, 0)),
            pl.BlockSpec((n_exp, d), lambda i: (0, 0)),
            pl.BlockSpec((n_exp, 1), lambda i: (0, 0)),
        ],
        out_specs=[per_tok, per_tok, per_tok, pl.BlockSpec((n_exp, 1), lambda i: (0, 0))],
        out_shape=[jax.ShapeDtypeStruct((TOP_K, n_tok), jnp.int32), jax.ShapeDtypeStruct((TOP_K, n_tok), _F32),
                   jax.ShapeDtypeStruct((TOP_K, n_tok), jnp.int32), jax.ShapeDtypeStruct((n_exp, 1), _F32)],
        compiler_params=_cparams("arbitrary"), name="moe_router",
    )(t, w_router.T.astype(_BF16), router_bias[:, None])


_MOE_TILE_ROWS = 256


def _packed_cols(d):
    return d // 2


def _pack_pairs(y):
    half = y.shape[1] // 2
    bits = lambda v: lax.bitcast_convert_type(v.astype(_BF16).astype(_F32), jnp.uint32)
    return (bits(y[:, :half]) & jnp.uint32(0xFFFF0000)) | (bits(y[:, half:]) >> jnp.uint32(16))


def _unpack_pairs(w):
    hi = lax.bitcast_convert_type(w & jnp.uint32(0xFFFF0000), _F32)
    lo = lax.bitcast_convert_type(w << jnp.uint32(16), _F32)
    return hi, lo


def _norm_mod_pack_kernel(x_ref, g_ref, sh_ref, sc_ref, o_ref, p_ref):
    y = _rms(x_ref[...], g_ref[...]) * (1.0 + sc_ref[...]) + sh_ref[...]
    o_ref[...] = y.astype(o_ref.dtype)
    p_ref[...] = _pack_pairs(y)


def _norm_mod_pack(x, gain, shift, scale):
    t, d = x.shape
    tm = _tile(t, 256)
    row = pl.BlockSpec((1, d), lambda i: (0, 0))
    blk = pl.BlockSpec((tm, d), lambda i: (i, 0))
    return pl.pallas_call(
        _norm_mod_pack_kernel, grid=(t // tm,), in_specs=[blk, row, row, row],
        out_specs=[blk, pl.BlockSpec((tm, _packed_cols(d)), lambda i: (i, 0))],
        out_shape=[jax.ShapeDtypeStruct((t, d), _BF16), jax.ShapeDtypeStruct((t, _packed_cols(d)), jnp.uint32)],
        compiler_params=_cparams("parallel"), name="rmsnorm_modulate_pack",
    )(x, gain[None, :], shift[None, :], scale[None, :])


def _moe_plan(idx8, rank8, counts, tile_rows):
    n_exp = counts.shape[0]
    n_tok = idx8.shape[1]
    cnt = counts[:, 0].astype(jnp.int32)
    padded = (cnt + tile_rows - 1) // tile_rows * tile_rows
    ends = jnp.cumsum(padded)
    starts = ends - padded
    onehot = idx8[:, :, None] == jnp.arange(n_exp, dtype=jnp.int32)
    pos8 = rank8 + jnp.sum(jnp.where(onehot, starts, 0), axis=-1)
    n_tiles = TOP_K * n_tok // tile_rows + n_exp
    tile_row0 = jnp.arange(n_tiles, dtype=jnp.int32) * tile_rows
    tile_expert = jnp.minimum(jnp.sum(tile_row0[:, None] >= ends[None, :], axis=1), n_exp - 1).astype(jnp.int32)
    n_active = (ends[-1:] // tile_rows).astype(jnp.int32)
    last_tile_row = jnp.where(padded > 0, ends - tile_rows, -1).astype(jnp.int32)
    return pos8.T.reshape(-1), tile_expert, n_active, last_tile_row, n_tiles


def _dispatch_kernel(pos_ref, last_ref, na_ref, tp_ref, xs_ref, zero_ref, sem, zsem, *, tile_rows):
    i = pl.program_id(0)
    tb = tp_ref.shape[0]
    n_exp = last_ref.shape[0]
    n_tiles = xs_ref.shape[0] // tile_rows

    @pl.when(i == 0)
    def _():
        zero_ref[...] = jnp.zeros(zero_ref.shape, zero_ref.dtype)

        def fill(row0):
            return pltpu.make_async_copy(zero_ref, xs_ref.at[pl.ds(pl.multiple_of(row0, tile_rows), tile_rows)],
                                         zsem)

        def each(e, start):
            tail = na_ref[0] + e

            def go(row0):
                cp = fill(row0)
                cp.start() if start else cp.wait()

            pl.when(last_ref[e] >= 0)(lambda: go(jnp.maximum(last_ref[e], 0)))
            pl.when(tail < n_tiles)(lambda: go(jnp.minimum(tail, n_tiles - 1) * tile_rows))

        lax.fori_loop(0, n_exp, lambda e, c: (each(e, True), c)[1], 0)
        lax.fori_loop(0, n_exp, lambda e, c: (each(e, False), c)[1], 0)

    def row(r, c):
        base = (i * tb + r) * TOP_K
        for k in range(TOP_K):
            pltpu.make_async_copy(tp_ref.at[pl.ds(r, 1)], xs_ref.at[pl.ds(pos_ref[base + k], 1)], sem).start(
                priority=k % 2)
        return c

    lax.fori_loop(0, tb, row, 0, unroll=2)
    for _ in range(TOP_K):
        pltpu.make_async_copy(tp_ref, xs_ref.at[pl.ds(0, tb)], sem).wait()


def _dispatch(tp, pos_flat, last_tile_row, n_active, n_rows, tile_rows):
    n_tok, c = tp.shape
    tb = _tile(n_tok, 512)
    return pl.pallas_call(
        functools.partial(_dispatch_kernel, tile_rows=tile_rows),
        grid_spec=pltpu.PrefetchScalarGridSpec(
            num_scalar_prefetch=3, grid=(n_tok // tb,),
            in_specs=[pl.BlockSpec((tb, c), lambda i, pos, last, na: (i, 0))],
            out_specs=pl.BlockSpec(memory_space=pl.ANY),
            scratch_shapes=[pltpu.VMEM((tile_rows, c), jnp.uint32), pltpu.SemaphoreType.DMA,
                            pltpu.SemaphoreType.DMA],
        ),
        out_shape=jax.ShapeDtypeStruct((n_rows, c), jnp.uint32),
        compiler_params=_cparams("arbitrary"), name="moe_dispatch",
    )(pos_flat, last_tile_row, n_active, tp)


def _grouped_ffn_kernel(te_ref, na_ref, xs_ref, wg_ref, wu_ref, wd_ref, ys_ref, wg_bf, wu_bf, wd_bf):
    i = pl.program_id(0)
    active = i < na_ref[0]
    new_expert = jnp.logical_or(i == 0, te_ref[i] != te_ref[jnp.maximum(i - 1, 0)])

    @pl.when(jnp.logical_and(active, new_expert))
    def _():
        wg_bf[...] = wg_ref[...].astype(_BF16)
        wu_bf[...] = wu_ref[...].astype(_BF16)
        wd_bf[...] = wd_ref[...].astype(_BF16)

    @pl.when(active)
    def _():
        half = wg_bf.shape[0] // 2
        hi, lo = _unpack_pairs(xs_ref[...])
        hi = hi.astype(_BF16)
        lo = lo.astype(_BF16)
        dot = lambda a, b: jnp.dot(a, b, preferred_element_type=_F32)
        hg = dot(hi, wg_bf[:half, :]) + dot(lo, wg_bf[half:, :])
        hu = dot(hi, wu_bf[:half, :]) + dot(lo, wu_bf[half:, :])
        h = (_silu(hg) * hu).astype(_BF16)
        ys_ref[...] = _pack_pairs(dot(h, wd_bf[...]))

    @pl.when(jnp.logical_not(active))
    def _():
        ys_ref[...] = jnp.zeros(ys_ref.shape, ys_ref.dtype)


def _grouped_ffn(xs, tile_expert, n_active, w_gate, w_up, w_down, layer, tile_rows):
    n_rows, c = xs.shape
    _, n_exp, d, ff = w_gate.shape
    n_tiles = n_rows // tile_rows
    row_blk = pl.BlockSpec((tile_rows, c), lambda i, te, na: (jnp.minimum(i, jnp.maximum(na[0] - 1, 0)), 0))
    return pl.pallas_call(
        _grouped_ffn_kernel,
        grid_spec=pltpu.PrefetchScalarGridSpec(
            num_scalar_prefetch=2, grid=(n_tiles,),
            in_specs=[row_blk,
                      pl.BlockSpec((None, None, d, ff), lambda i, te, na: (layer, te[i], 0, 0)),
                      pl.BlockSpec((None, None, d, ff), lambda i, te, na: (layer, te[i], 0, 0)),
                      pl.BlockSpec((None, None, ff, d), lambda i, te, na: (layer, te[i], 0, 0))],
            out_specs=pl.BlockSpec((tile_rows, c), lambda i, te, na: (i, 0)),
            scratch_shapes=[pltpu.VMEM((d, ff), _BF16), pltpu.VMEM((d, ff), _BF16), pltpu.VMEM((ff, d), _BF16)],
        ),
        out_shape=jax.ShapeDtypeStruct((n_rows, c), jnp.uint32),
        compiler_params=_cparams("arbitrary"), name="moe_grouped_ffn",
    )(tile_expert, n_active, xs, w_gate, w_up, w_down)


def _combine_kernel(pos_ref, ys_ref, g8_ref, x_ref, sh_ref, g2_ref, fg_ref, o_ref, buf, sems, *, final_norm):
    i = pl.program_id(0)
    tb, d = x_ref.shape
    half = d // 2

    def gather(block, slot):
        def row(r, c):
            base = (block * tb + r) * TOP_K
            for k in range(TOP_K):
                pltpu.make_async_copy(ys_ref.at[pl.ds(pos_ref[base + k], 1)], buf.at[slot, k, pl.ds(r, 1)],
                                      sems.at[slot]).start(priority=k % 2)
            return c

        lax.fori_loop(0, tb, row, 0, unroll=2)

    @pl.when(i == 0)
    def _():
        gather(0, 0)

    for parity in range(2):
        @pl.when(jnp.logical_and(i + 1 < pl.num_programs(0), lax.rem(i + 1, 2) == parity))
        def _():
            gather(i + 1, parity)

    slot = lax.rem(i, 2)
    for k in range(TOP_K):
        pltpu.make_async_copy(ys_ref.at[pl.ds(0, tb)], buf.at[slot, k], sems.at[slot]).wait()

    g8 = g8_ref[...]
    acc_hi = jnp.zeros((tb, half), _F32)
    acc_lo = jnp.zeros((tb, half), _F32)
    for k in range(TOP_K):
        hi, lo = _unpack_pairs(buf[slot, k])
        acc_hi = acc_hi + g8[:, k:k + 1] * hi
        acc_lo = acc_lo + g8[:, k:k + 1] * lo
    out_hi = x_ref[:, :half] + g2_ref[:, :half] * (sh_ref[:, :half] + acc_hi)
    out_lo = x_ref[:, half:] + g2_ref[:, half:] * (sh_ref[:, half:] + acc_lo)
    if final_norm:
        ssq = jnp.sum(out_hi * out_hi, axis=-1, keepdims=True) + jnp.sum(out_lo * out_lo, axis=-1, keepdims=True)
        inv = lax.rsqrt(ssq / d + RMS_EPS)
        out_hi = out_hi * inv * fg_ref[:, :half]
        out_lo = out_lo * inv * fg_ref[:, half:]
    o_ref[:, :half] = out_hi
    o_ref[:, half:] = out_lo


def _combine(ys, pos_flat, gate8, x, shared, gate2, final_gain):
    n_tok, d = x.shape
    c = ys.shape[1]
    tb = _tile(n_tok, 128)
    blk = pl.BlockSpec((tb, d), lambda i, pos: (i, 0))
    vec = pl.BlockSpec((1, d), lambda i, pos: (0, 0))
    final_norm = final_gain is not None
    if not final_norm:
        final_gain = jnp.ones((d,), _F32)
    return pl.pallas_call(
        functools.partial(_combine_kernel, final_norm=final_norm),
        grid_spec=pltpu.PrefetchScalarGridSpec(
            num_scalar_prefetch=1, grid=(n_tok // tb,),
            in_specs=[pl.BlockSpec(memory_space=pl.ANY),
                      pl.BlockSpec((tb, TOP_K), lambda i, pos: (i, 0)),
                      blk, blk, vec, vec],
            out_specs=blk,
            scratch_shapes=[pltpu.VMEM((2, TOP_K, tb, c), jnp.uint32), pltpu.SemaphoreType.DMA((2,))],
        ),
        out_shape=jax.ShapeDtypeStruct((n_tok, d), _F32),
        compiler_params=_cparams("arbitrary"), name="moe_combine",
    )(pos_flat, ys, gate8, x, shared, gate2[None, :], final_gain[None, :])


def _ffn_kernel(t_ref, wg_ref, wu_ref, wd_ref, o_ref):
    t = t_ref[...]
    h = _silu(jnp.dot(t, wg_ref[...], preferred_element_type=_F32)) * jnp.dot(t, wu_ref[...],
                                                                               preferred_element_type=_F32)
    o_ref[...] = jnp.dot(h.astype(_BF16), wd_ref[...], preferred_element_type=_F32)


def _ffn(t, w_gate, w_up, w_down):
    n_tok, d = t.shape
    ff = w_gate.shape[1]
    tm = _tile(n_tok, 256)
    blk = pl.BlockSpec((tm, d), lambda i: (i, 0))
    resident = lambda r, c: pl.BlockSpec((r, c), lambda i: (0, 0), pipeline_mode=pl.Buffered(1))
    return pl.pallas_call(
        _ffn_kernel, grid=(n_tok // tm,),
        in_specs=[blk, resident(d, ff), resident(d, ff), resident(ff, d)],
        out_specs=blk,
        out_shape=jax.ShapeDtypeStruct((n_tok, d), _F32),
        compiler_params=_cparams("parallel"), name="shared_ffn",
    )(t, w_gate, w_up, w_down)


def _moe_residual(x, gain, shift, scale, gate2, w_router, router_bias, w_gate, w_up, w_down, layer, ws_gate, ws_up,
                  ws_down, final_gain):
    t, tp = _norm_mod_pack(x, gain, shift, scale)
    idx8, gate8, rank8, counts = _router(t, w_router, router_bias)
    pos_flat, tile_expert, n_active, last_tile_row, n_tiles = _moe_plan(idx8, rank8, counts, _MOE_TILE_ROWS)
    xs = _dispatch(tp, pos_flat, last_tile_row, n_active, n_tiles * _MOE_TILE_ROWS, _MOE_TILE_ROWS)
    ys = _grouped_ffn(xs, tile_expert, n_active, w_gate, w_up, w_down, layer, _MOE_TILE_ROWS)
    shared = _ffn(t, ws_gate.astype(_BF16), ws_up.astype(_BF16), ws_down.astype(_BF16))
    return _combine(ys, pos_flat, gate8.T, x, shared, gate2, final_gain)


def _attention_layer(x, h_c, mods, mods_c, g_mix, w_in, w_out, rpb, q_gain, k_gain):
    shift1, scale1, gate1 = mods[:3]
    cshift1, cscale1 = mods_c[:2]
    n_tok = x.shape[0]
    na_w = NA_HEADS * HEAD_DIM
    gq_w = GQA_Q_HEADS * HEAD_DIM
    gkv_w = GQA_KV_HEADS * HEAD_DIM
    q_cols = na_w + gq_w
    o_nk, o_gk, o_gv, o_end = q_cols, q_cols + 2 * na_w, q_cols + 2 * na_w + gkv_w, w_in.shape[1]
    tn = int(np.gcd.reduce([na_w, gq_w, gkv_w, 512]))
    tiles = lambda lo, hi: list(range(lo // tn, hi // tn))
    w_bf = w_in.astype(_BF16)

    a_x = _norm_mod(x, g_mix, shift1, scale1, _BF16)
    a_c = _norm_mod(h_c, g_mix, cshift1, cscale1, _BF16)

    kv_plain = tiles(o_nk, o_gk) + tiles(o_gv, o_end)
    p_plain = _matmul_cols(a_x, w_bf, tiles(0, na_w) + kv_plain, tn, _BF16)
    c_plain = _matmul_cols(a_c, w_bf, kv_plain, tn, _BF16)

    cos, sin = _rope_tables(n_tok)
    gains = jnp.concatenate([jnp.tile(q_gain, GQA_Q_HEADS), jnp.tile(k_gain, GQA_KV_HEADS)])
    post = jnp.concatenate([jnp.full((gq_w,), HEAD_DIM ** -0.5 * np.log2(np.e), _F32), jnp.ones((gkv_w,), _F32)])
    qk = _matmul_cols_norm_rope(a_x, w_bf, tiles(na_w, q_cols) + tiles(o_gk, o_gv), tn, gains, post, cos,
                                sin)
    ck = _matmul_cols_norm_rope(a_c, w_bf, tiles(o_gk, o_gv), tn, gains[gq_w:], post[gq_w:], None, None)

    o_a = _na_attention(p_plain, c_plain, rpb, NA_HEADS, 0, NA_HEADS, 2 * NA_HEADS, 0, NA_HEADS)
    k_all = jnp.concatenate([qk[:, gq_w:], ck], axis=0)
    v_all = jnp.concatenate([p_plain[:, 3 * na_w:], c_plain[:, 2 * na_w:]], axis=0)
    o_b = _gqa_attention(qk, k_all, v_all, GQA_KV_HEADS, GQA_Q_HEADS)
    return _matmul2_res(o_a, o_b, w_out.astype(_BF16), x, gate1)


def _pool_layer(x, mods, g_mix, w_pool, pool_scale):
    shift1, scale1, gate1 = mods[:3]
    diffs = _pool_diffs(x, g_mix, shift1, scale1)
    return _matmul(diffs, w_pool.astype(_BF16), grouped=True, res=x, gate=gate1, colscale=pool_scale)


def kernel(x, c, ctx, c_ctx, w_mod, b_mod, g_mix, g_ffn, attn_w_in, attn_w_out, na_rpb, q_gain, k_gain, pool_w,
           pool_scale, moe_w_router, moe_router_bias, moe_w_gate, moe_w_up, moe_w_down, moe_ws_gate, moe_ws_up,
           moe_ws_down, g_final):
    batch, seq, d = x.shape
    depth = w_mod.shape[0]
    outs = []
    for bi in range(batch):
        xb = x[bi]
        h_c = ctx[bi]
        conds = jnp.stack([c[bi], c_ctx])
        for i in range(depth):
            j = i // 2
            mod_rows = _adaln(conds, w_mod, i, b_mod[i])
            mods = jnp.split(mod_rows[0], N_MOD)
            mods_c = jnp.split(mod_rows[1], N_MOD)
            if i % 2 == 0:
                xb = _attention_layer(xb, h_c, mods, mods_c, g_mix[i], attn_w_in[j], attn_w_out[j], na_rpb[j],
                                      q_gain[j], k_gain[j])
            else:
                xb = _pool_layer(xb, mods, g_mix[i], pool_w[j], pool_scale[j])
            assert not any(l % 2 == 0 for l in range(i + 1, depth)), "context-stream update not implemented"
            xb = _moe_residual(xb, g_ffn[i], mods[3], mods[4], mods[5], moe_w_router[i], moe_router_bias[i],
                               moe_w_gate, moe_w_up, moe_w_down, i, moe_ws_gate[i], moe_ws_up[i], moe_ws_down[i],
                               g_final if i == depth - 1 else None)
        outs.append(xb)
    return jnp.stack(outs)
```
